```python
import math
import jax
import jax.numpy as jnp
from jax import lax
import numpy as np

D_MODEL = 2048
BATCH = 4
SEQ = 2048
DEPTH = 4
DEC_BATCH = 32
DEC_SEQ = 8
PAST_LEN = 16384
PAGE_SIZE = 128

N_MIXERS = 4
N_SWA_LAYERS = (DEPTH + 3) // 4
N_SSD_LAYERS = (DEPTH + 2) // 4
N_DIL_LAYERS = (DEPTH + 1) // 4
N_LRU_LAYERS = DEPTH // 4

RMS_EPS = 1e-6
ROPE_THETA = 10000.0
ATTN_BLOCK = 128
NEG_INF = -1e30
D_FF = 4 * D_MODEL

SWA_WINDOW = 128
SWA_HEAD_DIM = 64
SWA_Q_HEADS = 32
SWA_KV_HEADS = 4
SWA_GRP = SWA_Q_HEADS // SWA_KV_HEADS

SSM_D_INNER = 2 * D_MODEL
SSM_HEAD_DIM = 64
SSM_HEADS = SSM_D_INNER // SSM_HEAD_DIM
SSM_GROUPS = 8
SSM_HPG = SSM_HEADS // SSM_GROUPS
SSM_D_STATE = 128
SSM_CONV = 4
SSM_CHUNK = 128
SSM_CONV_DIM = SSM_D_INNER + 2 * SSM_GROUPS * SSM_D_STATE

DIL_PATTERN = ((128, 1), (512, 4), (2048, 16))
DIL_KEYS = ('dil_kv_w128', 'dil_kv_w512', 'dil_kv_w2048')
N_DIL = 3
DIL_HEAD_DIM = 128
DIL_Q_HEADS = 16
DIL_KV_HEADS = 4
DIL_GRP = DIL_Q_HEADS // DIL_KV_HEADS

LRU_WIDTH = D_MODEL
LRU_BLOCKS = 8
LRU_BLOCK_DIM = LRU_WIDTH // LRU_BLOCKS
LRU_CONV = 4
LRU_C = 8.0

STATE_KEYS = ('swa_kv', 'ssd_conv', 'ssd', 'dil_kv_w128', 'dil_kv_w512', 'dil_kv_w2048', 'lru_conv', 'lru')

kernel_name = 'hybrid_swa_ssd_dilated_rglru_adaln_step'


def rmsnorm(x, g):
    xf = x.astype(jnp.float32)
    y = xf * lax.rsqrt(jnp.mean(xf * xf, axis=-1, keepdims=True) + RMS_EPS)
    return (y * g.astype(jnp.float32)).astype(x.dtype)


def rope(x, pos):
    half = x.shape[-1] // 2
    inv = ROPE_THETA ** (-jnp.arange(half, dtype=jnp.float32) / half)
    ang = pos.astype(jnp.float32)[:, None] * inv[None, :]
    cos = jnp.cos(ang)[None, :, None, :]
    sin = jnp.sin(ang)[None, :, None, :]
    xf = x.astype(jnp.float32)
    x1, x2 = xf[..., :half], xf[..., half:]
    return jnp.concatenate([x1 * cos - x2 * sin, x2 * cos + x1 * sin], axis=-1).astype(x.dtype)


def causal_conv(x, buf, w, b):
    k = w.shape[0]
    l = x.shape[1]
    xp = jnp.concatenate([buf.astype(x.dtype), x], axis=1)
    y = b
    for i in range(k):
        y = y + xp[:, i:i + l] * w[i]
    return y, xp[:, l:]


def banded_attn(q, k, v, k_pre, v_pre, pre_valid, window, sinks=None):
    n, l, kvh, grp, hd = q.shape
    bq = min(ATTN_BLOCK, l)
    nb = -(-l // bq)
    pad = nb * bq - l
    q = jnp.pad(q, ((0, 0), (0, pad), (0, 0), (0, 0), (0, 0)))
    kf = jnp.concatenate([k_pre.astype(k.dtype), jnp.pad(k, ((0, 0), (0, pad), (0, 0), (0, 0)))], axis=1)
    vf = jnp.concatenate([v_pre.astype(v.dtype), jnp.pad(v, ((0, 0), (0, pad), (0, 0), (0, 0)))], axis=1)
    valid = jnp.concatenate([pre_valid, jnp.ones((n, nb * bq), bool)], axis=1)
    idx = jnp.arange(nb)[:, None] * bq + jnp.arange(bq + window)[None, :]
    kb = kf[:, idx]
    vb = vf[:, idx]
    vmask = valid[:, idx]
    qb = q.reshape(n, nb, bq, kvh, grp, hd)
    s = jnp.einsum('nbqkgd,nbskd->nbkgqs', qb, kb, preferred_element_type=jnp.float32) * (hd ** -0.5)
    rel = jnp.arange(bq)[:, None] + window - jnp.arange(bq + window)[None, :]
    band = (rel >= 0) & (rel <= window)
    mask = band[None, None, None, None] & vmask[:, :, None, None, None, :]
    s = jnp.where(mask, s, NEG_INF)
    m = jnp.max(s, axis=-1)
    if sinks is not None:
        sk = sinks.astype(jnp.float32).reshape(kvh, grp)[None, None, :, :, None]
        m = jnp.maximum(m, sk)
    p = jnp.exp(s - m[..., None])
    denom = jnp.sum(p, axis=-1)
    if sinks is not None:
        denom = denom + jnp.exp(sk - m)
    o = jnp.einsum('nbkgqs,nbskd->nbqkgd', p, vb.astype(jnp.float32))
    o = o / jnp.moveaxis(denom, -1, 2)[..., None]
    lse = jnp.moveaxis(m + jnp.log(denom), -1, 2)
    o = o.reshape(n, nb * bq, kvh, grp, hd)[:, :l].astype(q.dtype)
    lse = lse.reshape(n, nb * bq, kvh, grp)[:, :l]
    return o, lse


def dilated_attn(q, k, v, k_pre, v_pre, pre_valid, window, dilation, sinks=None):
    n, l = q.shape[:2]
    d = dilation
    lp = -(-l // d) * d

    def padl(a):
        return jnp.pad(a, [(0, 0), (0, lp - l)] + [(0, 0)] * (a.ndim - 2))

    def split(a, length):
        a = a.reshape((n, length // d, d) + a.shape[2:])
        a = jnp.moveaxis(a, 2, 1)
        return a.reshape((n * d, length // d) + a.shape[3:])

    def merge(a):
        a = a.reshape((n, d, lp // d) + a.shape[2:])
        a = jnp.moveaxis(a, 1, 2)
        return a.reshape((n, lp) + a.shape[3:])[:, :l]

    o, lse = banded_attn(split(padl(q), lp), split(padl(k), lp), split(padl(v), lp),
                         split(k_pre, window), split(v_pre, window), split(pre_valid, window),
                         window // d, sinks)
    return merge(o), merge(lse)


def window_prefix(cache_kv, n, rows, kvh, hd, dtype):
    if cache_kv is None:
        z = jnp.zeros((n, rows, kvh, hd), dtype)
        return z, z, jnp.zeros((n, rows), bool)
    have = cache_kv.shape[1]
    kv = jnp.pad(cache_kv, ((0, 0), (rows - have, 0), (0, 0), (0, 0), (0, 0)))
    valid = jnp.broadcast_to(jnp.arange(rows) >= rows - have, (n, rows))
    return kv[:, :, 0], kv[:, :, 1], valid


def swa_mixer(h, pos, cache_kv, prm, j):
    n, l, _ = h.shape
    nq = SWA_Q_HEADS * SWA_HEAD_DIM
    nk = SWA_KV_HEADS * SWA_HEAD_DIM
    q, k, v = jnp.split(h @ prm['swa_w_qkv'][j], [nq, nq + nk], axis=-1)
    q = rope(q.reshape(n, l, SWA_Q_HEADS, SWA_HEAD_DIM), pos)
    k = rope(k.reshape(n, l, SWA_KV_HEADS, SWA_HEAD_DIM), pos)
    v = v.reshape(n, l, SWA_KV_HEADS, SWA_HEAD_DIM)
    k_pre, v_pre, valid = window_prefix(cache_kv, n, SWA_WINDOW, SWA_KV_HEADS, SWA_HEAD_DIM, h.dtype)
    o, _ = dilated_attn(q.reshape(n, l, SWA_KV_HEADS, SWA_GRP, SWA_HEAD_DIM), k, v, k_pre, v_pre, valid,
                        SWA_WINDOW, 1, prm['swa_sinks'][j])
    y = o.reshape(n, l, nq) @ prm['swa_w_o'][j]
    return y, jnp.stack([k, v], axis=2)


def ssd_scan(x, dt, a, b, c, h0):
    n, l, g, hpg, p = x.shape
    q = SSM_CHUNK if l % SSM_CHUNK == 0 else l
    nc = l // q
    xdt = (x * dt[..., None]).reshape(n, nc, q, g, hpg, p)
    b = b.reshape(n, nc, q, g, -1)
    c = c.reshape(n, nc, q, g, -1)
    cs = jnp.cumsum(jnp.moveaxis((dt * a).reshape(n, nc, q, g, hpg), 2, -1), axis=-1)
    causal = jnp.tril(jnp.ones((q, q), bool))
    seg = cs[..., :, None] - cs[..., None, :]
    decay = jnp.where(causal, jnp.exp(jnp.where(causal, seg, 0.0)), 0.0)
    cb = jnp.einsum('nclgs,ncmgs->ncglm', c, b)
    y_diag = jnp.einsum('ncglm,ncghlm,ncmghp->nclghp', cb, decay, xdt)
    to_end = jnp.exp(cs[..., -1:] - cs)
    states = jnp.einsum('ncmgs,ncghm,ncmghp->ncghps', b, to_end, xdt)
    chunk_decay = jnp.exp(cs[..., -1])

    def step(hc, inp):
        st, dc = inp
        return hc * dc[..., None, None] + st, hc

    h_last, h_in = lax.scan(step, h0, (jnp.moveaxis(states, 1, 0), jnp.moveaxis(chunk_decay, 1, 0)))
    h_in = jnp.moveaxis(h_in, 0, 1)
    y_off = jnp.einsum('nclgs,ncghps,ncghl->nclghp', c, h_in, jnp.exp(cs))
    return (y_diag + y_off).reshape(n, l, g, hpg, p), h_last


def ssd_mixer(h, conv_buf, ssm_state, prm, j):
    n, l, _ = h.shape
    z, xbc, dt = jnp.split(h @ prm['ssd_w_in'][j], [SSM_D_INNER, SSM_D_INNER + SSM_CONV_DIM], axis=-1)
    if conv_buf is None:
        conv_buf = jnp.zeros((n, SSM_CONV - 1, SSM_CONV_DIM), h.dtype)
    xbc, new_buf = causal_conv(xbc, conv_buf, prm['ssd_conv_w'][j], prm['ssd_conv_b'][j])
    xbc = jax.nn.silu(xbc)
    xs, bm, cm = jnp.split(xbc, [SSM_D_INNER, SSM_D_INNER + SSM_GROUPS * SSM_D_STATE], axis=-1)
    f32 = jnp.float32
    xs = xs.reshape(n, l, SSM_GROUPS, SSM_HPG, SSM_HEAD_DIM).astype(f32)
    bm = bm.reshape(n, l, SSM_GROUPS, SSM_D_STATE).astype(f32)
    cm = cm.reshape(n, l, SSM_GROUPS, SSM_D_STATE).astype(f32)
    dt = jax.nn.softplus(dt.astype(f32) + prm['ssd_dt_bias'][j].astype(f32)).reshape(n, l, SSM_GROUPS, SSM_HPG)
    a = -jnp.exp(prm['ssd_a_log'][j].astype(f32)).reshape(SSM_GROUPS, SSM_HPG)
    if ssm_state is None:
        h0 = jnp.zeros((n, SSM_GROUPS, SSM_HPG, SSM_HEAD_DIM, SSM_D_STATE), f32)
    else:
        h0 = ssm_state.astype(f32).reshape(n, SSM_GROUPS, SSM_HPG, SSM_HEAD_DIM, SSM_D_STATE)
    y, h_last = ssd_scan(xs, dt, a, bm, cm, h0)
    y = y + xs * prm['ssd_d'][j].astype(f32).reshape(SSM_GROUPS, SSM_HPG, 1)
    y = y.reshape(n, l, SSM_D_INNER) * jax.nn.silu(z.astype(f32))
    yg = y.reshape(n, l, SSM_GROUPS, SSM_D_INNER // SSM_GROUPS)
    yg = yg * lax.rsqrt(jnp.mean(yg * yg, axis=-1, keepdims=True) + RMS_EPS)
    y = (yg.reshape(n, l, SSM_D_INNER) * prm['ssd_norm'][j].astype(f32)).astype(h.dtype)
    new_state = h_last.reshape(n, SSM_HEADS, SSM_HEAD_DIM, SSM_D_STATE).astype(h.dtype)
    return y @ prm['ssd_w_out'][j], new_buf, new_state


def dil_mixer(h, pos, caches, prm, j):
    n, l, _ = h.shape
    nq = N_DIL * DIL_Q_HEADS * DIL_HEAD_DIM
    nk = N_DIL * DIL_KV_HEADS * DIL_HEAD_DIM
    q, k, v = jnp.split(h @ prm['dil_w_qkv'][j], [nq, nq + nk], axis=-1)
    q = rope(q.reshape(n, l, N_DIL * DIL_Q_HEADS, DIL_HEAD_DIM), pos)
    q = q.reshape(n, l, N_DIL, DIL_KV_HEADS, DIL_GRP, DIL_HEAD_DIM)
    k = rope(k.reshape(n, l, N_DIL * DIL_KV_HEADS, DIL_HEAD_DIM), pos).reshape(n, l, N_DIL, DIL_KV_HEADS, DIL_HEAD_DIM)
    v = v.reshape(n, l, N_DIL, DIL_KV_HEADS, DIL_HEAD_DIM)
    outs, lses, new_kv = [], [], []
    for g, (w, d) in enumerate(DIL_PATTERN):
        k_pre, v_pre, valid = window_prefix(None if caches is None else caches[g], n, w, DIL_KV_HEADS,
                                            DIL_HEAD_DIM, h.dtype)
        o, lse = dilated_attn(q[:, :, g], k[:, :, g], v[:, :, g], k_pre, v_pre, valid, w, d)
        outs.append(o.astype(jnp.float32))
        lses.append(lse)
        new_kv.append(jnp.stack([k[:, :, g], v[:, :, g]], axis=2))
    wts = jax.nn.softmax(jnp.stack(lses), axis=0)
    o = jnp.einsum('rnlkg,rnlkgd->nlkgd', wts, jnp.stack(outs))
    y = o.astype(h.dtype).reshape(n, l, DIL_Q_HEADS * DIL_HEAD_DIM) @ prm['dil_w_o'][j]
    return y, new_kv


def lru_mixer(h, conv_buf, h_state, prm, j):
    n, l, _ = h.shape
    f32 = jnp.float32
    gate, xb = jnp.split(h @ prm['lru_w_in'][j] + prm['lru_b_in'][j], [LRU_WIDTH], axis=-1)
    gate = jax.nn.gelu(gate)
    if conv_buf is None:
        conv_buf = jnp.zeros((n, LRU_CONV - 1, LRU_WIDTH), h.dtype)
    xb, new_buf = causal_conv(xb, conv_buf, prm['lru_conv_w'][j], prm['lru_conv_b'][j])
    xblk = xb.reshape(n, l, LRU_BLOCKS, LRU_BLOCK_DIM)
    r = jax.nn.sigmoid(jnp.einsum('nlbi,bij->nlbj', xblk, prm['lru_w_r'][j]).reshape(n, l, LRU_WIDTH)
                       + prm['lru_b_r'][j]).astype(f32)
    i = jax.nn.sigmoid(jnp.einsum('nlbi,bij->nlbj', xblk, prm['lru_w_i'][j]).reshape(n, l, LRU_WIDTH)
                       + prm['lru_b_i'][j]).astype(f32)
    log_a = -LRU_C * r * jax.nn.softplus(-prm['lru_lam'][j].astype(f32))
    a = jnp.exp(log_a)
    u = jnp.sqrt(-jnp.expm1(2.0 * log_a)) * (i * xb.astype(f32))
    h0 = jnp.zeros((n, LRU_WIDTH), f32) if h_state is None else h_state.astype(f32)
    u = u.at[:, 0].add(a[:, 0] * h0)

    def combine(left, right):
        a1, b1 = left
        a2, b2 = right
        return a1 * a2, a2 * b1 + b2

    _, hs = lax.associative_scan(combine, (a, u), axis=1)
    y = (hs.astype(h.dtype) * gate) @ prm['lru_w_out'][j]
    return y, new_buf, hs[:, -1].astype(h.dtype)


def trunk(x, c, pos, cache, prm):
    prompt = cache is None
    n, l, _ = x.shape
    new = {name: [] for name in STATE_KEYS}
    cond = jax.nn.silu(c)
    for layer in range(DEPTH):
        kind, j = layer % N_MIXERS, layer // N_MIXERS
        mod = cond @ prm['w_ada'][layer] + prm['b_ada'][layer]
        sh_m, sc_m, gt_m, sh_f, sc_f, gt_f = jnp.split(mod[:, None, :], 6, axis=-1)
        h = rmsnorm(x, prm['g_mix'][layer]) * (1.0 + sc_m) + sh_m
        if kind == 0:
            y, kv = swa_mixer(h, pos, None if prompt else cache['swa_kv'][j], prm, j)
            new['swa_kv'].append(kv[:, l - min(SWA_WINDOW, l):] if prompt else kv)
        elif kind == 1:
            y, cb, st = ssd_mixer(h, None if prompt else cache['ssd_conv'][j],
                                  None if prompt else cache['ssd'][j], prm, j)
            new['ssd_conv'].append(cb)
            new['ssd'].append(st)
        elif kind == 2:
            y, kvs = dil_mixer(h, pos, None if prompt else [cache[nm][j] for nm in DIL_KEYS], prm, j)
            for (w, _), nm, kv in zip(DIL_PATTERN, DIL_KEYS, kvs):
                new[nm].append(kv[:, l - min(w, l):] if prompt else kv)
        else:
            y, cb, st = lru_mixer(h, None if prompt else cache['lru_conv'][j],
                                  None if prompt else cache['lru'][j], prm, j)
            new['lru_conv'].append(cb)
            new['lru'].append(st)
        x = x + gt_m * y
        h = rmsnorm(x, prm['g_ffn'][layer]) * (1.0 + sc_f) + sh_f
        x = x + gt_f * (jnp.square(jax.nn.relu(h @ prm['w_ff1'][layer])) @ prm['w_ff2'][layer])
    y = rmsnorm(x, prm['g_final'])
    return y, {name: jnp.stack(v) for name, v in new.items()}


def setup_inputs(seed: int = 0) -> dict:
    key = jax.random.key(seed)
    ks = jax.random.split(key, 64)
    f32 = jnp.float32
    D = D_MODEL

    def nrm(i, shape, scale=1.0):
        return jax.random.normal(ks[i], shape, f32) * scale

    def unif(i, shape, lo, hi):
        return jax.random.uniform(ks[i], shape, f32, lo, hi)

    swa_rows = min(SWA_WINDOW, PAST_LEN)
    dil_rows = [min(w, PAST_LEN) for w, _ in DIL_PATTERN]
    dt0 = jnp.exp(unif(30, (N_SSD_LAYERS, SSM_HEADS), math.log(1e-3), math.log(1e-1)))
    a_pow = unif(40, (N_LRU_LAYERS, LRU_WIDTH), 0.9, 0.999)
    s = a_pow ** (1.0 / LRU_C)
    nq_dil = N_DIL * DIL_Q_HEADS * DIL_HEAD_DIM
    nkv_dil = 2 * N_DIL * DIL_KV_HEADS * DIL_HEAD_DIM
    return {
        'x_prompt': nrm(0, (BATCH, SEQ, D)),
        'x_sample': nrm(1, (DEC_BATCH, DEC_SEQ, D)),
        'cache_swa_kv': nrm(2, (N_SWA_LAYERS, DEC_BATCH, swa_rows, 2, SWA_KV_HEADS, SWA_HEAD_DIM)),
        'state_ssd_conv': nrm(3, (N_SSD_LAYERS, DEC_BATCH, SSM_CONV - 1, SSM_CONV_DIM)),
        'state_ssd': nrm(4, (N_SSD_LAYERS, DEC_BATCH, SSM_HEADS, SSM_HEAD_DIM, SSM_D_STATE), 0.1),
        'cache_dil_kv_w128': nrm(5, (N_DIL_LAYERS, DEC_BATCH, dil_rows[0], 2, DIL_KV_HEADS, DIL_HEAD_DIM)),
        'cache_dil_kv_w512': nrm(6, (N_DIL_LAYERS, DEC_BATCH, dil_rows[1], 2, DIL_KV_HEADS, DIL_HEAD_DIM)),
        'cache_dil_kv_w2048': nrm(7, (N_DIL_LAYERS, DEC_BATCH, dil_rows[2], 2, DIL_KV_HEADS, DIL_HEAD_DIM)),
        'state_lru_conv': nrm(8, (N_LRU_LAYERS, DEC_BATCH, LRU_CONV - 1, LRU_WIDTH)),
        'state_lru': nrm(9, (N_LRU_LAYERS, DEC_BATCH, LRU_WIDTH), 0.5),
        'c_prompt': nrm(10, (BATCH, D)),
        'c_sample': nrm(11, (DEC_BATCH, D)),
        'w_ada': nrm(12, (DEPTH, D, 6 * D), 0.5 * D ** -0.5),
        'b_ada': nrm(13, (DEPTH, 6 * D), 0.02),
        'g_mix': 1.0 + nrm(14, (DEPTH, D), 0.05),
        'g_ffn': 1.0 + nrm(15, (DEPTH, D), 0.05),
        'w_ff1': nrm(16, (DEPTH, D, D_FF), D ** -0.5),
        'w_ff2': nrm(17, (DEPTH, D_FF, D), D_FF ** -0.5),
        'g_final': 1.0 + nrm(18, (D,), 0.05),
        'swa_w_qkv': nrm(19, (N_SWA_LAYERS, D, (SWA_Q_HEADS + 2 * SWA_KV_HEADS) * SWA_HEAD_DIM), D ** -0.5),
        'swa_sinks': nrm(20, (N_SWA_LAYERS, SWA_Q_HEADS), 0.5),
        'swa_w_o': nrm(21, (N_SWA_LAYERS, SWA_Q_HEADS * SWA_HEAD_DIM, D), (SWA_Q_HEADS * SWA_HEAD_DIM) ** -0.5),
        'ssd_w_in': nrm(22, (N_SSD_LAYERS, D, SSM_D_INNER + SSM_CONV_DIM + SSM_HEADS), D ** -0.5),
        'ssd_conv_w': nrm(23, (N_SSD_LAYERS, SSM_CONV, SSM_CONV_DIM), SSM_CONV ** -0.5),
        'ssd_conv_b': nrm(24, (N_SSD_LAYERS, SSM_CONV_DIM), 0.02),
        'ssd_dt_bias': dt0 + jnp.log(-jnp.expm1(-dt0)),
        'ssd_a_log': jnp.log(unif(25, (N_SSD_LAYERS, SSM_HEADS), 1.0, 16.0)),
        'ssd_d': 1.0 + nrm(26, (N_SSD_LAYERS, SSM_HEADS), 0.1),
        'ssd_norm': 1.0 + nrm(27, (N_SSD_LAYERS, SSM_D_INNER), 0.05),
        'ssd_w_out': nrm(28, (N_SSD_LAYERS, SSM_D_INNER, D), SSM_D_INNER ** -0.5),
        'dil_w_qkv': nrm(29, (N_DIL_LAYERS, D, nq_dil + nkv_dil), D ** -0.5),
        'dil_w_o': nrm(31, (N_DIL_LAYERS, DIL_Q_HEADS * DIL_HEAD_DIM, D), (DIL_Q_HEADS * DIL_HEAD_DIM) ** -0.5),
        'lru_w_in': nrm(32, (N_LRU_LAYERS, D, 2 * LRU_WIDTH), D ** -0.5),
        'lru_b_in': nrm(33, (N_LRU_LAYERS, 2 * LRU_WIDTH), 0.02),
        'lru_conv_w': nrm(34, (N_LRU_LAYERS, LRU_CONV, LRU_WIDTH), LRU_CONV ** -0.5),
        'lru_conv_b': nrm(35, (N_LRU_LAYERS, LRU_WIDTH), 0.02),
        'lru_w_r': nrm(36, (N_LRU_LAYERS, LRU_BLOCKS, LRU_BLOCK_DIM, LRU_BLOCK_DIM), LRU_BLOCK_DIM ** -0.5),
        'lru_b_r': nrm(37, (N_LRU_LAYERS, LRU_WIDTH), 0.02),
        'lru_w_i': nrm(38, (N_LRU_LAYERS, LRU_BLOCKS, LRU_BLOCK_DIM, LRU_BLOCK_DIM), LRU_BLOCK_DIM ** -0.5),
        'lru_b_i': nrm(39, (N_LRU_LAYERS, LRU_WIDTH), 0.02),
        'lru_lam': jnp.log(s) - jnp.log1p(-s),
        'lru_w_out': nrm(41, (N_LRU_LAYERS, LRU_WIDTH, D), LRU_WIDTH ** -0.5),
    }


def reference(x_prompt, x_sample, cache_swa_kv, state_ssd_conv, state_ssd, cache_dil_kv_w128, cache_dil_kv_w512,
              cache_dil_kv_w2048, state_lru_conv, state_lru, c_prompt, c_sample, w_ada, b_ada, g_mix, g_ffn,
              w_ff1, w_ff2, g_final, swa_w_qkv, swa_sinks, swa_w_o, ssd_w_in, ssd_conv_w, ssd_conv_b,
              ssd_dt_bias, ssd_a_log, ssd_d, ssd_norm, ssd_w_out, dil_w_qkv, dil_w_o, lru_w_in, lru_b_in,
              lru_conv_w, lru_conv_b, lru_w_r, lru_b_r, lru_w_i, lru_b_i, lru_lam, lru_w_out):
    prm = dict(w_ada=w_ada, b_ada=b_ada, g_mix=g_mix, g_ffn=g_ffn, w_ff1=w_ff1, w_ff2=w_ff2, g_final=g_final,
               swa_w_qkv=swa_w_qkv, swa_sinks=swa_sinks, swa_w_o=swa_w_o,
               ssd_w_in=ssd_w_in, ssd_conv_w=ssd_conv_w, ssd_conv_b=ssd_conv_b, ssd_dt_bias=ssd_dt_bias,
               ssd_a_log=ssd_a_log, ssd_d=ssd_d, ssd_norm=ssd_norm, ssd_w_out=ssd_w_out,
               dil_w_qkv=dil_w_qkv, dil_w_o=dil_w_o,
               lru_w_in=lru_w_in, lru_b_in=lru_b_in, lru_conv_w=lru_conv_w, lru_conv_b=lru_conv_b,
               lru_w_r=lru_w_r, lru_b_r=lru_b_r, lru_w_i=lru_w_i, lru_b_i=lru_b_i, lru_lam=lru_lam,
               lru_w_out=lru_w_out)
    cache = dict(swa_kv=cache_swa_kv, ssd_conv=state_ssd_conv, ssd=state_ssd, dil_kv_w128=cache_dil_kv_w128,
                 dil_kv_w512=cache_dil_kv_w512, dil_kv_w2048=cache_dil_kv_w2048, lru_conv=state_lru_conv,
                 lru=state_lru)
    pos_p = jnp.arange(x_prompt.shape[1], dtype=jnp.int32)
    pos_s = PAST_LEN + jnp.arange(x_sample.shape[1], dtype=jnp.int32)
    y_prompt, sp = trunk(x_prompt, c_prompt, pos_p, None, prm)
    y_sample, ss = trunk(x_sample, c_sample, pos_s, cache, prm)
    return (y_prompt, y_sample,
            sp['swa_kv'], ss['swa_kv'],
            sp['ssd_conv'], ss['ssd_conv'],
            sp['ssd'], ss['ssd'],
            sp['dil_kv_w128'], ss['dil_kv_w128'],
            sp['dil_kv_w512'], ss['dil_kv_w512'],
            sp['dil_kv_w2048'], ss['dil_kv_w2048'],
            sp['lru_conv'], ss['lru_conv'],
            sp['lru'], ss['lru'])
```

```python
import functools
import math

import jax
import jax.numpy as jnp
from jax import lax
from jax.experimental import pallas as pl
from jax.experimental.pallas import tpu as pltpu

F32 = jnp.float32
BF16 = jnp.bfloat16
HI = lax.Precision.HIGHEST

RMS_EPS = 1e-6
ROPE_THETA = 10000.0
NEG_INF = -1e30
LANES = 128
VMEM_LIMIT = 56 * 1024 * 1024

PAST_LEN = 16384
ATTN_BLOCK = 128
SWA_WINDOW, SWA_HD, SWA_QH, SWA_KVH = 128, 64, 32, 4
DIL_PATTERN = ((128, 1), (512, 4), (2048, 16))
DIL_HD, DIL_QH, DIL_KVH = 128, 16, 4
SSM_HEADS, SSM_P, SSM_S, SSM_G, SSM_HPG, SSM_CHUNK = 64, 64, 128, 8, 8, 128
SSM_DI = SSM_HEADS * SSM_P
SSM_GW = SSM_HPG * SSM_P
LRU_BLOCKS, LRU_C = 8, 8.0


def _cparams(*sem):
    return pltpu.CompilerParams(dimension_semantics=sem, vmem_limit_bytes=VMEM_LIMIT)


def _nt_dot(a, b):
    return lax.dot_general(a, b, (((1,), (1,)), ((), ())), preferred_element_type=F32)


def _tn_dot(a, b):
    return lax.dot_general(a, b, (((0,), (0,)), ((), ())), preferred_element_type=F32)


class Seq:
    def __init__(self, n, l, tm, prompt):
        self.n, self.l, self.m, self.tm, self.prompt = n, l, n * l, tm, prompt

    def mod_spec(self, layer, k, d, row_axis):
        tm, l = self.tm, self.l
        if self.prompt:
            return pl.BlockSpec((None, None, 1, d), lambda *g: (layer, (g[row_axis] * tm) // l, 0, k))
        return pl.BlockSpec((None, None, tm, d), lambda *g: (layer, 0, g[row_axis], k))


def _ada_kernel(c_ref, w_ref, b_ref, o_ref):
    c = c_ref[...]
    cond = (c * jax.nn.sigmoid(c)).astype(BF16)
    o_ref[...] = jnp.dot(cond, w_ref[...].astype(BF16), preferred_element_type=F32) + b_ref[...]


def ada_mod(c_all, w_ada, b_ada, tn=1024):
    depth, d, n6 = w_ada.shape
    r = c_all.shape[0]
    return pl.pallas_call(
        _ada_kernel,
        grid=(depth, n6 // tn),
        in_specs=[pl.BlockSpec((r, d), lambda a, j: (0, 0)),
                  pl.BlockSpec((None, d, tn), lambda a, j: (a, 0, j)),
                  pl.BlockSpec((None, 1, tn), lambda a, j: (a, 0, j))],
        out_specs=pl.BlockSpec((None, r, tn), lambda a, j: (a, 0, j)),
        out_shape=jax.ShapeDtypeStruct((depth, r, n6), F32),
        compiler_params=_cparams("arbitrary", "arbitrary"),
        name="ada_mod",
    )(c_all, w_ada, b_ada.reshape(depth, 1, n6))


def _norm_mod_kernel(x_ref, g_ref, sh_ref, sc_ref, o_ref):
    x = x_ref[...]
    y = x * lax.rsqrt(jnp.mean(x * x, axis=-1, keepdims=True) + RMS_EPS) * g_ref[...]
    o_ref[...] = (y * (1.0 + sc_ref[...]) + sh_ref[...]).astype(o_ref.dtype)


def _norm_kernel(x_ref, g_ref, o_ref):
    x = x_ref[...]
    o_ref[...] = x * lax.rsqrt(jnp.mean(x * x, axis=-1, keepdims=True) + RMS_EPS) * g_ref[...]


def norm_mod(seq, x, g, layer, mod, k_shift):
    m, d = x.shape
    tm = min(seq.tm, 512)
    sub = Seq(seq.n, seq.l, tm, seq.prompt)
    return pl.pallas_call(
        _norm_mod_kernel,
        grid=(m // tm,),
        in_specs=[pl.BlockSpec((tm, d), lambda i: (i, 0)),
                  pl.BlockSpec((None, 1, d), lambda i: (layer, 0, 0)),
                  sub.mod_spec(layer, k_shift, d, 0),
                  sub.mod_spec(layer, k_shift + 1, d, 0)],
        out_specs=pl.BlockSpec((tm, d), lambda i: (i, 0)),
        out_shape=jax.ShapeDtypeStruct((m, d), BF16),
        compiler_params=_cparams("arbitrary"),
        name="norm_mod",
    )(x, g.reshape(g.shape[0], 1, d), mod, mod)


def final_norm(x, g, tm):
    m, d = x.shape
    return pl.pallas_call(
        _norm_kernel,
        grid=(m // tm,),
        in_specs=[pl.BlockSpec((tm, d), lambda i: (i, 0)), pl.BlockSpec((1, d), lambda i: (0, 0))],
        out_specs=pl.BlockSpec((tm, d), lambda i: (i, 0)),
        out_shape=jax.ShapeDtypeStruct((m, d), F32),
        compiler_params=_cparams("arbitrary"),
        name="final_norm",
    )(x, g.reshape(1, d))


def _rope_cols(y, cos, sin, hd):
    outs = []
    for c in range(y.shape[1] // LANES):
        yc = y[:, c * LANES:(c + 1) * LANES]
        if hd == LANES:
            partner = pltpu.roll(yc, LANES // 2, axis=1)
        else:
            lane = lax.broadcasted_iota(jnp.int32, yc.shape, 1)
            partner = jnp.where(lane % hd < hd // 2, pltpu.roll(yc, LANES - hd // 2, axis=1),
                                pltpu.roll(yc, hd // 2, axis=1))
        outs.append(yc * cos + partner * sin)
    return outs[0] if len(outs) == 1 else jnp.concatenate(outs, axis=1)


def _mm_kernel(*refs, epi, n_rope, hd, has_bias):
    x_ref, w_ref = refs[0], refs[1]
    o_ref, wb_ref = refs[-2], refs[-1]
    extra = refs[2:-2]

    @pl.when(pl.program_id(1) == 0)
    def _():
        wb_ref[...] = w_ref[...].astype(BF16)

    y = jnp.dot(x_ref[...].astype(BF16), wb_ref[...], preferred_element_type=F32)
    if has_bias:
        y = y + extra[0][...]
        extra = extra[1:]
    if epi == "none":
        o_ref[...] = y.astype(o_ref.dtype)
    elif epi == "softplus":
        o_ref[...] = jax.nn.softplus(y).astype(o_ref.dtype)
    elif epi == "resid":
        res_ref, gate_ref = extra
        o_ref[...] = res_ref[...] + gate_ref[...] * y
    elif epi == "rope":
        cos_ref, sin_ref = extra
        j = pl.program_id(0)

        @pl.when(j < n_rope)
        def _():
            o_ref[...] = _rope_cols(y, cos_ref[...], sin_ref[...], hd).astype(o_ref.dtype)

        @pl.when(j >= n_rope)
        def _():
            o_ref[...] = y.astype(o_ref.dtype)
    else:
        raise ValueError(epi)


def matmul(seq, x, w, layer, *, col0=0, ncols=None, tn, out_dtype, tm=None, bias=None, epi="none",
           rope=None, n_rope=0, hd=LANES, res=None, mod=None, mod_layer=0, mod_k=0, name="mm"):
    m, k = x.shape
    ntot = w.shape[2]
    ncols = ntot - col0 if ncols is None else ncols
    tm = seq.tm if tm is None else tm
    cb = col0 // tn
    assert col0 % tn == 0 and ncols % tn == 0 and m % tm == 0
    in_specs = [pl.BlockSpec((tm, k), lambda j, i: (i, 0)),
                pl.BlockSpec((None, k, tn), lambda j, i: (layer, 0, cb + j))]
    args = [x, w]
    if bias is not None:
        in_specs.append(pl.BlockSpec((None, 1, tn), lambda j, i: (0, 0, cb + j)))
        args.append(bias.reshape(1, 1, -1))
    if epi == "rope":
        cos, sin = rope
        nrt = cos.shape[0] // tm
        for t in (cos, sin):
            in_specs.append(pl.BlockSpec((tm, LANES), lambda j, i: (i % nrt, 0)))
            args.append(t)
    if epi == "resid":
        sub = Seq(seq.n, seq.l, tm, seq.prompt)
        base = sub.mod_spec(mod_layer, mod_k, tn, 1)
        nk = w.shape[2] // tn
        gate_spec = pl.BlockSpec(base.block_shape,
                                 lambda j, i, f=base.index_map: f(j, i)[:3] + (f(j, i)[3] * nk + j,))
        in_specs += [pl.BlockSpec((tm, tn), lambda j, i: (i, j)), gate_spec]
        args += [res, mod]
    return pl.pallas_call(
        functools.partial(_mm_kernel, epi=epi, n_rope=n_rope, hd=hd, has_bias=bias is not None),
        grid=(ncols // tn, m // tm),
        in_specs=in_specs,
        out_specs=pl.BlockSpec((tm, tn), lambda j, i: (i, j)),
        out_shape=jax.ShapeDtypeStruct((m, ncols), out_dtype),
        scratch_shapes=[pltpu.VMEM((k, tn), BF16)],
        compiler_params=_cparams("arbitrary", "arbitrary"),
        name=name,
    )(*args)


def _ffn_kernel(x_ref, w1_ref, w2_ref, res_ref, gate_ref, o_ref, acc_ref):
    j = pl.program_id(1)
    h = jnp.dot(x_ref[...], w1_ref[...].astype(BF16), preferred_element_type=F32)
    h = jnp.square(jnp.maximum(h, 0.0)).astype(BF16)
    part = jnp.dot(h, w2_ref[...].astype(BF16), preferred_element_type=F32)

    @pl.when(j == 0)
    def _():
        acc_ref[...] = part

    @pl.when(j > 0)
    def _():
        acc_ref[...] += part

    @pl.when(j == pl.num_programs(1) - 1)
    def _():
        o_ref[...] = res_ref[...] + gate_ref[...] * acc_ref[...]


def ffn(seq, h, w1, w2, layer, res, mod, tf=512):
    m, d = h.shape
    f = w1.shape[2]
    tm = min(seq.tm, 512)
    sub = Seq(seq.n, seq.l, tm, seq.prompt)
    return pl.pallas_call(
        _ffn_kernel,
        grid=(m // tm, f // tf),
        in_specs=[pl.BlockSpec((tm, d), lambda i, j: (i, 0)),
                  pl.BlockSpec((None, d, tf), lambda i, j: (layer, 0, j)),
                  pl.BlockSpec((None, tf, d), lambda i, j: (layer, j, 0)),
                  pl.BlockSpec((tm, d), lambda i, j: (i, 0)),
                  sub.mod_spec(layer, 5, d, 0)],
        out_specs=pl.BlockSpec((tm, d), lambda i, j: (i, 0)),
        out_shape=jax.ShapeDtypeStruct((m, d), F32),
        scratch_shapes=[pltpu.VMEM((tm, d), F32)],
        compiler_params=_cparams("arbitrary", "arbitrary"),
        name="ffn",
    )(h, w1, w2, res, mod)


def rope_tables(pos, hd, reps):
    half = hd // 2
    inv = ROPE_THETA ** (-jnp.arange(half, dtype=F32) / half)
    ang = pos.astype(F32)[:, None] * inv[None, :]
    cos = jnp.concatenate([jnp.cos(ang), jnp.cos(ang)], axis=-1)
    sin = jnp.concatenate([-jnp.sin(ang), jnp.sin(ang)], axis=-1)
    lane_reps = LANES // hd
    return jnp.tile(cos, (reps, lane_reps)), jnp.tile(sin, (reps, lane_reps))


def _attn_prompt_kernel(*refs, hd, kvh, grp, scale, has_sinks, want_lse):
    if has_sinks:
        sink_ref, refs = refs[0], refs[1:]
    q_ref, kp_ref, kc_ref, vp_ref, vc_ref = refs[:5]
    o_ref = refs[5]
    lse_ref = refs[6] if want_lse else None
    ub = pl.program_id(2)
    bq = q_ref.shape[0]
    iq = lax.broadcasted_iota(jnp.int32, (bq, 2 * bq), 0)
    jk = lax.broadcasted_iota(jnp.int32, (bq, 2 * bq), 1)
    mask = (jk >= iq) & (jk <= iq + bq) & ((jk >= bq) | (ub > 0))
    lane = lax.broadcasted_iota(jnp.int32, (bq, LANES), 1)
    lse_tile = jnp.zeros((bq, LANES), F32)
    for kh in range(kvh):
        ksl = slice(kh * hd, (kh + 1) * hd)
        kk = jnp.concatenate([kp_ref[:, ksl], kc_ref[:, ksl]], axis=0).astype(BF16)
        vv = jnp.concatenate([vp_ref[:, ksl], vc_ref[:, ksl]], axis=0).astype(BF16)
        for g in range(grp):
            h = kh * grp + g
            qh = q_ref[:, h * hd:(h + 1) * hd]
            s = _nt_dot(qh, kk) * scale
            s = jnp.where(mask, s, NEG_INF)
            mx = jnp.max(s, axis=-1, keepdims=True)
            if has_sinks:
                mx = jnp.maximum(mx, sink_ref[h])
            p = jnp.exp(s - mx)
            den = jnp.sum(p, axis=-1, keepdims=True)
            if has_sinks:
                den = den + jnp.exp(sink_ref[h] - mx)
            o = jnp.dot(p.astype(BF16), vv, preferred_element_type=F32) / den
            o_ref[:, h * hd:(h + 1) * hd] = o.astype(o_ref.dtype)
            if want_lse:
                lse_tile = jnp.where(lane == h, mx + jnp.log(den), lse_tile)
    if want_lse:
        lse_ref[...] = lse_tile


def attn_prompt(n, l, q, kv, *, d, hd, kvh, grp, q_blk, k_blk, v_blk, sinks=None, want_lse=True, out_dtype=F32):
    bq = ATTN_BLOCK
    cq, ck = grp * kvh * hd, kvh * hd
    nq, nk = q.shape[1] // cq, kv.shape[1] // ck
    lu = l // d
    qv = q.reshape(n, lu, d * q.shape[1])
    kvv = kv.reshape(n, lu, d * kv.shape[1])
    prev = lambda u: jnp.maximum(u - 1, 0)
    in_specs = [pl.BlockSpec((None, bq, cq), lambda b, r, u: (b, u, r * nq + q_blk)),
                pl.BlockSpec((None, bq, ck), lambda b, r, u: (b, prev(u), r * nk + k_blk)),
                pl.BlockSpec((None, bq, ck), lambda b, r, u: (b, u, r * nk + k_blk)),
                pl.BlockSpec((None, bq, ck), lambda b, r, u: (b, prev(u), r * nk + v_blk)),
                pl.BlockSpec((None, bq, ck), lambda b, r, u: (b, u, r * nk + v_blk))]
    args = [qv, kvv, kvv, kvv, kvv]
    if sinks is not None:
        in_specs.insert(0, pl.BlockSpec(memory_space=pltpu.SMEM))
        args.insert(0, sinks)
    out_shape = [jax.ShapeDtypeStruct((n, lu, d * cq), out_dtype)]
    out_specs = [pl.BlockSpec((None, bq, cq), lambda b, r, u: (b, u, r))]
    if want_lse:
        out_shape.append(jax.ShapeDtypeStruct((n, lu, d * LANES), F32))
        out_specs.append(pl.BlockSpec((None, bq, LANES), lambda b, r, u: (b, u, r)))
    outs = pl.pallas_call(
        functools.partial(_attn_prompt_kernel, hd=hd, kvh=kvh, grp=grp, scale=hd ** -0.5,
                          has_sinks=sinks is not None, want_lse=want_lse),
        grid=(n, d, lu // bq),
        in_specs=in_specs,
        out_specs=out_specs,
        out_shape=out_shape,
        compiler_params=_cparams("arbitrary", "arbitrary", "arbitrary"),
        name=f"attn_prompt_d{d}",
    )(*args)
    o = outs[0].reshape(n * l, cq)
    return (o, outs[1].reshape(n * l, LANES)) if want_lse else (o, None)


def _attn_sample_kernel(*refs, hd, kvh, grp, scale, d, has_sinks, want_lse):
    if has_sinks:
        sink_ref, refs = refs[0], refs[1:]
    q_ref, kn_ref, vn_ref, kc_ref, vc_ref = refs[:5]
    o_ref = refs[5]
    lse_ref = refs[6] if want_lse else None
    lq = q_ref.shape[0]
    w = kc_ref.shape[0]
    t = lax.broadcasted_iota(jnp.int32, (lq, w + lq), 0)
    jk = lax.broadcasted_iota(jnp.int32, (lq, w + lq), 1)
    dist = t + w - jk
    mask = (dist >= 0) & (dist <= w) & (dist % d == 0)
    lane = lax.broadcasted_iota(jnp.int32, (lq, LANES), 1)
    lse_tile = jnp.zeros((lq, LANES), F32)
    for kh in range(kvh):
        ksl = slice(kh * hd, (kh + 1) * hd)
        kk = jnp.concatenate([kc_ref[:, ksl], kn_ref[:, ksl]], axis=0).astype(BF16)
        vv = jnp.concatenate([vc_ref[:, ksl], vn_ref[:, ksl]], axis=0).astype(BF16)
        for g in range(grp):
            h = kh * grp + g
            qh = q_ref[:, h * hd:(h + 1) * hd].astype(BF16)
            s = _nt_dot(qh, kk) * scale
            s = jnp.where(mask, s, NEG_INF)
            mx = jnp.max(s, axis=-1, keepdims=True)
            if has_sinks:
                mx = jnp.maximum(mx, sink_ref[h])
            p = jnp.exp(s - mx)
            den = jnp.sum(p, axis=-1, keepdims=True)
            if has_sinks:
                den = den + jnp.exp(sink_ref[h] - mx)
            o = jnp.dot(p.astype(BF16), vv, preferred_element_type=F32) / den
            o_ref[:, h * hd:(h + 1) * hd] = o
            if want_lse:
                lse_tile = jnp.where(lane == h, mx + jnp.log(den), lse_tile)
    if want_lse:
        lse_ref[...] = lse_tile


def attn_sample(n, l, q, kv, cache, *, d, hd, kvh, grp, q_blk, k_blk, v_blk, sinks=None, want_lse=True):
    cq, ck = grp * kvh * hd, kvh * hd
    w = cache.shape[1]
    in_specs = [pl.BlockSpec((l, cq), lambda b: (b, q_blk)),
                pl.BlockSpec((l, ck), lambda b: (b, k_blk)),
                pl.BlockSpec((l, ck), lambda b: (b, v_blk)),
                pl.BlockSpec((None, w, ck), lambda b: (b, 0, 0)),
                pl.BlockSpec((None, w, ck), lambda b: (b, 0, 1))]
    args = [q, kv, kv, cache, cache]
    if sinks is not None:
        in_specs.insert(0, pl.BlockSpec(memory_space=pltpu.SMEM))
        args.insert(0, sinks)
    out_shape = [jax.ShapeDtypeStruct((n * l, cq), F32)]
    out_specs = [pl.BlockSpec((l, cq), lambda b: (b, 0))]
    if want_lse:
        out_shape.append(jax.ShapeDtypeStruct((n * l, LANES), F32))
        out_specs.append(pl.BlockSpec((l, LANES), lambda b: (b, 0)))
    outs = pl.pallas_call(
        functools.partial(_attn_sample_kernel, hd=hd, kvh=kvh, grp=grp, scale=hd ** -0.5, d=d,
                          has_sinks=sinks is not None, want_lse=want_lse),
        grid=(n,),
        in_specs=in_specs,
        out_specs=out_specs,
        out_shape=out_shape,
        compiler_params=_cparams("arbitrary"),
        name=f"attn_sample_d{d}",
    )(*args)
    return (outs[0], outs[1]) if want_lse else (outs[0], None)


def _dil_merge_kernel(o0, o1, o2, l0, l1, l2, out_ref, *, qh, hd):
    ls = [r[...] for r in (l0, l1, l2)]
    mx = jnp.maximum(jnp.maximum(ls[0], ls[1]), ls[2])
    es = [jnp.exp(v - mx) for v in ls]
    tot = es[0] + es[1] + es[2]
    row = lax.broadcasted_iota(jnp.int32, (LANES, qh * hd), 0)
    col = lax.broadcasted_iota(jnp.int32, (LANES, qh * hd), 1)
    expand = (col // hd == row).astype(F32)
    acc = None
    for e, o in zip(es, (o0, o1, o2)):
        wt = jnp.dot(e / tot, expand, precision=HI, preferred_element_type=F32)
        acc = wt * o[...] if acc is None else acc + wt * o[...]
    out_ref[...] = acc.astype(out_ref.dtype)


def dil_merge(outs, lses, tm, out_dtype):
    m, c = outs[0].shape
    return pl.pallas_call(
        functools.partial(_dil_merge_kernel, qh=DIL_QH, hd=DIL_HD),
        grid=(m // tm,),
        in_specs=[pl.BlockSpec((tm, c), lambda i: (i, 0))] * 3 + [pl.BlockSpec((tm, LANES), lambda i: (i, 0))] * 3,
        out_specs=pl.BlockSpec((tm, c), lambda i: (i, 0)),
        out_shape=jax.ShapeDtypeStruct((m, c), out_dtype),
        compiler_params=_cparams("arbitrary"),
        name="dil_merge",
    )(*outs, *lses)


def _conv_silu(xp_ref, tail_ref, x_ref, w_ref, b_ref, q):
    xp_ref[0:8, :] = tail_ref[...]
    xp_ref[8:8 + q, :] = x_ref[...]
    y = b_ref[...]
    for i in range(4):
        y = y + xp_ref[5 + i:5 + i + q, :] * w_ref[i:i + 1, :]
    tail_ref[...] = xp_ref[q:q + 8, :]
    return y * jax.nn.sigmoid(y)


def _ssd_kernel(*refs, q, has_state):
    (z_ref, x_ref, b_ref, c_ref, bufx_ref, bufb_ref, bufc_ref, dtc_ref, dtr_ref, alr_ref, alc_ref, dpar_ref,
     cwx_ref, cwb_ref, cwc_ref, cbx_ref, cbb_ref, cbc_ref, nw_ref) = refs[:19]
    refs = refs[19:]
    if has_state:
        h0_ref, refs = refs[0], refs[1:]
    y_ref, st_ref, xpx, xpb, xpc, tlx, tlb, tlc, state, ysc = refs
    c = pl.program_id(2)

    @pl.when(c == 0)
    def _():
        tlx[...] = bufx_ref[...]
        tlb[...] = bufb_ref[...]
        tlc[...] = bufc_ref[...]
        state[...] = h0_ref[...] if has_state else jnp.zeros(state.shape, F32)

    xs = _conv_silu(xpx, tlx, x_ref, cwx_ref, cbx_ref, q)
    bm = _conv_silu(xpb, tlb, b_ref, cwb_ref, cbb_ref, q)
    cm = _conv_silu(xpc, tlc, c_ref, cwc_ref, cbc_ref, q)
    bmb, cmb = bm.astype(BF16), cm.astype(BF16)

    dt_c = dtc_ref[...]
    dt_r = dtr_ref[...]
    a_r = -jnp.exp(alr_ref[...])
    a_c = -jnp.exp(alc_ref[...])
    li = lax.broadcasted_iota(jnp.int32, (q, q), 0)
    mi = lax.broadcasted_iota(jnp.int32, (q, q), 1)
    causal = li >= mi
    cs_c = jnp.dot(causal.astype(F32), dt_c * a_r, precision=HI, preferred_element_type=F32)
    cs_r = jnp.dot(dt_r * a_c, (li <= mi).astype(F32), precision=HI, preferred_element_type=F32)
    cs_last = cs_c[q - 1:q, :]

    hrow = lax.broadcasted_iota(jnp.int32, (SSM_HPG, SSM_GW), 0)
    hcol = lax.broadcasted_iota(jnp.int32, (SSM_HPG, SSM_GW), 1)
    expand = (hcol // SSM_P == hrow).astype(F32)

    def widen(v):
        return jnp.dot(v, expand, precision=HI, preferred_element_type=F32)

    xdt = xs * widen(dt_c)
    xe = (xdt * widen(jnp.exp(cs_last - cs_c))).astype(BF16)
    xdtb = xdt.astype(BF16)
    cb = _nt_dot(cmb, bmb)
    st = state[...]
    y_off = _nt_dot(cmb, st.astype(BF16)) * widen(jnp.exp(cs_c))
    for h in range(SSM_HPG):
        seg = cs_c[:, h:h + 1] - cs_r[h:h + 1, :]
        dec = jnp.where(causal, jnp.exp(jnp.where(causal, seg, 0.0)), 0.0)
        gm = (cb * dec).astype(BF16)
        ysc[:, h * SSM_P:(h + 1) * SSM_P] = jnp.dot(gm, xdtb[:, h * SSM_P:(h + 1) * SSM_P],
                                                    preferred_element_type=F32)
    y = ysc[...] + y_off + xs * widen(jnp.broadcast_to(dpar_ref[...], (8, SSM_HPG)))[0:1, :]
    new_st = _tn_dot(xe, bmb)
    dec_last = jnp.broadcast_to(jnp.exp(cs_r[:, q - 1:q]), (SSM_HPG, SSM_S))
    crow = lax.broadcasted_iota(jnp.int32, (SSM_GW, SSM_HPG), 0)
    ccol = lax.broadcasted_iota(jnp.int32, (SSM_GW, SSM_HPG), 1)
    st = st * jnp.dot((crow // SSM_P == ccol).astype(F32), dec_last, precision=HI, preferred_element_type=F32) + new_st
    state[...] = st
    st_ref[...] = st

    z = z_ref[...]
    y = y * (z * jax.nn.sigmoid(z))
    y = y * lax.rsqrt(jnp.mean(y * y, axis=-1, keepdims=True) + RMS_EPS) * nw_ref[...]
    y_ref[...] = y.astype(y_ref.dtype)


def ssd_core(n, l, zx, dt, conv_buf8, h0, p, out_dtype):
    q = SSM_CHUNK if l % SSM_CHUNK == 0 else l
    nc = l // q
    gw, s, g = SSM_GW, SSM_S, SSM_G
    xb0 = SSM_DI // gw
    bb0 = 2 * SSM_DI // s
    cb0 = bb0 + g
    dt4 = dt.reshape(n, l, g, SSM_HPG)
    dt_c = jnp.transpose(dt4, (0, 2, 1, 3))
    dt_r = jnp.transpose(dt4, (0, 2, 3, 1))
    alog = p["ssd_a_log"].reshape(g, SSM_HPG)
    row = lambda b, gi, c: (b * nc + c)
    in_specs = [
        pl.BlockSpec((q, gw), lambda b, gi, c: (row(b, gi, c), gi)),
        pl.BlockSpec((q, gw), lambda b, gi, c: (row(b, gi, c), xb0 + gi)),
        pl.BlockSpec((q, s), lambda b, gi, c: (row(b, gi, c), bb0 + gi)),
        pl.BlockSpec((q, s), lambda b, gi, c: (row(b, gi, c), cb0 + gi)),
        pl.BlockSpec((None, 8, gw), lambda b, gi, c: (b, 0, gi)),
        pl.BlockSpec((None, 8, s), lambda b, gi, c: (b, 0, SSM_DI // s + gi)),
        pl.BlockSpec((None, 8, s), lambda b, gi, c: (b, 0, SSM_DI // s + g + gi)),
        pl.BlockSpec((None, None, q, SSM_HPG), lambda b, gi, c: (b, gi, c, 0)),
        pl.BlockSpec((None, None, SSM_HPG, q), lambda b, gi, c: (b, gi, 0, c)),
        pl.BlockSpec((None, 1, SSM_HPG), lambda b, gi, c: (gi, 0, 0)),
        pl.BlockSpec((None, SSM_HPG, 1), lambda b, gi, c: (gi, 0, 0)),
        pl.BlockSpec((None, 1, SSM_HPG), lambda b, gi, c: (gi, 0, 0)),
        pl.BlockSpec((4, gw), lambda b, gi, c: (0, gi)),
        pl.BlockSpec((4, s), lambda b, gi, c: (0, SSM_DI // s + gi)),
        pl.BlockSpec((4, s), lambda b, gi, c: (0, SSM_DI // s + g + gi)),
        pl.BlockSpec((1, gw), lambda b, gi, c: (0, gi)),
        pl.BlockSpec((1, s), lambda b, gi, c: (0, SSM_DI // s + gi)),
        pl.BlockSpec((1, s), lambda b, gi, c: (0, SSM_DI // s + g + gi)),
        pl.BlockSpec((1, gw), lambda b, gi, c: (0, gi)),
    ]
    cw = p["ssd_conv_w"][0]
    cbias = p["ssd_conv_b"]
    args = [zx, zx, zx, zx, conv_buf8, conv_buf8, conv_buf8, dt_c, dt_r,
            alog.reshape(g, 1, SSM_HPG), alog.reshape(g, SSM_HPG, 1), p["ssd_d"].reshape(g, 1, SSM_HPG),
            cw, cw, cw, cbias, cbias, cbias, p["ssd_norm"]]
    if h0 is not None:
        in_specs.append(pl.BlockSpec((None, None, gw, s), lambda b, gi, c: (b, gi, 0, 0)))
        args.append(h0.reshape(n, g, gw, s))
    y, st = pl.pallas_call(
        functools.partial(_ssd_kernel, q=q, has_state=h0 is not None),
        grid=(n, g, nc),
        in_specs=in_specs,
        out_specs=[pl.BlockSpec((q, gw), lambda b, gi, c: (row(b, gi, c), gi)),
                   pl.BlockSpec((None, None, gw, s), lambda b, gi, c: (b, gi, 0, 0))],
        out_shape=[jax.ShapeDtypeStruct((n * l, SSM_DI), out_dtype),
                   jax.ShapeDtypeStruct((n, g, gw, s), F32)],
        scratch_shapes=[pltpu.VMEM((8 + q, gw), F32), pltpu.VMEM((8 + q, s), F32), pltpu.VMEM((8 + q, s), F32),
                        pltpu.VMEM((8, gw), F32), pltpu.VMEM((8, s), F32), pltpu.VMEM((8, s), F32),
                        pltpu.VMEM((gw, s), F32), pltpu.VMEM((q, gw), F32)],
        compiler_params=_cparams("arbitrary", "arbitrary", "arbitrary"),
        name="ssd_core",
    )(*args)
    return y, st.reshape(n, SSM_HEADS, SSM_P, SSM_S)


def _lru_kernel(*refs, tl, has_state):
    (gate_ref, xb_ref, buf_ref, cw_ref, cb_ref, wr_ref, wi_ref, br_ref, bi_ref, lam_ref) = refs[:10]
    refs = refs[10:]
    if has_state:
        h0_ref, refs = refs[0], refs[1:]
    y_ref, last_ref, xp, tail, a_sc, u_sc, hs_sc, hcar, wrb, wib = refs
    t = pl.program_id(1)

    @pl.when((pl.program_id(0) == 0) & (t == 0))
    def _():
        wrb[...] = wr_ref[...].astype(BF16)
        wib[...] = wi_ref[...].astype(BF16)

    @pl.when(t == 0)
    def _():
        tail[...] = buf_ref[...]
        hcar[...] = h0_ref[...] if has_state else jnp.zeros(hcar.shape, F32)

    xp[0:8, :] = tail[...]
    xp[8:8 + tl, :] = xb_ref[...]
    xc = cb_ref[...]
    for i in range(4):
        xc = xc + xp[5 + i:5 + i + tl, :] * cw_ref[i:i + 1, :]
    tail[...] = xp[tl:tl + 8, :]

    xcb = xc.astype(BF16)
    bd = xc.shape[1] // LRU_BLOCKS
    rs, is_ = [], []
    for b in range(LRU_BLOCKS):
        xblk = xcb[:, b * bd:(b + 1) * bd]
        rs.append(jnp.dot(xblk, wrb[b], preferred_element_type=F32))
        is_.append(jnp.dot(xblk, wib[b], preferred_element_type=F32))
    r = jax.nn.sigmoid(jnp.concatenate(rs, axis=1) + br_ref[...])
    ig = jax.nn.sigmoid(jnp.concatenate(is_, axis=1) + bi_ref[...])
    log_a = -LRU_C * r * jax.nn.softplus(-lam_ref[...])
    a_sc[...] = jnp.exp(log_a)
    u_sc[...] = jnp.sqrt(-jnp.tanh(log_a) * (jnp.exp(2.0 * log_a) + 1.0)) * (ig * xc)

    def step(i, h):
        h = a_sc[pl.ds(i, 1), :] * h + u_sc[pl.ds(i, 1), :]
        hs_sc[pl.ds(i, 1), :] = h
        return h

    h = lax.fori_loop(0, tl, step, hcar[...], unroll=8)
    hcar[...] = h
    last_ref[...] = h
    y_ref[...] = (hs_sc[...] * jax.nn.gelu(gate_ref[...])).astype(y_ref.dtype)


def lru_core(n, l, gx, conv_buf8, h0, p, tl, out_dtype):
    wd = gx.shape[1] // 2
    nt = l // tl
    bd = wd // LRU_BLOCKS
    vec = lambda a: a.reshape(1, wd)
    cst2 = lambda b, t: (0, 0)
    in_specs = [pl.BlockSpec((tl, wd), lambda b, t: (b * nt + t, 0)),
                pl.BlockSpec((tl, wd), lambda b, t: (b * nt + t, 1)),
                pl.BlockSpec((None, 8, wd), lambda b, t: (b, 0, 0)),
                pl.BlockSpec((4, wd), cst2), pl.BlockSpec((1, wd), cst2),
                pl.BlockSpec((LRU_BLOCKS, bd, bd), lambda b, t: (0, 0, 0)),
                pl.BlockSpec((LRU_BLOCKS, bd, bd), lambda b, t: (0, 0, 0)),
                pl.BlockSpec((1, wd), cst2), pl.BlockSpec((1, wd), cst2), pl.BlockSpec((1, wd), cst2)]
    args = [gx, gx, conv_buf8, p["lru_conv_w"][0], vec(p["lru_conv_b"]), p["lru_w_r"][0], p["lru_w_i"][0],
            vec(p["lru_b_r"]), vec(p["lru_b_i"]), vec(p["lru_lam"])]
    if h0 is not None:
        in_specs.append(pl.BlockSpec((None, 1, wd), lambda b, t: (b, 0, 0)))
        args.append(h0.reshape(n, 1, wd))
    y, last = pl.pallas_call(
        functools.partial(_lru_kernel, tl=tl, has_state=h0 is not None),
        grid=(n, nt),
        in_specs=in_specs,
        out_specs=[pl.BlockSpec((tl, wd), lambda b, t: (b * nt + t, 0)),
                   pl.BlockSpec((None, 1, wd), lambda b, t: (b, 0, 0))],
        out_shape=[jax.ShapeDtypeStruct((n * l, wd), out_dtype), jax.ShapeDtypeStruct((n, 1, wd), F32)],
        scratch_shapes=[pltpu.VMEM((8 + tl, wd), F32), pltpu.VMEM((8, wd), F32),
                        pltpu.VMEM((tl, wd), F32), pltpu.VMEM((tl, wd), F32), pltpu.VMEM((tl, wd), F32),
                        pltpu.VMEM((1, wd), F32),
                        pltpu.VMEM((LRU_BLOCKS, bd, bd), BF16), pltpu.VMEM((LRU_BLOCKS, bd, bd), BF16)],
        compiler_params=_cparams("arbitrary", "arbitrary"),
        name="lru_core",
    )(*args)
    return y, last.reshape(n, wd)


def _pad_buf8(buf):
    return jnp.pad(buf, ((0, 0), (5, 0), (0, 0)))


def _trunk(seq, x, mod, pos, cache, p):
    n, l, m = seq.n, seq.l, seq.m
    prompt = seq.prompt
    d = x.shape[1]
    act = BF16 if prompt else F32
    new = {}
    reps = 1 if prompt else n

    h = norm_mod(seq, x, p["g_mix"], 0, mod, 0)
    rope64 = rope_tables(pos, SWA_HD, reps)
    nq = SWA_QH * SWA_HD
    q = matmul(seq, h, p["swa_w_qkv"], 0, col0=0, ncols=nq, tn=1024, out_dtype=act, epi="rope", rope=rope64,
               n_rope=nq // 1024, hd=SWA_HD, name="swa_q")
    kv = matmul(seq, h, p["swa_w_qkv"], 0, col0=nq, tn=256, out_dtype=F32, epi="rope", rope=rope64,
                n_rope=1, hd=SWA_HD, name="swa_kv")
    akw = dict(d=1, hd=SWA_HD, kvh=SWA_KVH, grp=SWA_QH // SWA_KVH, q_blk=0, k_blk=0, v_blk=1,
               sinks=p["swa_sinks"][0], want_lse=False)
    if prompt:
        o, _ = attn_prompt(n, l, q, kv, out_dtype=BF16, **akw)
        new["swa_kv"] = kv.reshape(n, l, 2, SWA_KVH, SWA_HD)[:, l - min(SWA_WINDOW, l):]
    else:
        c = cache["swa_kv"][0]
        o, _ = attn_sample(n, l, q, kv, c.reshape(n, c.shape[1], -1), **akw)
        new["swa_kv"] = kv.reshape(n, l, 2, SWA_KVH, SWA_HD)
    x = matmul(seq, o, p["swa_w_o"], 0, tn=512, out_dtype=F32, epi="resid", res=x, mod=mod, mod_layer=0, mod_k=2,
               name="swa_o")
    x = ffn(seq, norm_mod(seq, x, p["g_ffn"], 0, mod, 3), p["w_ff1"], p["w_ff2"], 0, x, mod)

    h = norm_mod(seq, x, p["g_mix"], 1, mod, 0)
    nzx = SSM_DI + SSM_DI + 2 * SSM_G * SSM_S
    zx = matmul(seq, h, p["ssd_w_in"], 0, col0=0, ncols=nzx, tn=1024, out_dtype=F32, name="ssd_in")
    w_dt = p["ssd_w_in"][:, :, nzx:]
    dt = matmul(seq, h, w_dt, 0, tn=SSM_HEADS, out_dtype=F32, bias=p["ssd_dt_bias"], epi="softplus", name="ssd_dt")
    xbc = zx[:, SSM_DI:].reshape(n, l, -1)
    if prompt:
        buf8 = jnp.zeros((n, 8, nzx - SSM_DI), F32)
        h0 = None
        new["ssd_conv"] = xbc[:, l - 3:]
    else:
        buf8 = _pad_buf8(cache["ssd_conv"][0])
        h0 = cache["ssd"][0]
        new["ssd_conv"] = jnp.concatenate([cache["ssd_conv"][0], xbc], axis=1)[:, -3:]
    y, new["ssd"] = ssd_core(n, l, zx, dt, buf8, h0, p, act)
    x = matmul(seq, y, p["ssd_w_out"], 0, tn=512, tm=min(seq.tm, 512), out_dtype=F32, epi="resid", res=x, mod=mod,
               mod_layer=1, mod_k=2, name="ssd_out")
    x = ffn(seq, norm_mod(seq, x, p["g_ffn"], 1, mod, 3), p["w_ff1"], p["w_ff2"], 1, x, mod)

    h = norm_mod(seq, x, p["g_mix"], 2, mod, 0)
    rope128 = rope_tables(pos, DIL_HD, reps)
    ng = len(DIL_PATTERN)
    nq = ng * DIL_QH * DIL_HD
    q = matmul(seq, h, p["dil_w_qkv"], 0, col0=0, ncols=nq, tn=1024, out_dtype=act, epi="rope", rope=rope128,
               n_rope=nq // 1024, hd=DIL_HD, name="dil_q")
    kv = matmul(seq, h, p["dil_w_qkv"], 0, col0=nq, tn=512, out_dtype=F32, epi="rope", rope=rope128,
                n_rope=ng, hd=DIL_HD, name="dil_kv")
    ck = DIL_KVH * DIL_HD
    outs, lses = [], []
    for g, ((w, dil), key) in enumerate(zip(DIL_PATTERN, ("dil_kv_w128", "dil_kv_w512", "dil_kv_w2048"))):
        akw = dict(d=dil, hd=DIL_HD, kvh=DIL_KVH, grp=DIL_QH // DIL_KVH, q_blk=g, k_blk=g, v_blk=ng + g)
        kv_g = jnp.stack([kv[:, g * ck:(g + 1) * ck], kv[:, (ng + g) * ck:(ng + g + 1) * ck]], axis=1)
        kv_g = kv_g.reshape(n, l, 2, DIL_KVH, DIL_HD)
        if prompt:
            o, lse = attn_prompt(n, l, q, kv, **akw)
            new[key] = kv_g[:, l - min(w, l):]
        else:
            c = cache[key][0]
            o, lse = attn_sample(n, l, q, kv, c.reshape(n, c.shape[1], -1), **akw)
            new[key] = kv_g
        outs.append(o)
        lses.append(lse)
    o = dil_merge(outs, lses, seq.tm if not prompt else 512, act)
    x = matmul(seq, o, p["dil_w_o"], 0, tn=512, out_dtype=F32, epi="resid", res=x, mod=mod, mod_layer=2, mod_k=2,
               name="dil_o")
    x = ffn(seq, norm_mod(seq, x, p["g_ffn"], 2, mod, 3), p["w_ff1"], p["w_ff2"], 2, x, mod)

    h = norm_mod(seq, x, p["g_mix"], 3, mod, 0)
    gx = matmul(seq, h, p["lru_w_in"], 0, tn=1024, out_dtype=F32, bias=p["lru_b_in"], name="lru_in")
    wd = gx.shape[1] // 2
    xb = gx[:, wd:].reshape(n, l, wd)
    if prompt:
        buf8 = jnp.zeros((n, 8, wd), F32)
        h0 = None
        new["lru_conv"] = xb[:, l - 3:]
    else:
        buf8 = _pad_buf8(cache["lru_conv"][0])
        h0 = cache["lru"][0]
        new["lru_conv"] = jnp.concatenate([cache["lru_conv"][0], xb], axis=1)[:, -3:]
    y, new["lru"] = lru_core(n, l, gx, buf8, h0, p, 256 if prompt else l, act)
    x = matmul(seq, y, p["lru_w_out"], 0, tn=512, out_dtype=F32, epi="resid", res=x, mod=mod, mod_layer=3, mod_k=2,
               name="lru_out")
    x = ffn(seq, norm_mod(seq, x, p["g_ffn"], 3, mod, 3), p["w_ff1"], p["w_ff2"], 3, x, mod)

    y = final_norm(x, p["g_final"], min(seq.tm, 512))
    return y.reshape(n, l, d), {k: v[None] for k, v in new.items()}


def kernel(x_prompt, x_sample, cache_swa_kv, state_ssd_conv, state_ssd, cache_dil_kv_w128, cache_dil_kv_w512,
           cache_dil_kv_w2048, state_lru_conv, state_lru, c_prompt, c_sample, w_ada, b_ada, g_mix, g_ffn,
           w_ff1, w_ff2, g_final, swa_w_qkv, swa_sinks, swa_w_o, ssd_w_in, ssd_conv_w, ssd_conv_b,
           ssd_dt_bias, ssd_a_log, ssd_d, ssd_norm, ssd_w_out, dil_w_qkv, dil_w_o, lru_w_in, lru_b_in,
           lru_conv_w, lru_conv_b, lru_w_r, lru_b_r, lru_w_i, lru_b_i, lru_lam, lru_w_out):
    p = dict(g_mix=g_mix, g_ffn=g_ffn, w_ff1=w_ff1, w_ff2=w_ff2, g_final=g_final,
             swa_w_qkv=swa_w_qkv, swa_sinks=swa_sinks, swa_w_o=swa_w_o,
             ssd_w_in=ssd_w_in, ssd_conv_w=ssd_conv_w, ssd_conv_b=ssd_conv_b, ssd_dt_bias=ssd_dt_bias,
             ssd_a_log=ssd_a_log, ssd_d=ssd_d, ssd_norm=ssd_norm, ssd_w_out=ssd_w_out,
             dil_w_qkv=dil_w_qkv, dil_w_o=dil_w_o,
             lru_w_in=lru_w_in, lru_b_in=lru_b_in, lru_conv_w=lru_conv_w, lru_conv_b=lru_conv_b,
             lru_w_r=lru_w_r, lru_b_r=lru_b_r, lru_w_i=lru_w_i, lru_b_i=lru_b_i, lru_lam=lru_lam,
             lru_w_out=lru_w_out)
    cache = dict(swa_kv=cache_swa_kv, ssd_conv=state_ssd_conv, ssd=state_ssd, dil_kv_w128=cache_dil_kv_w128,
                 dil_kv_w512=cache_dil_kv_w512, dil_kv_w2048=cache_dil_kv_w2048, lru_conv=state_lru_conv,
                 lru=state_lru)
    nb, l, d = x_prompt.shape
    ns, ls, _ = x_sample.shape
    depth = w_ada.shape[0]
    rows = -(-(nb + ns) // 16) * 16
    c_all = jnp.concatenate([c_prompt, c_sample, jnp.zeros((rows - nb - ns, d), F32)], axis=0)
    mod = ada_mod(c_all, w_ada, b_ada)
    mod_p = mod.reshape(depth, rows, 1, 6 * d)
    mod_s = jnp.repeat(mod[:, nb:nb + ns], ls, axis=1).reshape(depth, 1, ns * ls, 6 * d)

    seq_p = Seq(nb, l, 1024, True)
    seq_s = Seq(ns, ls, ns * ls, False)
    y_p, sp = _trunk(seq_p, x_prompt.reshape(nb * l, d), mod_p, jnp.arange(l, dtype=jnp.int32), None, p)
    y_s, ss = _trunk(seq_s, x_sample.reshape(ns * ls, d), mod_s, PAST_LEN + jnp.arange(ls, dtype=jnp.int32), cache, p)
    return (y_p, y_s,
            sp["swa_kv"], ss["swa_kv"],
            sp["ssd_conv"], ss["ssd_conv"],
            sp["ssd"], ss["ssd"],
            sp["dil_kv_w128"], ss["dil_kv_w128"],
            sp["dil_kv_w512"], ss["dil_kv_w512"],
            sp["dil_kv_w2048"], ss["dil_kv_w2048"],
            sp["lru_conv"], ss["lru_conv"],
            sp["lru"], ss["lru"])
```

```python
import functools
import math

import jax
import jax.numpy as jnp
from jax import lax
from jax.experimental import pallas as pl
from jax.experimental.pallas import tpu as pltpu

F32 = jnp.float32
BF16 = jnp.bfloat16
HI = lax.Precision.HIGHEST

RMS_EPS = 1e-6
ROPE_THETA = 10000.0
NEG_INF = -1e30
LANES = 128
VMEM_LIMIT = 56 * 1024 * 1024

PAST_LEN = 16384
ATTN_BLOCK = 128
SWA_WINDOW, SWA_HD, SWA_QH, SWA_KVH = 128, 64, 32, 4
DIL_PATTERN = ((128, 1), (512, 4), (2048, 16))
DIL_HD, DIL_QH, DIL_KVH = 128, 16, 4
SSM_HEADS, SSM_P, SSM_S, SSM_G, SSM_HPG, SSM_CHUNK = 64, 64, 128, 8, 8, 128
SSM_DI = SSM_HEADS * SSM_P
SSM_GW = SSM_HPG * SSM_P
LRU_BLOCKS, LRU_C = 8, 8.0


def _cparams(*sem):
    return pltpu.CompilerParams(dimension_semantics=sem, vmem_limit_bytes=VMEM_LIMIT)


def _nt_dot(a, b):
    return lax.dot_general(a, b, (((1,), (1,)), ((), ())), preferred_element_type=F32)


def _tn_dot(a, b):
    return lax.dot_general(a, b, (((0,), (0,)), ((), ())), preferred_element_type=F32)


class Seq:
    def __init__(self, n, l, tm, prompt):
        self.n, self.l, self.m, self.tm, self.prompt = n, l, n * l, tm, prompt

    def mod_spec(self, layer, k, d, row_axis):
        tm, l = self.tm, self.l
        if self.prompt:
            return pl.BlockSpec((None, None, 1, d), lambda *g: (layer, (g[row_axis] * tm) // l, 0, k))
        return pl.BlockSpec((None, None, tm, d), lambda *g: (layer, 0, g[row_axis], k))


def _ada_kernel(c_ref, w_ref, b_ref, o_ref):
    c = c_ref[...]
    cond = (c * jax.nn.sigmoid(c)).astype(BF16)
    o_ref[...] = jnp.dot(cond, w_ref[...].astype(BF16), preferred_element_type=F32) + b_ref[...]


def ada_mod(c_all, w_ada, b_ada, tn=1024):
    depth, d, n6 = w_ada.shape
    r = c_all.shape[0]
    return pl.pallas_call(
        _ada_kernel,
        grid=(depth, n6 // tn),
        in_specs=[pl.BlockSpec((r, d), lambda a, j: (0, 0)),
                  pl.BlockSpec((None, d, tn), lambda a, j: (a, 0, j)),
                  pl.BlockSpec((None, 1, tn), lambda a, j: (a, 0, j))],
        out_specs=pl.BlockSpec((None, r, tn), lambda a, j: (a, 0, j)),
        out_shape=jax.ShapeDtypeStruct((depth, r, n6), F32),
        compiler_params=_cparams("arbitrary", "arbitrary"),
        name="ada_mod",
    )(c_all, w_ada, b_ada.reshape(depth, 1, n6))


def _norm_mod_kernel(x_ref, g_ref, sh_ref, sc_ref, o_ref):
    x = x_ref[...]
    y = x * lax.rsqrt(jnp.mean(x * x, axis=-1, keepdims=True) + RMS_EPS) * g_ref[...]
    o_ref[...] = (y * (1.0 + sc_ref[...]) + sh_ref[...]).astype(o_ref.dtype)


def _norm_kernel(x_ref, g_ref, o_ref):
    x = x_ref[...]
    o_ref[...] = x * lax.rsqrt(jnp.mean(x * x, axis=-1, keepdims=True) + RMS_EPS) * g_ref[...]


def norm_mod(seq, x, g, layer, mod, k_shift):
    m, d = x.shape
    tm = min(seq.tm, 512)
    sub = Seq(seq.n, seq.l, tm, seq.prompt)
    return pl.pallas_call(
        _norm_mod_kernel,
        grid=(m // tm,),
        in_specs=[pl.BlockSpec((tm, d), lambda i: (i, 0)),
                  pl.BlockSpec((None, 1, d), lambda i: (layer, 0, 0)),
                  sub.mod_spec(layer, k_shift, d, 0),
                  sub.mod_spec(layer, k_shift + 1, d, 0)],
        out_specs=pl.BlockSpec((tm, d), lambda i: (i, 0)),
        out_shape=jax.ShapeDtypeStruct((m, d), BF16),
        compiler_params=_cparams("arbitrary"),
        name="norm_mod",
    )(x, g.reshape(g.shape[0], 1, d), mod, mod)


def final_norm(x, g, tm):
    m, d = x.shape
    return pl.pallas_call(
        _norm_kernel,
        grid=(m // tm,),
        in_specs=[pl.BlockSpec((tm, d), lambda i: (i, 0)), pl.BlockSpec((1, d), lambda i: (0, 0))],
        out_specs=pl.BlockSpec((tm, d), lambda i: (i, 0)),
        out_shape=jax.ShapeDtypeStruct((m, d), F32),
        compiler_params=_cparams("arbitrary"),
        name="final_norm",
    )(x, g.reshape(1, d))


def _rope_cols(y, cos, sin, hd):
    outs = []
    for c in range(y.shape[1] // LANES):
        yc = y[:, c * LANES:(c + 1) * LANES]
        if hd == LANES:
            partner = pltpu.roll(yc, LANES // 2, axis=1)
        else:
            lane = lax.broadcasted_iota(jnp.int32, yc.shape, 1)
            partner = jnp.where(lane % hd < hd // 2, pltpu.roll(yc, LANES - hd // 2, axis=1),
                                pltpu.roll(yc, hd // 2, axis=1))
        outs.append(yc * cos + partner * sin)
    return outs[0] if len(outs) == 1 else jnp.concatenate(outs, axis=1)


def _mm_kernel(*refs, epi, n_rope, hd, has_bias):
    x_ref, w_ref = refs[0], refs[1]
    o_ref, wb_ref = refs[-2], refs[-1]
    extra = refs[2:-2]

    @pl.when(pl.program_id(1) == 0)
    def _():
        wb_ref[...] = w_ref[...].astype(BF16)

    y = jnp.dot(x_ref[...].astype(BF16), wb_ref[...], preferred_element_type=F32)
    if has_bias:
        y = y + extra[0][...]
        extra = extra[1:]
    if epi == "none":
        o_ref[...] = y.astype(o_ref.dtype)
    elif epi == "softplus":
        o_ref[...] = jax.nn.softplus(y).astype(o_ref.dtype)
    elif epi == "resid":
        res_ref, gate_ref = extra
        o_ref[...] = res_ref[...] + gate_ref[...] * y
    elif epi == "rope":
        cos_ref, sin_ref = extra
        j = pl.program_id(0)

        @pl.when(j < n_rope)
        def _():
            o_ref[...] = _rope_cols(y, cos_ref[...], sin_ref[...], hd).astype(o_ref.dtype)

        @pl.when(j >= n_rope)
        def _():
            o_ref[...] = y.astype(o_ref.dtype)
    else:
        raise ValueError(epi)


def matmul(seq, x, w, layer, *, col0=0, ncols=None, tn, out_dtype, tm=None, bias=None, epi="none",
           rope=None, n_rope=0, hd=LANES, res=None, mod=None, mod_layer=0, mod_k=0, cstride=1, name="mm"):
    m, k = x.shape
    ntot = w.shape[2]
    ncols = ntot - col0 if ncols is None else ncols
    tm = seq.tm if tm is None else tm
    cb = col0 // tn
    assert col0 % tn == 0 and ncols % tn == 0 and m % tm == 0
    in_specs = [pl.BlockSpec((tm, k), lambda j, i: (i, 0)),
                pl.BlockSpec((None, k, tn), lambda j, i: (layer, 0, cb + j * cstride))]
    args = [x, w]
    if bias is not None:
        in_specs.append(pl.BlockSpec((None, 1, tn), lambda j, i: (0, 0, cb + j * cstride)))
        args.append(bias.reshape(1, 1, -1))
    if epi == "rope":
        cos, sin = rope
        nrt = cos.shape[0] // tm
        for t in (cos, sin):
            in_specs.append(pl.BlockSpec((tm, LANES), lambda j, i: (i % nrt, 0)))
            args.append(t)
    if epi == "resid":
        sub = Seq(seq.n, seq.l, tm, seq.prompt)
        base = sub.mod_spec(mod_layer, mod_k, tn, 1)
        nk = w.shape[2] // tn
        gate_spec = pl.BlockSpec(base.block_shape,
                                 lambda j, i, f=base.index_map: f(j, i)[:3] + (f(j, i)[3] * nk + j,))
        in_specs += [pl.BlockSpec((tm, tn), lambda j, i: (i, j)), gate_spec]
        args += [res, mod]
    return pl.pallas_call(
        functools.partial(_mm_kernel, epi=epi, n_rope=n_rope, hd=hd, has_bias=bias is not None),
        grid=(ncols // tn, m // tm),
        in_specs=in_specs,
        out_specs=pl.BlockSpec((tm, tn), lambda j, i: (i, j)),
        out_shape=jax.ShapeDtypeStruct((m, ncols), out_dtype),
        scratch_shapes=[pltpu.VMEM((k, tn), BF16)],
        compiler_params=_cparams("arbitrary", "arbitrary"),
        name=name,
    )(*args)


def _ffn_kernel(x_ref, w1_ref, w2_ref, res_ref, gate_ref, o_ref, h_ref, *, na, tf):
    s = pl.program_id(1)

    @pl.when(s < na)
    def _():
        h = jnp.dot(x_ref[...], w1_ref[...], preferred_element_type=F32)
        col = pl.multiple_of(s * tf, tf)
        h_ref[:, pl.ds(col, tf)] = jnp.square(jnp.maximum(h, 0.0)).astype(BF16)

    @pl.when(s >= na)
    def _():
        y = jnp.dot(h_ref[...], w2_ref[...], preferred_element_type=F32)
        o_ref[...] = res_ref[...] + gate_ref[...] * y


def ffn(seq, h, w1, w2, layer, res, mod, tf=1024, tn=256):
    m, d = h.shape
    f = w1.shape[2]
    tm = seq.tm
    na, nb = f // tf, d // tn
    sub = Seq(seq.n, seq.l, tm, seq.prompt)
    base = sub.mod_spec(layer, 5, tn, 0)
    ocol = lambda s: jnp.maximum(s - na, 0)
    gate_spec = pl.BlockSpec(base.block_shape,
                             lambda i, s, fm=base.index_map: fm(i, s)[:3] + (5 * nb + ocol(s),))
    return pl.pallas_call(
        functools.partial(_ffn_kernel, na=na, tf=tf),
        grid=(m // tm, na + nb),
        in_specs=[pl.BlockSpec((tm, d), lambda i, s: (i, 0)),
                  pl.BlockSpec((None, d, tf), lambda i, s: (layer, 0, jnp.minimum(s, na - 1))),
                  pl.BlockSpec((None, f, tn), lambda i, s: (layer, 0, ocol(s))),
                  pl.BlockSpec((tm, tn), lambda i, s: (i, ocol(s))),
                  gate_spec],
        out_specs=pl.BlockSpec((tm, tn), lambda i, s: (i, ocol(s))),
        out_shape=jax.ShapeDtypeStruct((m, d), F32),
        scratch_shapes=[pltpu.VMEM((tm, f), BF16)],
        compiler_params=_cparams("arbitrary", "arbitrary"),
        name="ffn",
    )(h, w1, w2, res, mod)


def rope_tables(pos, hd, reps):
    half = hd // 2
    inv = ROPE_THETA ** (-jnp.arange(half, dtype=F32) / half)
    ang = pos.astype(F32)[:, None] * inv[None, :]
    cos = jnp.concatenate([jnp.cos(ang), jnp.cos(ang)], axis=-1)
    sin = jnp.concatenate([-jnp.sin(ang), jnp.sin(ang)], axis=-1)
    lane_reps = LANES // hd
    return jnp.tile(cos, (reps, lane_reps)), jnp.tile(sin, (reps, lane_reps))


def _attn_prompt_kernel(*refs, hd, kvh, grp, scale, has_sinks, want_lse):
    if has_sinks:
        sink_ref, refs = refs[0], refs[1:]
    q_ref, kp_ref, kc_ref, vp_ref, vc_ref = refs[:5]
    o_ref = refs[5]
    lse_ref = refs[6] if want_lse else None
    ub = pl.program_id(2)
    bq = q_ref.shape[0]
    iq = lax.broadcasted_iota(jnp.int32, (bq, 2 * bq), 0)
    jk = lax.broadcasted_iota(jnp.int32, (bq, 2 * bq), 1)
    mask = (jk >= iq) & (jk <= iq + bq) & ((jk >= bq) | (ub > 0))
    lane = lax.broadcasted_iota(jnp.int32, (bq, LANES), 1)
    lse_tile = jnp.zeros((bq, LANES), F32)
    for kh in range(kvh):
        ksl = slice(kh * hd, (kh + 1) * hd)
        kk = jnp.concatenate([kp_ref[:, ksl], kc_ref[:, ksl]], axis=0).astype(BF16)
        vv = jnp.concatenate([vp_ref[:, ksl], vc_ref[:, ksl]], axis=0).astype(BF16)
        for g in range(grp):
            h = kh * grp + g
            qh = q_ref[:, h * hd:(h + 1) * hd].astype(BF16)
            s = _nt_dot(qh, kk) * scale
            s = jnp.where(mask, s, NEG_INF)
            mx = jnp.max(s, axis=-1, keepdims=True)
            if has_sinks:
                mx = jnp.maximum(mx, sink_ref[h])
            p = jnp.exp(s - mx)
            den = jnp.sum(p, axis=-1, keepdims=True)
            if has_sinks:
                den = den + jnp.exp(sink_ref[h] - mx)
            o = jnp.dot(p.astype(BF16), vv, preferred_element_type=F32) / den
            o_ref[:, h * hd:(h + 1) * hd] = o.astype(o_ref.dtype)
            if want_lse:
                lse_tile = jnp.where(lane == h, mx + jnp.log(den), lse_tile)
    if want_lse:
        lse_ref[...] = lse_tile


def attn_prompt(n, l, q, kv, *, d, hd, kvh, grp, q_blk, k_blk, v_blk, sinks=None, want_lse=True, out_dtype=F32):
    bq = ATTN_BLOCK
    cq, ck = grp * kvh * hd, kvh * hd
    nq, nk = q.shape[1] // cq, kv.shape[1] // ck
    lu = l // d
    qv = q.reshape(n, lu, d * q.shape[1])
    kvv = kv.reshape(n, lu, d * kv.shape[1])
    prev = lambda u: jnp.maximum(u - 1, 0)
    in_specs = [pl.BlockSpec((None, bq, cq), lambda b, r, u: (b, u, r * nq + q_blk)),
                pl.BlockSpec((None, bq, ck), lambda b, r, u: (b, prev(u), r * nk + k_blk)),
                pl.BlockSpec((None, bq, ck), lambda b, r, u: (b, u, r * nk + k_blk)),
                pl.BlockSpec((None, bq, ck), lambda b, r, u: (b, prev(u), r * nk + v_blk)),
                pl.BlockSpec((None, bq, ck), lambda b, r, u: (b, u, r * nk + v_blk))]
    args = [qv, kvv, kvv, kvv, kvv]
    if sinks is not None:
        in_specs.insert(0, pl.BlockSpec(memory_space=pltpu.SMEM))
        args.insert(0, sinks)
    out_shape = [jax.ShapeDtypeStruct((n, lu, d * cq), out_dtype)]
    out_specs = [pl.BlockSpec((None, bq, cq), lambda b, r, u: (b, u, r))]
    if want_lse:
        out_shape.append(jax.ShapeDtypeStruct((n, lu, d * LANES), F32))
        out_specs.append(pl.BlockSpec((None, bq, LANES), lambda b, r, u: (b, u, r)))
    outs = pl.pallas_call(
        functools.partial(_attn_prompt_kernel, hd=hd, kvh=kvh, grp=grp, scale=hd ** -0.5,
                          has_sinks=sinks is not None, want_lse=want_lse),
        grid=(n, d, lu // bq),
        in_specs=in_specs,
        out_specs=out_specs,
        out_shape=out_shape,
        compiler_params=_cparams("arbitrary", "arbitrary", "arbitrary"),
        name=f"attn_prompt_d{d}",
    )(*args)
    o = outs[0].reshape(n * l, cq)
    return (o, outs[1].reshape(n * l, LANES)) if want_lse else (o, None)


def _attn_dil_kernel(q_ref, kp_ref, kc_ref, vp_ref, vc_ref, o_ref, lse_ref, *, d, grp, scale, has_prev):
    ub, kh, g = pl.program_id(1), pl.program_id(2), pl.program_id(3)
    h = kh * grp + g
    bq = ATTN_BLOCK
    nk = 2 * bq if has_prev else bq
    iq = lax.broadcasted_iota(jnp.int32, (bq, nk), 0)
    jk = lax.broadcasted_iota(jnp.int32, (bq, nk), 1)
    if has_prev:
        mask = (jk >= iq) & (jk <= iq + bq) & ((jk >= bq) | (ub > 0))
    else:
        mask = jk <= iq
    lane = lax.broadcasted_iota(jnp.int32, (bq, LANES), 1)

    @pl.when(h == 0)
    def _():
        lse_ref[...] = jnp.zeros(lse_ref.shape, F32)

    def body(r, carry):
        rows = pl.ds(r, bq, stride=d)
        qh = q_ref[rows, :].astype(BF16)
        if has_prev:
            kk = jnp.concatenate([kp_ref[rows, :], kc_ref[rows, :]], axis=0).astype(BF16)
            vv = jnp.concatenate([vp_ref[rows, :], vc_ref[rows, :]], axis=0).astype(BF16)
        else:
            kk = kc_ref[rows, :].astype(BF16)
            vv = vc_ref[rows, :].astype(BF16)
        s = jnp.where(mask, _nt_dot(qh, kk) * scale, NEG_INF)
        mx = jnp.max(s, axis=-1, keepdims=True)
        p = jnp.exp(s - mx)
        den = jnp.sum(p, axis=-1, keepdims=True)
        o_ref[rows, :] = jnp.dot(p.astype(BF16), vv, preferred_element_type=F32) / den
        lse_ref[rows, :] = jnp.where(lane == h, mx + jnp.log(den), lse_ref[rows, :])
        return carry

    lax.fori_loop(0, d, body, 0)


def attn_dil_prompt(n, l, q, kv, *, d, kvh, grp, q_head0, k_head0, v_head0):
    hd = LANES
    rt = d * ATTN_BLOCK
    nub = l // rt
    has_prev = nub > 1
    prev = lambda u: jnp.maximum(u - 1, 0)
    blk = lambda col, rowf: pl.BlockSpec((rt, hd), lambda b, u, kh, g: (b * nub + rowf(u), col(kh, g)))
    cur = lambda u: u
    o, lse = pl.pallas_call(
        functools.partial(_attn_dil_kernel, d=d, grp=grp, scale=hd ** -0.5, has_prev=has_prev),
        grid=(n, nub, kvh, grp),
        in_specs=[blk(lambda kh, g: q_head0 + kh * grp + g, cur),
                  blk(lambda kh, g: k_head0 + kh, prev), blk(lambda kh, g: k_head0 + kh, cur),
                  blk(lambda kh, g: v_head0 + kh, prev), blk(lambda kh, g: v_head0 + kh, cur)],
        out_specs=[blk(lambda kh, g: kh * grp + g, cur), blk(lambda kh, g: 0, cur)],
        out_shape=[jax.ShapeDtypeStruct((n * l, kvh * grp * hd), F32), jax.ShapeDtypeStruct((n * l, LANES), F32)],
        compiler_params=_cparams("arbitrary", "arbitrary", "arbitrary", "arbitrary"),
        name=f"attn_dil_d{d}",
    )(q, kv, kv, kv, kv)
    return o, lse


def _attn_sample_kernel(*refs, hd, kvh, grp, scale, d, has_sinks, want_lse):
    if has_sinks:
        sink_ref, refs = refs[0], refs[1:]
    q_ref, kn_ref, vn_ref, c_ref = refs[:4]
    o_ref = refs[4]
    lse_ref = refs[5] if want_lse else None
    lq = q_ref.shape[0]
    w = c_ref.shape[0]
    rows = grp * lq
    row = lax.broadcasted_iota(jnp.int32, (rows, w + lq), 0)
    jk = lax.broadcasted_iota(jnp.int32, (rows, w + lq), 1)
    dist = row % lq + w - jk
    mask = (dist >= 0) & (dist <= w) & (dist % d == 0)
    rcol = lax.broadcasted_iota(jnp.int32, (rows, 1), 0)
    lane = lax.broadcasted_iota(jnp.int32, (lq, LANES), 1)
    lse_tile = jnp.zeros((lq, LANES), F32)
    for kh in range(kvh):
        ksl = slice(kh * hd, (kh + 1) * hd)
        kk = jnp.concatenate([c_ref[:, kh, :], kn_ref[:, ksl]], axis=0).astype(BF16)
        vv = jnp.concatenate([c_ref[:, kvh + kh, :], vn_ref[:, ksl]], axis=0).astype(BF16)
        qs = jnp.concatenate([q_ref[:, (kh * grp + g) * hd:(kh * grp + g + 1) * hd] for g in range(grp)], axis=0)
        s = jnp.where(mask, _nt_dot(qs.astype(BF16), kk) * scale, NEG_INF)
        mx = jnp.max(s, axis=-1, keepdims=True)
        if has_sinks:
            sk = jnp.zeros((rows, 1), F32)
            for g in range(grp):
                sk = jnp.where(rcol // lq == g, sink_ref[kh * grp + g], sk)
            mx = jnp.maximum(mx, sk)
        p = jnp.exp(s - mx)
        den = jnp.sum(p, axis=-1, keepdims=True)
        if has_sinks:
            den = den + jnp.exp(sk - mx)
        o = jnp.dot(p.astype(BF16), vv, preferred_element_type=F32) / den
        lse = mx + jnp.log(den)
        for g in range(grp):
            h = kh * grp + g
            o_ref[:, h * hd:(h + 1) * hd] = o[g * lq:(g + 1) * lq, :]
            if want_lse:
                lse_tile = jnp.where(lane == h, lse[g * lq:(g + 1) * lq, :], lse_tile)
    if want_lse:
        lse_ref[...] = lse_tile


def attn_sample(n, l, q, kv, cache, *, d, hd, kvh, grp, q_blk, k_blk, v_blk, sinks=None, want_lse=True):
    cq, ck = grp * kvh * hd, kvh * hd
    w = cache.shape[1]
    in_specs = [pl.BlockSpec((l, cq), lambda b: (b, q_blk)),
                pl.BlockSpec((l, ck), lambda b: (b, k_blk)),
                pl.BlockSpec((l, ck), lambda b: (b, v_blk)),
                pl.BlockSpec((None, w, 2 * kvh, hd), lambda b: (b, 0, 0, 0))]
    args = [q, kv, kv, cache]
    if sinks is not None:
        in_specs.insert(0, pl.BlockSpec(memory_space=pltpu.SMEM))
        args.insert(0, sinks)
    out_shape = [jax.ShapeDtypeStruct((n * l, cq), F32)]
    out_specs = [pl.BlockSpec((l, cq), lambda b: (b, 0))]
    if want_lse:
        out_shape.append(jax.ShapeDtypeStruct((n * l, LANES), F32))
        out_specs.append(pl.BlockSpec((l, LANES), lambda b: (b, 0)))
    outs = pl.pallas_call(
        functools.partial(_attn_sample_kernel, hd=hd, kvh=kvh, grp=grp, scale=hd ** -0.5, d=d,
                          has_sinks=sinks is not None, want_lse=want_lse),
        grid=(n,),
        in_specs=in_specs,
        out_specs=out_specs,
        out_shape=out_shape,
        compiler_params=_cparams("arbitrary"),
        name=f"attn_sample_d{d}",
    )(*args)
    return (outs[0], outs[1]) if want_lse else (outs[0], None)


def _dil_merge_kernel(o0, o1, o2, l0, l1, l2, out_ref, *, qh, hd):
    ls = [r[...] for r in (l0, l1, l2)]
    mx = jnp.maximum(jnp.maximum(ls[0], ls[1]), ls[2])
    es = [jnp.exp(v - mx) for v in ls]
    tot = es[0] + es[1] + es[2]
    wts = [e / tot for e in es]
    for h in range(qh):
        sl = slice(h * hd, (h + 1) * hd)
        acc = wts[0][:, h:h + 1] * o0[:, sl] + wts[1][:, h:h + 1] * o1[:, sl] + wts[2][:, h:h + 1] * o2[:, sl]
        out_ref[:, sl] = acc.astype(out_ref.dtype)


def dil_merge(outs, lses, tm, out_dtype):
    m, c = outs[0].shape
    return pl.pallas_call(
        functools.partial(_dil_merge_kernel, qh=DIL_QH, hd=DIL_HD),
        grid=(m // tm,),
        in_specs=[pl.BlockSpec((tm, c), lambda i: (i, 0))] * 3 + [pl.BlockSpec((tm, LANES), lambda i: (i, 0))] * 3,
        out_specs=pl.BlockSpec((tm, c), lambda i: (i, 0)),
        out_shape=jax.ShapeDtypeStruct((m, c), out_dtype),
        compiler_params=_cparams("arbitrary"),
        name="dil_merge",
    )(*outs, *lses)


def _conv_silu(xp_ref, tail_ref, x_ref, w_ref, b_ref, q):
    xp_ref[0:8, :] = tail_ref[...]
    xp_ref[8:8 + q, :] = x_ref[...]
    y = b_ref[...]
    for i in range(4):
        y = y + xp_ref[5 + i:5 + i + q, :] * w_ref[i:i + 1, :]
    tail_ref[...] = xp_ref[q:q + 8, :]
    return y * jax.nn.sigmoid(y)


def _ssd_kernel(*refs, q, has_state):
    (z_ref, x_ref, b_ref, c_ref, bufx_ref, bufb_ref, bufc_ref, dtc_ref, dtr_ref, alr_ref, alc_ref, dpar_ref,
     cwx_ref, cwb_ref, cwc_ref, cbx_ref, cbb_ref, cbc_ref, nw_ref) = refs[:19]
    refs = refs[19:]
    if has_state:
        h0_ref, refs = refs[0], refs[1:]
    y_ref, st_ref, xpx, xpb, xpc, tlx, tlb, tlc, state, ysc = refs
    c = pl.program_id(2)

    @pl.when(c == 0)
    def _():
        tlx[...] = bufx_ref[...]
        tlb[...] = bufb_ref[...]
        tlc[...] = bufc_ref[...]
        state[...] = h0_ref[...] if has_state else jnp.zeros(state.shape, F32)

    xs = _conv_silu(xpx, tlx, x_ref, cwx_ref, cbx_ref, q)
    bm = _conv_silu(xpb, tlb, b_ref, cwb_ref, cbb_ref, q)
    cm = _conv_silu(xpc, tlc, c_ref, cwc_ref, cbc_ref, q)
    bmb, cmb = bm.astype(BF16), cm.astype(BF16)

    dt_c = dtc_ref[...]
    dt_r = dtr_ref[...]
    a_r = -jnp.exp(alr_ref[...])
    a_c = -jnp.exp(alc_ref[...])
    li = lax.broadcasted_iota(jnp.int32, (q, q), 0)
    mi = lax.broadcasted_iota(jnp.int32, (q, q), 1)
    causal = li >= mi
    cs_c = jnp.dot(causal.astype(F32), dt_c * a_r, precision=HI, preferred_element_type=F32)
    cs_r = jnp.dot(dt_r * a_c, (li <= mi).astype(F32), precision=HI, preferred_element_type=F32)
    cs_last = cs_c[q - 1:q, :]

    low = lax.broadcasted_iota(jnp.int32, (q, LANES), 1) < SSM_P

    def widen(v):
        return jnp.concatenate([jnp.where(low, v[:, 2 * c:2 * c + 1], v[:, 2 * c + 1:2 * c + 2])
                                for c in range(SSM_GW // LANES)], axis=1)

    xdt = xs * widen(dt_c)
    xe = (xdt * widen(jnp.exp(cs_last - cs_c))).astype(BF16)
    xdtb = xdt.astype(BF16)
    cb = _nt_dot(cmb, bmb)
    st = state[...]
    y_off = _nt_dot(cmb, st.astype(BF16)) * widen(jnp.exp(cs_c))
    for h in range(SSM_HPG):
        seg = cs_c[:, h:h + 1] - cs_r[h:h + 1, :]
        gm = (cb * jnp.exp(jnp.where(causal, seg, NEG_INF))).astype(BF16)
        ysc[:, h * SSM_P:(h + 1) * SSM_P] = jnp.dot(gm, xdtb[:, h * SSM_P:(h + 1) * SSM_P],
                                                    preferred_element_type=F32)
    y = ysc[...] + y_off + xs * dpar_ref[...]
    new_st = _tn_dot(xe, bmb)
    dec_last = jnp.exp(cs_r[:, q - 1:q])
    st = jnp.concatenate([st[h * SSM_P:(h + 1) * SSM_P, :] * dec_last[h:h + 1, :] for h in range(SSM_HPG)],
                         axis=0) + new_st
    state[...] = st
    st_ref[...] = st

    z = z_ref[...]
    y = y * (z * jax.nn.sigmoid(z))
    y = y * lax.rsqrt(jnp.mean(y * y, axis=-1, keepdims=True) + RMS_EPS) * nw_ref[...]
    y_ref[...] = y.astype(y_ref.dtype)


def ssd_core(n, l, zx, dt, conv_buf8, h0, p, out_dtype):
    q = SSM_CHUNK if l % SSM_CHUNK == 0 else l
    nc = l // q
    gw, s, g = SSM_GW, SSM_S, SSM_G
    xb0 = SSM_DI // gw
    bb0 = 2 * SSM_DI // s
    cb0 = bb0 + g
    dt4 = dt.reshape(n, l, g, SSM_HPG)
    dt_c = jnp.transpose(dt4, (0, 2, 1, 3))
    dt_r = jnp.transpose(dt4, (0, 2, 3, 1))
    alog = p["ssd_a_log"].reshape(g, SSM_HPG)
    row = lambda b, gi, c: (b * nc + c)
    in_specs = [
        pl.BlockSpec((q, gw), lambda b, gi, c: (row(b, gi, c), gi)),
        pl.BlockSpec((q, gw), lambda b, gi, c: (row(b, gi, c), xb0 + gi)),
        pl.BlockSpec((q, s), lambda b, gi, c: (row(b, gi, c), bb0 + gi)),
        pl.BlockSpec((q, s), lambda b, gi, c: (row(b, gi, c), cb0 + gi)),
        pl.BlockSpec((None, 8, gw), lambda b, gi, c: (b, 0, gi)),
        pl.BlockSpec((None, 8, s), lambda b, gi, c: (b, 0, SSM_DI // s + gi)),
        pl.BlockSpec((None, 8, s), lambda b, gi, c: (b, 0, SSM_DI // s + g + gi)),
        pl.BlockSpec((None, None, q, SSM_HPG), lambda b, gi, c: (b, gi, c, 0)),
        pl.BlockSpec((None, None, SSM_HPG, q), lambda b, gi, c: (b, gi, 0, c)),
        pl.BlockSpec((None, 1, SSM_HPG), lambda b, gi, c: (gi, 0, 0)),
        pl.BlockSpec((None, SSM_HPG, 1), lambda b, gi, c: (gi, 0, 0)),
        pl.BlockSpec((1, gw), lambda b, gi, c: (0, gi)),
        pl.BlockSpec((4, gw), lambda b, gi, c: (0, gi)),
        pl.BlockSpec((4, s), lambda b, gi, c: (0, SSM_DI // s + gi)),
        pl.BlockSpec((4, s), lambda b, gi, c: (0, SSM_DI // s + g + gi)),
        pl.BlockSpec((1, gw), lambda b, gi, c: (0, gi)),
        pl.BlockSpec((1, s), lambda b, gi, c: (0, SSM_DI // s + gi)),
        pl.BlockSpec((1, s), lambda b, gi, c: (0, SSM_DI // s + g + gi)),
        pl.BlockSpec((1, gw), lambda b, gi, c: (0, gi)),
    ]
    cw = p["ssd_conv_w"][0]
    cbias = p["ssd_conv_b"]
    args = [zx, zx, zx, zx, conv_buf8, conv_buf8, conv_buf8, dt_c, dt_r,
            alog.reshape(g, 1, SSM_HPG), alog.reshape(g, SSM_HPG, 1),
            jnp.repeat(p["ssd_d"].reshape(-1), SSM_P).reshape(1, SSM_DI),
            cw, cw, cw, cbias, cbias, cbias, p["ssd_norm"]]
    if h0 is not None:
        in_specs.append(pl.BlockSpec((None, None, gw, s), lambda b, gi, c: (b, gi, 0, 0)))
        args.append(h0.reshape(n, g, gw, s))
    y, st = pl.pallas_call(
        functools.partial(_ssd_kernel, q=q, has_state=h0 is not None),
        grid=(n, g, nc),
        in_specs=in_specs,
        out_specs=[pl.BlockSpec((q, gw), lambda b, gi, c: (row(b, gi, c), gi)),
                   pl.BlockSpec((None, None, gw, s), lambda b, gi, c: (b, gi, 0, 0))],
        out_shape=[jax.ShapeDtypeStruct((n * l, SSM_DI), out_dtype),
                   jax.ShapeDtypeStruct((n, g, gw, s), F32)],
        scratch_shapes=[pltpu.VMEM((8 + q, gw), F32), pltpu.VMEM((8 + q, s), F32), pltpu.VMEM((8 + q, s), F32),
                        pltpu.VMEM((8, gw), F32), pltpu.VMEM((8, s), F32), pltpu.VMEM((8, s), F32),
                        pltpu.VMEM((gw, s), F32), pltpu.VMEM((q, gw), F32)],
        compiler_params=_cparams("arbitrary", "arbitrary", "arbitrary"),
        name="ssd_core",
    )(*args)
    return y, st.reshape(n, SSM_HEADS, SSM_P, SSM_S)


def _lru_kernel(*refs, tl, has_state):
    (gate_ref, xb_ref, buf_ref, cw_ref, cb_ref, wr_ref, wi_ref, br_ref, bi_ref, lam_ref) = refs[:10]
    refs = refs[10:]
    if has_state:
        h0_ref, refs = refs[0], refs[1:]
    y_ref, last_ref, xp, tail, a_sc, u_sc, hs_sc, hcar, wrb, wib = refs
    t = pl.program_id(1)

    @pl.when((pl.program_id(0) == 0) & (t == 0))
    def _():
        wrb[...] = wr_ref[...].astype(BF16)
        wib[...] = wi_ref[...].astype(BF16)

    @pl.when(t == 0)
    def _():
        tail[...] = buf_ref[...]
        hcar[...] = h0_ref[...] if has_state else jnp.zeros(hcar.shape, F32)

    xp[0:8, :] = tail[...]
    xp[8:8 + tl, :] = xb_ref[...]
    xc = cb_ref[...]
    for i in range(4):
        xc = xc + xp[5 + i:5 + i + tl, :] * cw_ref[i:i + 1, :]
    tail[...] = xp[tl:tl + 8, :]

    xcb = xc.astype(BF16)
    bd = xc.shape[1] // LRU_BLOCKS
    rs, is_ = [], []
    for b in range(LRU_BLOCKS):
        xblk = xcb[:, b * bd:(b + 1) * bd]
        rs.append(jnp.dot(xblk, wrb[b], preferred_element_type=F32))
        is_.append(jnp.dot(xblk, wib[b], preferred_element_type=F32))
    r = jax.nn.sigmoid(jnp.concatenate(rs, axis=1) + br_ref[...])
    ig = jax.nn.sigmoid(jnp.concatenate(is_, axis=1) + bi_ref[...])
    log_a = -LRU_C * r * jax.nn.softplus(-lam_ref[...])
    a_sc[...] = jnp.exp(log_a)
    u_sc[...] = jnp.sqrt(-jnp.tanh(log_a) * (jnp.exp(2.0 * log_a) + 1.0)) * (ig * xc)

    def step(i, h):
        h = a_sc[pl.ds(i, 1), :] * h + u_sc[pl.ds(i, 1), :]
        hs_sc[pl.ds(i, 1), :] = h
        return h

    h = lax.fori_loop(0, tl, step, hcar[...], unroll=8)
    hcar[...] = h
    last_ref[...] = h
    y_ref[...] = (hs_sc[...] * jax.nn.gelu(gate_ref[...])).astype(y_ref.dtype)


def lru_core(n, l, gx, conv_buf8, h0, p, tl, out_dtype):
    wd = gx.shape[1] // 2
    nt = l // tl
    bd = wd // LRU_BLOCKS
    vec = lambda a: a.reshape(1, wd)
    cst2 = lambda b, t: (0, 0)
    in_specs = [pl.BlockSpec((tl, wd), lambda b, t: (b * nt + t, 0)),
                pl.BlockSpec((tl, wd), lambda b, t: (b * nt + t, 1)),
                pl.BlockSpec((None, 8, wd), lambda b, t: (b, 0, 0)),
                pl.BlockSpec((4, wd), cst2), pl.BlockSpec((1, wd), cst2),
                pl.BlockSpec((LRU_BLOCKS, bd, bd), lambda b, t: (0, 0, 0)),
                pl.BlockSpec((LRU_BLOCKS, bd, bd), lambda b, t: (0, 0, 0)),
                pl.BlockSpec((1, wd), cst2), pl.BlockSpec((1, wd), cst2), pl.BlockSpec((1, wd), cst2)]
    args = [gx, gx, conv_buf8, p["lru_conv_w"][0], vec(p["lru_conv_b"]), p["lru_w_r"][0], p["lru_w_i"][0],
            vec(p["lru_b_r"]), vec(p["lru_b_i"]), vec(p["lru_lam"])]
    if h0 is not None:
        in_specs.append(pl.BlockSpec((None, 1, wd), lambda b, t: (b, 0, 0)))
        args.append(h0.reshape(n, 1, wd))
    y, last = pl.pallas_call(
        functools.partial(_lru_kernel, tl=tl, has_state=h0 is not None),
        grid=(n, nt),
        in_specs=in_specs,
        out_specs=[pl.BlockSpec((tl, wd), lambda b, t: (b * nt + t, 0)),
                   pl.BlockSpec((None, 1, wd), lambda b, t: (b, 0, 0))],
        out_shape=[jax.ShapeDtypeStruct((n * l, wd), out_dtype), jax.ShapeDtypeStruct((n, 1, wd), F32)],
        scratch_shapes=[pltpu.VMEM((8 + tl, wd), F32), pltpu.VMEM((8, wd), F32),
                        pltpu.VMEM((tl, wd), F32), pltpu.VMEM((tl, wd), F32), pltpu.VMEM((tl, wd), F32),
                        pltpu.VMEM((1, wd), F32),
                        pltpu.VMEM((LRU_BLOCKS, bd, bd), BF16), pltpu.VMEM((LRU_BLOCKS, bd, bd), BF16)],
        compiler_params=_cparams("arbitrary", "arbitrary"),
        name="lru_core",
    )(*args)
    return y, last.reshape(n, wd)


def _pad_buf8(buf):
    return jnp.pad(buf, ((0, 0), (5, 0), (0, 0)))


def _trunk(seq, x, mod, pos, cache, p):
    n, l, m = seq.n, seq.l, seq.m
    prompt = seq.prompt
    d = x.shape[1]
    act = BF16 if prompt else F32
    new = {}
    reps = 1 if prompt else n

    h = norm_mod(seq, x, p["g_mix"], 0, mod, 0)
    rope64 = rope_tables(pos, SWA_HD, reps)
    nq = SWA_QH * SWA_HD
    q = matmul(seq, h, p["swa_w_qkv"], 0, col0=0, ncols=nq, tn=1024, out_dtype=act, epi="rope", rope=rope64,
               n_rope=nq // 1024, hd=SWA_HD, name="swa_q")
    kv = matmul(seq, h, p["swa_w_qkv"], 0, col0=nq, tn=256, out_dtype=F32, epi="rope", rope=rope64,
                n_rope=1, hd=SWA_HD, name="swa_kv")
    akw = dict(d=1, hd=SWA_HD, kvh=SWA_KVH, grp=SWA_QH // SWA_KVH, q_blk=0, k_blk=0, v_blk=1,
               sinks=p["swa_sinks"][0], want_lse=False)
    if prompt:
        o, _ = attn_prompt(n, l, q, kv, out_dtype=BF16, **akw)
        new["swa_kv"] = kv.reshape(n, l, 2, SWA_KVH, SWA_HD)[:, l - min(SWA_WINDOW, l):]
    else:
        c = cache["swa_kv"][0]
        o, _ = attn_sample(n, l, q, kv, c.reshape(n, c.shape[1], 2 * SWA_KVH, SWA_HD), **akw)
        new["swa_kv"] = kv.reshape(n, l, 2, SWA_KVH, SWA_HD)
    x = matmul(seq, o, p["swa_w_o"], 0, tn=512, out_dtype=F32, epi="resid", res=x, mod=mod, mod_layer=0, mod_k=2,
               name="swa_o")
    x = ffn(seq, norm_mod(seq, x, p["g_ffn"], 0, mod, 3), p["w_ff1"], p["w_ff2"], 0, x, mod)

    h = norm_mod(seq, x, p["g_mix"], 1, mod, 0)
    nzx = SSM_DI + SSM_DI + 2 * SSM_G * SSM_S
    zx = matmul(seq, h, p["ssd_w_in"], 0, col0=0, ncols=nzx, tn=1024, out_dtype=F32, name="ssd_in")
    w_dt = p["ssd_w_in"][:, :, nzx:]
    dt = matmul(seq, h, w_dt, 0, tn=SSM_HEADS, out_dtype=F32, bias=p["ssd_dt_bias"], epi="softplus", name="ssd_dt")
    zx3 = zx.reshape(n, l, nzx)
    if prompt:
        buf8 = jnp.zeros((n, 8, nzx - SSM_DI), F32)
        h0 = None
        new["ssd_conv"] = zx3[:, l - 3:, SSM_DI:]
    else:
        buf8 = _pad_buf8(cache["ssd_conv"][0])
        h0 = cache["ssd"][0]
        new["ssd_conv"] = jnp.concatenate([cache["ssd_conv"][0], zx3[:, :, SSM_DI:]], axis=1)[:, -3:]
    y, new["ssd"] = ssd_core(n, l, zx, dt, buf8, h0, p, act)
    x = matmul(seq, y, p["ssd_w_out"], 0, tn=512, tm=min(seq.tm, 512), out_dtype=F32, epi="resid", res=x, mod=mod,
               mod_layer=1, mod_k=2, name="ssd_out")
    x = ffn(seq, norm_mod(seq, x, p["g_ffn"], 1, mod, 3), p["w_ff1"], p["w_ff2"], 1, x, mod)

    h = norm_mod(seq, x, p["g_mix"], 2, mod, 0)
    rope128 = rope_tables(pos, DIL_HD, reps)
    ng = len(DIL_PATTERN)
    nq = ng * DIL_QH * DIL_HD
    q = matmul(seq, h, p["dil_w_qkv"], 0, col0=0, ncols=nq, tn=1024, out_dtype=F32, epi="rope", rope=rope128,
               n_rope=nq // 1024, hd=DIL_HD, name="dil_q")
    ck = DIL_KVH * DIL_HD
    grp = DIL_QH // DIL_KVH
    outs, lses = [], []
    for g, ((w, dil), key) in enumerate(zip(DIL_PATTERN, ("dil_kv_w128", "dil_kv_w512", "dil_kv_w2048"))):
        kv = matmul(seq, h, p["dil_w_qkv"], 0, col0=nq + g * ck, ncols=2 * ck, cstride=ng, tn=ck, out_dtype=F32,
                    epi="rope", rope=rope128, n_rope=1, hd=DIL_HD, name="dil_kv")
        akw = dict(d=dil, hd=DIL_HD, kvh=DIL_KVH, grp=grp, q_blk=g, k_blk=0, v_blk=1)
        kv_g = kv.reshape(n, l, 2, DIL_KVH, DIL_HD)
        if not prompt:
            c = cache[key][0]
            o, lse = attn_sample(n, l, q, kv, c.reshape(n, c.shape[1], 2 * DIL_KVH, DIL_HD), **akw)
            new[key] = kv_g
        else:
            if dil == 1:
                o, lse = attn_prompt(n, l, q, kv, **akw)
            else:
                o, lse = attn_dil_prompt(n, l, q, kv, d=dil, kvh=DIL_KVH, grp=grp, q_head0=g * DIL_QH,
                                         k_head0=0, v_head0=DIL_KVH)
            new[key] = kv_g[:, l - min(w, l):]
        outs.append(o)
        lses.append(lse)
    o = dil_merge(outs, lses, seq.tm if not prompt else 512, act)
    x = matmul(seq, o, p["dil_w_o"], 0, tn=512, out_dtype=F32, epi="resid", res=x, mod=mod, mod_layer=2, mod_k=2,
               name="dil_o")
    x = ffn(seq, norm_mod(seq, x, p["g_ffn"], 2, mod, 3), p["w_ff1"], p["w_ff2"], 2, x, mod)

    h = norm_mod(seq, x, p["g_mix"], 3, mod, 0)
    gx = matmul(seq, h, p["lru_w_in"], 0, tn=1024, out_dtype=F32, bias=p["lru_b_in"], name="lru_in")
    wd = gx.shape[1] // 2
    gx3 = gx.reshape(n, l, 2 * wd)
    if prompt:
        buf8 = jnp.zeros((n, 8, wd), F32)
        h0 = None
        new["lru_conv"] = gx3[:, l - 3:, wd:]
    else:
        buf8 = _pad_buf8(cache["lru_conv"][0])
        h0 = cache["lru"][0]
        new["lru_conv"] = jnp.concatenate([cache["lru_conv"][0], gx3[:, :, wd:]], axis=1)[:, -3:]
    y, new["lru"] = lru_core(n, l, gx, buf8, h0, p, 256 if prompt else l, act)
    x = matmul(seq, y, p["lru_w_out"], 0, tn=512, out_dtype=F32, epi="resid", res=x, mod=mod, mod_layer=3, mod_k=2,
               name="lru_out")
    x = ffn(seq, norm_mod(seq, x, p["g_ffn"], 3, mod, 3), p["w_ff1"], p["w_ff2"], 3, x, mod)

    y = final_norm(x, p["g_final"], min(seq.tm, 512))
    return y.reshape(n, l, d), {k: v[None] for k, v in new.items()}


def kernel(x_prompt, x_sample, cache_swa_kv, state_ssd_conv, state_ssd, cache_dil_kv_w128, cache_dil_kv_w512,
           cache_dil_kv_w2048, state_lru_conv, state_lru, c_prompt, c_sample, w_ada, b_ada, g_mix, g_ffn,
           w_ff1, w_ff2, g_final, swa_w_qkv, swa_sinks, swa_w_o, ssd_w_in, ssd_conv_w, ssd_conv_b,
           ssd_dt_bias, ssd_a_log, ssd_d, ssd_norm, ssd_w_out, dil_w_qkv, dil_w_o, lru_w_in, lru_b_in,
           lru_conv_w, lru_conv_b, lru_w_r, lru_b_r, lru_w_i, lru_b_i, lru_lam, lru_w_out):
    p = dict(g_mix=g_mix, g_ffn=g_ffn, w_ff1=w_ff1.astype(BF16), w_ff2=w_ff2.astype(BF16), g_final=g_final,
             swa_w_qkv=swa_w_qkv, swa_sinks=swa_sinks, swa_w_o=swa_w_o,
             ssd_w_in=ssd_w_in, ssd_conv_w=ssd_conv_w, ssd_conv_b=ssd_conv_b, ssd_dt_bias=ssd_dt_bias,
             ssd_a_log=ssd_a_log, ssd_d=ssd_d, ssd_norm=ssd_norm, ssd_w_out=ssd_w_out,
             dil_w_qkv=dil_w_qkv, dil_w_o=dil_w_o,
             lru_w_in=lru_w_in, lru_b_in=lru_b_in, lru_conv_w=lru_conv_w, lru_conv_b=lru_conv_b,
             lru_w_r=lru_w_r, lru_b_r=lru_b_r, lru_w_i=lru_w_i, lru_b_i=lru_b_i, lru_lam=lru_lam,
             lru_w_out=lru_w_out)
    cache = dict(swa_kv=cache_swa_kv, ssd_conv=state_ssd_conv, ssd=state_ssd, dil_kv_w128=cache_dil_kv_w128,
                 dil_kv_w512=cache_dil_kv_w512, dil_kv_w2048=cache_dil_kv_w2048, lru_conv=state_lru_conv,
                 lru=state_lru)
    nb, l, d = x_prompt.shape
    ns, ls, _ = x_sample.shape
    depth = w_ada.shape[0]
    rows = -(-(nb + ns) // 16) * 16
    c_all = jnp.concatenate([c_prompt, c_sample, jnp.zeros((rows - nb - ns, d), F32)], axis=0)
    mod = ada_mod(c_all, w_ada, b_ada)
    mod_p = mod.reshape(depth, rows, 1, 6 * d)
    mod_s = jnp.repeat(mod[:, nb:nb + ns], ls, axis=1).reshape(depth, 1, ns * ls, 6 * d)

    seq_p = Seq(nb, l, 1024, True)
    seq_s = Seq(ns, ls, ns * ls, False)
    y_p, sp = _trunk(seq_p, x_prompt.reshape(nb * l, d), mod_p, jnp.arange(l, dtype=jnp.int32), None, p)
    y_s, ss = _trunk(seq_s, x_sample.reshape(ns * ls, d), mod_s, PAST_LEN + jnp.arange(ls, dtype=jnp.int32), cache, p)
    return (y_p, y_s,
            sp["swa_kv"], ss["swa_kv"],
            sp["ssd_conv"], ss["ssd_conv"],
            sp["ssd"], ss["ssd"],
            sp["dil_kv_w128"], ss["dil_kv_w128"],
            sp["dil_kv_w512"], ss["dil_kv_w512"],
            sp["dil_kv_w2048"], ss["dil_kv_w2048"],
            sp["lru_conv"], ss["lru_conv"],
            sp["lru"], ss["lru"])
```

```python
import functools
import math

import jax
import jax.numpy as jnp
from jax import lax
from jax.experimental import pallas as pl
from jax.experimental.pallas import tpu as pltpu

F32 = jnp.float32
BF16 = jnp.bfloat16
HI = lax.Precision.HIGHEST

RMS_EPS = 1e-6
ROPE_THETA = 10000.0
NEG_INF = -1e30
LANES = 128
VMEM_LIMIT = 56 * 1024 * 1024

PAST_LEN = 16384
ATTN_BLOCK = 128
SWA_WINDOW, SWA_HD, SWA_QH, SWA_KVH = 128, 64, 32, 4
DIL_PATTERN = ((128, 1), (512, 4), (2048, 16))
DIL_HD, DIL_QH, DIL_KVH = 128, 16, 4
SSM_HEADS, SSM_P, SSM_S, SSM_G, SSM_HPG, SSM_CHUNK = 64, 64, 128, 8, 8, 128
SSM_DI = SSM_HEADS * SSM_P
SSM_GW = SSM_HPG * SSM_P
LRU_BLOCKS, LRU_C = 8, 8.0


def _cparams(*sem):
    return pltpu.CompilerParams(dimension_semantics=sem, vmem_limit_bytes=VMEM_LIMIT)


def _nt_dot(a, b):
    return lax.dot_general(a, b, (((1,), (1,)), ((), ())), preferred_element_type=F32)


def _tn_dot(a, b):
    return lax.dot_general(a, b, (((0,), (0,)), ((), ())), preferred_element_type=F32)


class Seq:
    def __init__(self, n, l, tm, prompt):
        self.n, self.l, self.m, self.tm, self.prompt = n, l, n * l, tm, prompt

    def mod_spec(self, layer, k, d, row_axis):
        tm, l = self.tm, self.l
        if self.prompt:
            return pl.BlockSpec((None, None, 1, d), lambda *g: (layer, (g[row_axis] * tm) // l, 0, k))
        return pl.BlockSpec((None, None, tm, d), lambda *g: (layer, 0, g[row_axis], k))


def _ada_kernel(c_ref, w_ref, b_ref, o_ref):
    c = c_ref[...]
    cond = (c * jax.nn.sigmoid(c)).astype(BF16)
    o_ref[...] = jnp.dot(cond, w_ref[...].astype(BF16), preferred_element_type=F32) + b_ref[...]


def ada_mod(c_all, w_ada, b_ada, tn=1024):
    depth, d, n6 = w_ada.shape
    r = c_all.shape[0]
    return pl.pallas_call(
        _ada_kernel,
        grid=(depth, n6 // tn),
        in_specs=[pl.BlockSpec((r, d), lambda a, j: (0, 0)),
                  pl.BlockSpec((None, d, tn), lambda a, j: (a, 0, j)),
                  pl.BlockSpec((None, 1, tn), lambda a, j: (a, 0, j))],
        out_specs=pl.BlockSpec((None, r, tn), lambda a, j: (a, 0, j)),
        out_shape=jax.ShapeDtypeStruct((depth, r, n6), F32),
        compiler_params=_cparams("arbitrary", "arbitrary"),
        name="ada_mod",
    )(c_all, w_ada, b_ada.reshape(depth, 1, n6))


def _norm_mod_kernel(x_ref, g_ref, sh_ref, sc_ref, o_ref):
    x = x_ref[...]
    y = x * lax.rsqrt(jnp.mean(x * x, axis=-1, keepdims=True) + RMS_EPS) * g_ref[...]
    o_ref[...] = (y * (1.0 + sc_ref[...]) + sh_ref[...]).astype(o_ref.dtype)


def _norm_kernel(x_ref, g_ref, o_ref):
    x = x_ref[...]
    o_ref[...] = x * lax.rsqrt(jnp.mean(x * x, axis=-1, keepdims=True) + RMS_EPS) * g_ref[...]


def norm_mod(seq, x, g, layer, mod, k_shift):
    m, d = x.shape
    tm = min(seq.tm, 512)
    sub = Seq(seq.n, seq.l, tm, seq.prompt)
    return pl.pallas_call(
        _norm_mod_kernel,
        grid=(m // tm,),
        in_specs=[pl.BlockSpec((tm, d), lambda i: (i, 0)),
                  pl.BlockSpec((None, 1, d), lambda i: (layer, 0, 0)),
                  sub.mod_spec(layer, k_shift, d, 0),
                  sub.mod_spec(layer, k_shift + 1, d, 0)],
        out_specs=pl.BlockSpec((tm, d), lambda i: (i, 0)),
        out_shape=jax.ShapeDtypeStruct((m, d), BF16),
        compiler_params=_cparams("arbitrary"),
        name="norm_mod",
    )(x, g.reshape(g.shape[0], 1, d), mod, mod)


def final_norm(x, g, tm):
    m, d = x.shape
    return pl.pallas_call(
        _norm_kernel,
        grid=(m // tm,),
        in_specs=[pl.BlockSpec((tm, d), lambda i: (i, 0)), pl.BlockSpec((1, d), lambda i: (0, 0))],
        out_specs=pl.BlockSpec((tm, d), lambda i: (i, 0)),
        out_shape=jax.ShapeDtypeStruct((m, d), F32),
        compiler_params=_cparams("arbitrary"),
        name="final_norm",
    )(x, g.reshape(1, d))


def _rope_cols(y, cos, sin, hd):
    outs = []
    for c in range(y.shape[1] // LANES):
        yc = y[:, c * LANES:(c + 1) * LANES]
        if hd == LANES:
            partner = pltpu.roll(yc, LANES // 2, axis=1)
        else:
            lane = lax.broadcasted_iota(jnp.int32, yc.shape, 1)
            partner = jnp.where(lane % hd < hd // 2, pltpu.roll(yc, LANES - hd // 2, axis=1),
                                pltpu.roll(yc, hd // 2, axis=1))
        outs.append(yc * cos + partner * sin)
    return outs[0] if len(outs) == 1 else jnp.concatenate(outs, axis=1)


def _mm_kernel(*refs, epi, n_rope, hd, has_bias):
    x_ref, w_ref = refs[0], refs[1]
    o_ref, wb_ref = refs[-2], refs[-1]
    extra = refs[2:-2]

    @pl.when(pl.program_id(1) == 0)
    def _():
        wb_ref[...] = w_ref[...].astype(BF16)

    y = jnp.dot(x_ref[...].astype(BF16), wb_ref[...], preferred_element_type=F32)
    if has_bias:
        y = y + extra[0][...]
        extra = extra[1:]
    if epi == "none":
        o_ref[...] = y.astype(o_ref.dtype)
    elif epi == "softplus":
        o_ref[...] = jax.nn.softplus(y).astype(o_ref.dtype)
    elif epi == "resid":
        res_ref, gate_ref = extra
        o_ref[...] = res_ref[...] + gate_ref[...] * y
    elif epi == "rope":
        cos_ref, sin_ref = extra
        j = pl.program_id(0)

        @pl.when(j < n_rope)
        def _():
            o_ref[...] = _rope_cols(y, cos_ref[...], sin_ref[...], hd).astype(o_ref.dtype)

        @pl.when(j >= n_rope)
        def _():
            o_ref[...] = y.astype(o_ref.dtype)
    else:
        raise ValueError(epi)


def matmul(seq, x, w, layer, *, col0=0, ncols=None, tn, out_dtype, tm=None, bias=None, epi="none",
           rope=None, n_rope=0, hd=LANES, res=None, mod=None, mod_layer=0, mod_k=0, cstride=1, name="mm"):
    m, k = x.shape
    ntot = w.shape[2]
    ncols = ntot - col0 if ncols is None else ncols
    tm = seq.tm if tm is None else tm
    cb = col0 // tn
    assert col0 % tn == 0 and ncols % tn == 0 and m % tm == 0
    in_specs = [pl.BlockSpec((tm, k), lambda j, i: (i, 0)),
                pl.BlockSpec((None, k, tn), lambda j, i: (layer, 0, cb + j * cstride))]
    args = [x, w]
    if bias is not None:
        in_specs.append(pl.BlockSpec((None, 1, tn), lambda j, i: (0, 0, cb + j * cstride)))
        args.append(bias.reshape(1, 1, -1))
    if epi == "rope":
        cos, sin = rope
        nrt = cos.shape[0] // tm
        for t in (cos, sin):
            in_specs.append(pl.BlockSpec((tm, LANES), lambda j, i: (i % nrt, 0)))
            args.append(t)
    if epi == "resid":
        sub = Seq(seq.n, seq.l, tm, seq.prompt)
        base = sub.mod_spec(mod_layer, mod_k, tn, 1)
        nk = w.shape[2] // tn
        gate_spec = pl.BlockSpec(base.block_shape,
                                 lambda j, i, f=base.index_map: f(j, i)[:3] + (f(j, i)[3] * nk + j,))
        in_specs += [pl.BlockSpec((tm, tn), lambda j, i: (i, j)), gate_spec]
        args += [res, mod]
    return pl.pallas_call(
        functools.partial(_mm_kernel, epi=epi, n_rope=n_rope, hd=hd, has_bias=bias is not None),
        grid=(ncols // tn, m // tm),
        in_specs=in_specs,
        out_specs=pl.BlockSpec((tm, tn), lambda j, i: (i, j)),
        out_shape=jax.ShapeDtypeStruct((m, ncols), out_dtype),
        scratch_shapes=[pltpu.VMEM((k, tn), BF16)],
        compiler_params=_cparams("arbitrary", "arbitrary"),
        name=name,
    )(*args)


def _ffn_kernel(x_ref, w1_ref, w2_ref, res_ref, gate_ref, o_ref, h_ref, *, na, tf):
    s = pl.program_id(1)

    @pl.when(s < na)
    def _():
        h = jnp.dot(x_ref[...], w1_ref[...], preferred_element_type=F32)
        col = pl.multiple_of(s * tf, tf)
        h_ref[:, pl.ds(col, tf)] = jnp.square(jnp.maximum(h, 0.0)).astype(BF16)

    @pl.when(s >= na)
    def _():
        y = jnp.dot(h_ref[...], w2_ref[...], preferred_element_type=F32)
        o_ref[...] = res_ref[...] + gate_ref[...] * y


def ffn(seq, h, w1, w2, layer, res, mod, tf=1024, tn=256):
    m, d = h.shape
    f = w1.shape[2]
    tm = seq.tm
    na, nb = f // tf, d // tn
    sub = Seq(seq.n, seq.l, tm, seq.prompt)
    base = sub.mod_spec(layer, 5, tn, 0)
    ocol = lambda s: jnp.maximum(s - na, 0)
    gate_spec = pl.BlockSpec(base.block_shape,
                             lambda i, s, fm=base.index_map: fm(i, s)[:3] + (5 * nb + ocol(s),))
    return pl.pallas_call(
        functools.partial(_ffn_kernel, na=na, tf=tf),
        grid=(m // tm, na + nb),
        in_specs=[pl.BlockSpec((tm, d), lambda i, s: (i, 0)),
                  pl.BlockSpec((None, d, tf), lambda i, s: (layer, 0, jnp.minimum(s, na - 1))),
                  pl.BlockSpec((None, f, tn), lambda i, s: (layer, 0, ocol(s))),
                  pl.BlockSpec((tm, tn), lambda i, s: (i, ocol(s))),
                  gate_spec],
        out_specs=pl.BlockSpec((tm, tn), lambda i, s: (i, ocol(s))),
        out_shape=jax.ShapeDtypeStruct((m, d), F32),
        scratch_shapes=[pltpu.VMEM((tm, f), BF16)],
        compiler_params=_cparams("arbitrary", "arbitrary"),
        name="ffn",
    )(h, w1, w2, res, mod)


def rope_tables(pos, hd, reps):
    half = hd // 2
    inv = ROPE_THETA ** (-jnp.arange(half, dtype=F32) / half)
    ang = pos.astype(F32)[:, None] * inv[None, :]
    cos = jnp.concatenate([jnp.cos(ang), jnp.cos(ang)], axis=-1)
    sin = jnp.concatenate([-jnp.sin(ang), jnp.sin(ang)], axis=-1)
    lane_reps = LANES // hd
    return jnp.tile(cos, (reps, lane_reps)), jnp.tile(sin, (reps, lane_reps))


def _attn_prompt_kernel(*refs, hd, kvh, grp, scale, has_sinks, want_lse):
    if has_sinks:
        sink_ref, refs = refs[0], refs[1:]
    q_ref, kp_ref, kc_ref, vp_ref, vc_ref = refs[:5]
    o_ref = refs[5]
    lse_ref = refs[6] if want_lse else None
    ub = pl.program_id(2)
    bq = q_ref.shape[0]
    rows = grp * bq
    iq = lax.broadcasted_iota(jnp.int32, (rows, 2 * bq), 0) % bq
    jk = lax.broadcasted_iota(jnp.int32, (rows, 2 * bq), 1)
    mask = (jk >= iq) & (jk <= iq + bq) & ((jk >= bq) | (ub > 0))
    rcol = lax.broadcasted_iota(jnp.int32, (rows, 1), 0)
    lane = lax.broadcasted_iota(jnp.int32, (bq, LANES), 1)
    lse_tile = jnp.zeros((bq, LANES), F32)
    for kh in range(kvh):
        ksl = slice(kh * hd, (kh + 1) * hd)
        kk = jnp.concatenate([kp_ref[:, ksl], kc_ref[:, ksl]], axis=0).astype(BF16)
        vv = jnp.concatenate([vp_ref[:, ksl], vc_ref[:, ksl]], axis=0).astype(BF16)
        qs = jnp.concatenate([q_ref[:, (kh * grp + g) * hd:(kh * grp + g + 1) * hd] for g in range(grp)], axis=0)
        s = jnp.where(mask, _nt_dot(qs.astype(BF16), kk) * scale, NEG_INF)
        mx = jnp.max(s, axis=-1, keepdims=True)
        if has_sinks:
            sk = jnp.zeros((rows, 1), F32)
            for g in range(grp):
                sk = jnp.where(rcol // bq == g, sink_ref[kh * grp + g], sk)
            mx = jnp.maximum(mx, sk)
        p = jnp.exp(s - mx)
        den = jnp.sum(p, axis=-1, keepdims=True)
        if has_sinks:
            den = den + jnp.exp(sk - mx)
        o = jnp.dot(p.astype(BF16), vv, preferred_element_type=F32) / den
        lse = mx + jnp.log(den)
        for g in range(grp):
            h = kh * grp + g
            o_ref[:, h * hd:(h + 1) * hd] = o[g * bq:(g + 1) * bq, :].astype(o_ref.dtype)
            if want_lse:
                lse_tile = jnp.where(lane == h, lse[g * bq:(g + 1) * bq, :], lse_tile)
    if want_lse:
        lse_ref[...] = lse_tile


def attn_prompt(n, l, q, kv, *, d, hd, kvh, grp, q_blk, k_blk, v_blk, sinks=None, want_lse=True, out_dtype=F32):
    bq = ATTN_BLOCK
    cq, ck = grp * kvh * hd, kvh * hd
    nq, nk = q.shape[1] // cq, kv.shape[1] // ck
    lu = l // d
    qv = q.reshape(n, lu, d * q.shape[1])
    kvv = kv.reshape(n, lu, d * kv.shape[1])
    prev = lambda u: jnp.maximum(u - 1, 0)
    in_specs = [pl.BlockSpec((None, bq, cq), lambda b, r, u: (b, u, r * nq + q_blk)),
                pl.BlockSpec((None, bq, ck), lambda b, r, u: (b, prev(u), r * nk + k_blk)),
                pl.BlockSpec((None, bq, ck), lambda b, r, u: (b, u, r * nk + k_blk)),
                pl.BlockSpec((None, bq, ck), lambda b, r, u: (b, prev(u), r * nk + v_blk)),
                pl.BlockSpec((None, bq, ck), lambda b, r, u: (b, u, r * nk + v_blk))]
    args = [qv, kvv, kvv, kvv, kvv]
    if sinks is not None:
        in_specs.insert(0, pl.BlockSpec(memory_space=pltpu.SMEM))
        args.insert(0, sinks)
    out_shape = [jax.ShapeDtypeStruct((n, lu, d * cq), out_dtype)]
    out_specs = [pl.BlockSpec((None, bq, cq), lambda b, r, u: (b, u, r))]
    if want_lse:
        out_shape.append(jax.ShapeDtypeStruct((n, lu, d * LANES), F32))
        out_specs.append(pl.BlockSpec((None, bq, LANES), lambda b, r, u: (b, u, r)))
    outs = pl.pallas_call(
        functools.partial(_attn_prompt_kernel, hd=hd, kvh=kvh, grp=grp, scale=hd ** -0.5,
                          has_sinks=sinks is not None, want_lse=want_lse),
        grid=(n, d, lu // bq),
        in_specs=in_specs,
        out_specs=out_specs,
        out_shape=out_shape,
        compiler_params=_cparams("arbitrary", "arbitrary", "arbitrary"),
        name=f"attn_prompt_d{d}",
    )(*args)
    o = outs[0].reshape(n * l, cq)
    return (o, outs[1].reshape(n * l, LANES)) if want_lse else (o, None)


def _attn_dil_kernel(*refs, d, grp, scale, has_prev):
    q_refs = refs[:grp]
    kp_ref, kc_ref, vp_ref, vc_ref = refs[grp:grp + 4]
    o_refs = refs[grp + 4:2 * grp + 4]
    lse_ref = refs[2 * grp + 4]
    ub, kh = pl.program_id(1), pl.program_id(2)
    bq = ATTN_BLOCK
    nk = 2 * bq if has_prev else bq
    iq = lax.broadcasted_iota(jnp.int32, (grp * bq, nk), 0) % bq
    jk = lax.broadcasted_iota(jnp.int32, (grp * bq, nk), 1)
    if has_prev:
        mask = (jk >= iq) & (jk <= iq + bq) & ((jk >= bq) | (ub > 0))
    else:
        mask = jk <= iq
    lane = lax.broadcasted_iota(jnp.int32, (bq, LANES), 1)

    @pl.when(kh == 0)
    def _():
        lse_ref[...] = jnp.zeros(lse_ref.shape, F32)

    for r in range(d):
        rows = pl.ds(r, bq, stride=d)
        qs = jnp.concatenate([qr[rows, :] for qr in q_refs], axis=0).astype(BF16)
        if has_prev:
            kk = jnp.concatenate([kp_ref[rows, :], kc_ref[rows, :]], axis=0).astype(BF16)
            vv = jnp.concatenate([vp_ref[rows, :], vc_ref[rows, :]], axis=0).astype(BF16)
        else:
            kk = kc_ref[rows, :].astype(BF16)
            vv = vc_ref[rows, :].astype(BF16)
        s = jnp.where(mask, _nt_dot(qs, kk) * scale, NEG_INF)
        mx = jnp.max(s, axis=-1, keepdims=True)
        p = jnp.exp(s - mx)
        den = jnp.sum(p, axis=-1, keepdims=True)
        o = jnp.dot(p.astype(BF16), vv, preferred_element_type=F32) / den
        lse = mx + jnp.log(den)
        tile = lse_ref[r * bq:(r + 1) * bq, :]
        for g in range(grp):
            o_refs[g][rows, :] = o[g * bq:(g + 1) * bq, :]
            tile = jnp.where(lane == kh * grp + g, lse[g * bq:(g + 1) * bq, :], tile)
        lse_ref[r * bq:(r + 1) * bq, :] = tile


def attn_dil_prompt(n, l, q, kv, *, d, kvh, grp, q_head0, k_head0, v_head0):
    hd = LANES
    rt = d * ATTN_BLOCK
    nub = l // rt
    has_prev = nub > 1
    prev = lambda u: jnp.maximum(u - 1, 0)
    cur = lambda u: u
    blk = lambda col, rowf: pl.BlockSpec((rt, hd), lambda b, u, kh: (b * nub + rowf(u), col(kh)))
    q_specs = [blk(lambda kh, g=g: q_head0 + kh * grp + g, cur) for g in range(grp)]
    outs = pl.pallas_call(
        functools.partial(_attn_dil_kernel, d=d, grp=grp, scale=hd ** -0.5, has_prev=has_prev),
        grid=(n, nub, kvh),
        in_specs=q_specs + [blk(lambda kh: k_head0 + kh, prev), blk(lambda kh: k_head0 + kh, cur),
                            blk(lambda kh: v_head0 + kh, prev), blk(lambda kh: v_head0 + kh, cur)],
        out_specs=[blk(lambda kh: kh, cur)] * grp + [blk(lambda kh: 0, cur)],
        out_shape=[jax.ShapeDtypeStruct((n * l, kvh * hd), F32)] * grp + [jax.ShapeDtypeStruct((n * l, LANES), F32)],
        compiler_params=_cparams("arbitrary", "arbitrary", "arbitrary"),
        name=f"attn_dil_d{d}",
    )(*([q] * grp), kv, kv, kv, kv)
    lse = outs[grp].reshape(n * nub, d, ATTN_BLOCK, LANES).transpose(0, 2, 1, 3).reshape(n * l, LANES)
    return list(outs[:grp]), lse


def _attn_sample_kernel(*refs, hd, kvh, grp, scale, d, has_sinks, want_lse):
    if has_sinks:
        sink_ref, refs = refs[0], refs[1:]
    q_ref, kn_ref, vn_ref, c_ref = refs[:4]
    o_ref = refs[4]
    lse_ref = refs[5] if want_lse else None
    lq = q_ref.shape[0]
    w = c_ref.shape[0]
    rows = grp * lq
    row = lax.broadcasted_iota(jnp.int32, (rows, w + lq), 0)
    jk = lax.broadcasted_iota(jnp.int32, (rows, w + lq), 1)
    dist = row % lq + w - jk
    mask = (dist >= 0) & (dist <= w) & (dist % d == 0)
    rcol = lax.broadcasted_iota(jnp.int32, (rows, 1), 0)
    lane = lax.broadcasted_iota(jnp.int32, (lq, LANES), 1)
    lse_tile = jnp.zeros((lq, LANES), F32)
    for kh in range(kvh):
        ksl = slice(kh * hd, (kh + 1) * hd)
        kk = jnp.concatenate([c_ref[:, kh, :], kn_ref[:, ksl]], axis=0).astype(BF16)
        vv = jnp.concatenate([c_ref[:, kvh + kh, :], vn_ref[:, ksl]], axis=0).astype(BF16)
        qs = jnp.concatenate([q_ref[:, (kh * grp + g) * hd:(kh * grp + g + 1) * hd] for g in range(grp)], axis=0)
        s = jnp.where(mask, _nt_dot(qs.astype(BF16), kk) * scale, NEG_INF)
        mx = jnp.max(s, axis=-1, keepdims=True)
        if has_sinks:
            sk = jnp.zeros((rows, 1), F32)
            for g in range(grp):
                sk = jnp.where(rcol // lq == g, sink_ref[kh * grp + g], sk)
            mx = jnp.maximum(mx, sk)
        p = jnp.exp(s - mx)
        den = jnp.sum(p, axis=-1, keepdims=True)
        if has_sinks:
            den = den + jnp.exp(sk - mx)
        o = jnp.dot(p.astype(BF16), vv, preferred_element_type=F32) / den
        lse = mx + jnp.log(den)
        for g in range(grp):
            h = kh * grp + g
            o_ref[:, h * hd:(h + 1) * hd] = o[g * lq:(g + 1) * lq, :]
            if want_lse:
                lse_tile = jnp.where(lane == h, lse[g * lq:(g + 1) * lq, :], lse_tile)
    if want_lse:
        lse_ref[...] = lse_tile


def attn_sample(n, l, q, kv, cache, *, d, hd, kvh, grp, q_blk, k_blk, v_blk, sinks=None, want_lse=True):
    cq, ck = grp * kvh * hd, kvh * hd
    w = cache.shape[1]
    in_specs = [pl.BlockSpec((l, cq), lambda b: (b, q_blk)),
                pl.BlockSpec((l, ck), lambda b: (b, k_blk)),
                pl.BlockSpec((l, ck), lambda b: (b, v_blk)),
                pl.BlockSpec((None, w, 2 * kvh, hd), lambda b: (b, 0, 0, 0))]
    args = [q, kv, kv, cache]
    if sinks is not None:
        in_specs.insert(0, pl.BlockSpec(memory_space=pltpu.SMEM))
        args.insert(0, sinks)
    out_shape = [jax.ShapeDtypeStruct((n * l, cq), F32)]
    out_specs = [pl.BlockSpec((l, cq), lambda b: (b, 0))]
    if want_lse:
        out_shape.append(jax.ShapeDtypeStruct((n * l, LANES), F32))
        out_specs.append(pl.BlockSpec((l, LANES), lambda b: (b, 0)))
    outs = pl.pallas_call(
        functools.partial(_attn_sample_kernel, hd=hd, kvh=kvh, grp=grp, scale=hd ** -0.5, d=d,
                          has_sinks=sinks is not None, want_lse=want_lse),
        grid=(n,),
        in_specs=in_specs,
        out_specs=out_specs,
        out_shape=out_shape,
        compiler_params=_cparams("arbitrary"),
        name=f"attn_sample_d{d}",
    )(*args)
    return (outs[0], outs[1]) if want_lse else (outs[0], None)


def _dil_merge_kernel(*refs, counts, qh, hd):
    no = sum(counts)
    o_refs, l_refs, out_ref = refs[:no], refs[no:no + len(counts)], refs[-1]
    ls = [r[...] for r in l_refs]
    mx = functools.reduce(jnp.maximum, ls)
    es = [jnp.exp(v - mx) for v in ls]
    tot = functools.reduce(lambda a, b: a + b, es)
    wts = [e / tot for e in es]
    for h in range(qh):
        acc, first = None, 0
        for cnt, wt in zip(counts, wts):
            if cnt == 1:
                piece = o_refs[first][:, h * hd:(h + 1) * hd]
            else:
                piece = o_refs[first + h % cnt][:, (h // cnt) * hd:(h // cnt + 1) * hd]
            first += cnt
            term = wt[:, h:h + 1] * piece
            acc = term if acc is None else acc + term
        out_ref[:, h * hd:(h + 1) * hd] = acc.astype(out_ref.dtype)


def dil_merge(outs, lses, tm, out_dtype):
    groups = [o if isinstance(o, (list, tuple)) else [o] for o in outs]
    flat = [a for grp_arrays in groups for a in grp_arrays]
    m = flat[0].shape[0]
    c = DIL_QH * DIL_HD
    return pl.pallas_call(
        functools.partial(_dil_merge_kernel, counts=tuple(len(g) for g in groups), qh=DIL_QH, hd=DIL_HD),
        grid=(m // tm,),
        in_specs=[pl.BlockSpec((tm, a.shape[1]), lambda i: (i, 0)) for a in flat]
        + [pl.BlockSpec((tm, LANES), lambda i: (i, 0))] * len(lses),
        out_specs=pl.BlockSpec((tm, c), lambda i: (i, 0)),
        out_shape=jax.ShapeDtypeStruct((m, c), out_dtype),
        compiler_params=_cparams("arbitrary"),
        name="dil_merge",
    )(*flat, *lses)


def _conv_silu(xp_ref, tail_ref, x_ref, w_ref, b_ref, q):
    xp_ref[0:8, :] = tail_ref[...]
    xp_ref[8:8 + q, :] = x_ref[...]
    y = b_ref[...]
    for i in range(4):
        y = y + xp_ref[5 + i:5 + i + q, :] * w_ref[i:i + 1, :]
    tail_ref[...] = xp_ref[q:q + 8, :]
    return y * jax.nn.sigmoid(y)


def _ssd_kernel(*refs, q, ng, has_state):
    (z_ref, x_ref, b_ref, c_ref, bufx_ref, bufb_ref, bufc_ref, dtc_ref, dtr_ref, alr_ref, alc_ref, dpar_ref,
     cwx_ref, cwb_ref, cwc_ref, cbx_ref, cbb_ref, cbc_ref, nw_ref) = refs[:19]
    refs = refs[19:]
    if has_state:
        h0_ref, refs = refs[0], refs[1:]
    y_ref, st_ref, xpx, xpb, xpc, tlx, tlb, tlc, state, ysc = refs
    c = pl.program_id(2)
    gw, s = SSM_GW, SSM_S

    @pl.when(c == 0)
    def _():
        tlx[...] = bufx_ref[...]
        tlb[...] = bufb_ref[...]
        tlc[...] = bufc_ref[...]
        state[...] = h0_ref[...] if has_state else jnp.zeros(state.shape, F32)

    xs_all = _conv_silu(xpx, tlx, x_ref, cwx_ref, cbx_ref, q)
    bm_all = _conv_silu(xpb, tlb, b_ref, cwb_ref, cbb_ref, q)
    cm_all = _conv_silu(xpc, tlc, c_ref, cwc_ref, cbc_ref, q)

    li = lax.broadcasted_iota(jnp.int32, (q, q), 0)
    mi = lax.broadcasted_iota(jnp.int32, (q, q), 1)
    causal = li >= mi
    tri_c = causal.astype(F32)
    tri_r = (li <= mi).astype(F32)
    low = lax.broadcasted_iota(jnp.int32, (q, LANES), 1) < SSM_P

    def widen(v):
        return jnp.concatenate([jnp.where(low, v[:, 2 * j:2 * j + 1], v[:, 2 * j + 1:2 * j + 2])
                                for j in range(gw // LANES)], axis=1)

    for k in range(ng):
        xs = xs_all[:, k * gw:(k + 1) * gw]
        bmb = bm_all[:, k * s:(k + 1) * s].astype(BF16)
        cmb = cm_all[:, k * s:(k + 1) * s].astype(BF16)
        dt_c = dtc_ref[k]
        dt_r = dtr_ref[k]
        a_r = -jnp.exp(alr_ref[k])
        a_c = -jnp.exp(alc_ref[k])
        cs_c = jnp.dot(tri_c, dt_c * a_r, precision=HI, preferred_element_type=F32)
        cs_r = jnp.dot(dt_r * a_c, tri_r, precision=HI, preferred_element_type=F32)
        cs_last = cs_c[q - 1:q, :]

        xdt = xs * widen(dt_c)
        xe = (xdt * widen(jnp.exp(cs_last - cs_c))).astype(BF16)
        xdtb = xdt.astype(BF16)
        cb = _nt_dot(cmb, bmb)
        st = state[k]
        y_off = _nt_dot(cmb, st.astype(BF16)) * widen(jnp.exp(cs_c))
        for h in range(SSM_HPG):
            seg = cs_c[:, h:h + 1] - cs_r[h:h + 1, :]
            gm = (cb * jnp.exp(jnp.where(causal, seg, NEG_INF))).astype(BF16)
            ysc[:, k * gw + h * SSM_P:k * gw + (h + 1) * SSM_P] = jnp.dot(
                gm, xdtb[:, h * SSM_P:(h + 1) * SSM_P], preferred_element_type=F32)
        y = ysc[:, k * gw:(k + 1) * gw] + y_off + xs * dpar_ref[:, k * gw:(k + 1) * gw]
        new_st = _tn_dot(xe, bmb)
        dec_last = jnp.exp(cs_r[:, q - 1:q])
        st = jnp.concatenate([st[h * SSM_P:(h + 1) * SSM_P, :] * dec_last[h:h + 1, :] for h in range(SSM_HPG)],
                             axis=0) + new_st
        state[k] = st
        st_ref[k] = st

        z = z_ref[:, k * gw:(k + 1) * gw]
        y = y * (z * jax.nn.sigmoid(z))
        y = y * lax.rsqrt(jnp.mean(y * y, axis=-1, keepdims=True) + RMS_EPS) * nw_ref[:, k * gw:(k + 1) * gw]
        y_ref[:, k * gw:(k + 1) * gw] = y.astype(y_ref.dtype)


def ssd_core(n, l, zx, dt, conv_buf8, h0, p, out_dtype, ng):
    q = SSM_CHUNK if l % SSM_CHUNK == 0 else l
    nc = l // q
    g = SSM_G
    gw, s = ng * SSM_GW, ng * SSM_S
    xb0 = SSM_DI // gw
    bb0 = 2 * SSM_DI // s
    cb0 = bb0 + g // ng
    kb0 = SSM_DI // s
    kc0 = kb0 + g // ng
    dt4 = dt.reshape(n, l, g, SSM_HPG)
    dt_c = jnp.transpose(dt4, (0, 2, 1, 3))
    dt_r = jnp.transpose(dt4, (0, 2, 3, 1))
    alog = p["ssd_a_log"].reshape(g, SSM_HPG)
    row = lambda b, gi, c: (b * nc + c)
    in_specs = [
        pl.BlockSpec((q, gw), lambda b, gi, c: (row(b, gi, c), gi)),
        pl.BlockSpec((q, gw), lambda b, gi, c: (row(b, gi, c), xb0 + gi)),
        pl.BlockSpec((q, s), lambda b, gi, c: (row(b, gi, c), bb0 + gi)),
        pl.BlockSpec((q, s), lambda b, gi, c: (row(b, gi, c), cb0 + gi)),
        pl.BlockSpec((None, 8, gw), lambda b, gi, c: (b, 0, gi)),
        pl.BlockSpec((None, 8, s), lambda b, gi, c: (b, 0, kb0 + gi)),
        pl.BlockSpec((None, 8, s), lambda b, gi, c: (b, 0, kc0 + gi)),
        pl.BlockSpec((None, ng, q, SSM_HPG), lambda b, gi, c: (b, gi, c, 0)),
        pl.BlockSpec((None, ng, SSM_HPG, q), lambda b, gi, c: (b, gi, 0, c)),
        pl.BlockSpec((ng, 1, SSM_HPG), lambda b, gi, c: (gi, 0, 0)),
        pl.BlockSpec((ng, SSM_HPG, 1), lambda b, gi, c: (gi, 0, 0)),
        pl.BlockSpec((1, gw), lambda b, gi, c: (0, gi)),
        pl.BlockSpec((4, gw), lambda b, gi, c: (0, gi)),
        pl.BlockSpec((4, s), lambda b, gi, c: (0, kb0 + gi)),
        pl.BlockSpec((4, s), lambda b, gi, c: (0, kc0 + gi)),
        pl.BlockSpec((1, gw), lambda b, gi, c: (0, gi)),
        pl.BlockSpec((1, s), lambda b, gi, c: (0, kb0 + gi)),
        pl.BlockSpec((1, s), lambda b, gi, c: (0, kc0 + gi)),
        pl.BlockSpec((1, gw), lambda b, gi, c: (0, gi)),
    ]
    cw = p["ssd_conv_w"][0]
    cbias = p["ssd_conv_b"]
    args = [zx, zx, zx, zx, conv_buf8, conv_buf8, conv_buf8, dt_c, dt_r,
            alog.reshape(g, 1, SSM_HPG), alog.reshape(g, SSM_HPG, 1),
            jnp.repeat(p["ssd_d"].reshape(-1), SSM_P).reshape(1, SSM_DI),
            cw, cw, cw, cbias, cbias, cbias, p["ssd_norm"]]
    st_spec = pl.BlockSpec((None, ng, SSM_GW, SSM_S), lambda b, gi, c: (b, gi, 0, 0))
    if h0 is not None:
        in_specs.append(st_spec)
        args.append(h0.reshape(n, g, SSM_GW, SSM_S))
    y, st = pl.pallas_call(
        functools.partial(_ssd_kernel, q=q, ng=ng, has_state=h0 is not None),
        grid=(n, g // ng, nc),
        in_specs=in_specs,
        out_specs=[pl.BlockSpec((q, gw), lambda b, gi, c: (row(b, gi, c), gi)), st_spec],
        out_shape=[jax.ShapeDtypeStruct((n * l, SSM_DI), out_dtype),
                   jax.ShapeDtypeStruct((n, g, SSM_GW, SSM_S), F32)],
        scratch_shapes=[pltpu.VMEM((8 + q, gw), F32), pltpu.VMEM((8 + q, s), F32), pltpu.VMEM((8 + q, s), F32),
                        pltpu.VMEM((8, gw), F32), pltpu.VMEM((8, s), F32), pltpu.VMEM((8, s), F32),
                        pltpu.VMEM((ng, SSM_GW, SSM_S), F32), pltpu.VMEM((q, gw), F32)],
        compiler_params=_cparams("arbitrary", "arbitrary", "arbitrary"),
        name="ssd_core",
    )(*args)
    return y, st.reshape(n, SSM_HEADS, SSM_P, SSM_S)


def _lru_kernel(*refs, tl, has_state):
    (gate_ref, xb_ref, buf_ref, cw_ref, cb_ref, wr_ref, wi_ref, br_ref, bi_ref, lam_ref) = refs[:10]
    refs = refs[10:]
    if has_state:
        h0_ref, refs = refs[0], refs[1:]
    y_ref, last_ref, xp, tail, a_sc, u_sc, hs_sc, hcar, wrb, wib = refs
    t = pl.program_id(1)

    @pl.when((pl.program_id(0) == 0) & (t == 0))
    def _():
        wrb[...] = wr_ref[...].astype(BF16)
        wib[...] = wi_ref[...].astype(BF16)

    @pl.when(t == 0)
    def _():
        tail[...] = buf_ref[...]
        hcar[...] = h0_ref[...] if has_state else jnp.zeros(hcar.shape, F32)

    xp[0:8, :] = tail[...]
    xp[8:8 + tl, :] = xb_ref[...]
    xc = cb_ref[...]
    for i in range(4):
        xc = xc + xp[5 + i:5 + i + tl, :] * cw_ref[i:i + 1, :]
    tail[...] = xp[tl:tl + 8, :]

    xcb = xc.astype(BF16)
    bd = xc.shape[1] // LRU_BLOCKS
    rs, is_ = [], []
    for b in range(LRU_BLOCKS):
        xblk = xcb[:, b * bd:(b + 1) * bd]
        rs.append(jnp.dot(xblk, wrb[b], preferred_element_type=F32))
        is_.append(jnp.dot(xblk, wib[b], preferred_element_type=F32))
    r = jax.nn.sigmoid(jnp.concatenate(rs, axis=1) + br_ref[...])
    ig = jax.nn.sigmoid(jnp.concatenate(is_, axis=1) + bi_ref[...])
    log_a = -LRU_C * r * jax.nn.softplus(-lam_ref[...])
    a_sc[...] = jnp.exp(log_a)
    u_sc[...] = jnp.sqrt(-jnp.tanh(log_a) * (jnp.exp(2.0 * log_a) + 1.0)) * (ig * xc)

    def step(i, h):
        h = a_sc[pl.ds(i, 1), :] * h + u_sc[pl.ds(i, 1), :]
        hs_sc[pl.ds(i, 1), :] = h
        return h

    h = lax.fori_loop(0, tl, step, hcar[...], unroll=8)
    hcar[...] = h
    last_ref[...] = h
    y_ref[...] = (hs_sc[...] * jax.nn.gelu(gate_ref[...])).astype(y_ref.dtype)


def lru_core(n, l, gx, conv_buf8, h0, p, tl, out_dtype):
    wd = gx.shape[1] // 2
    nt = l // tl
    bd = wd // LRU_BLOCKS
    vec = lambda a: a.reshape(1, wd)
    cst2 = lambda b, t: (0, 0)
    in_specs = [pl.BlockSpec((tl, wd), lambda b, t: (b * nt + t, 0)),
                pl.BlockSpec((tl, wd), lambda b, t: (b * nt + t, 1)),
                pl.BlockSpec((None, 8, wd), lambda b, t: (b, 0, 0)),
                pl.BlockSpec((4, wd), cst2), pl.BlockSpec((1, wd), cst2),
                pl.BlockSpec((LRU_BLOCKS, bd, bd), lambda b, t: (0, 0, 0)),
                pl.BlockSpec((LRU_BLOCKS, bd, bd), lambda b, t: (0, 0, 0)),
                pl.BlockSpec((1, wd), cst2), pl.BlockSpec((1, wd), cst2), pl.BlockSpec((1, wd), cst2)]
    args = [gx, gx, conv_buf8, p["lru_conv_w"][0], vec(p["lru_conv_b"]), p["lru_w_r"][0], p["lru_w_i"][0],
            vec(p["lru_b_r"]), vec(p["lru_b_i"]), vec(p["lru_lam"])]
    if h0 is not None:
        in_specs.append(pl.BlockSpec((None, 1, wd), lambda b, t: (b, 0, 0)))
        args.append(h0.reshape(n, 1, wd))
    y, last = pl.pallas_call(
        functools.partial(_lru_kernel, tl=tl, has_state=h0 is not None),
        grid=(n, nt),
        in_specs=in_specs,
        out_specs=[pl.BlockSpec((tl, wd), lambda b, t: (b * nt + t, 0)),
                   pl.BlockSpec((None, 1, wd), lambda b, t: (b, 0, 0))],
        out_shape=[jax.ShapeDtypeStruct((n * l, wd), out_dtype), jax.ShapeDtypeStruct((n, 1, wd), F32)],
        scratch_shapes=[pltpu.VMEM((8 + tl, wd), F32), pltpu.VMEM((8, wd), F32),
                        pltpu.VMEM((tl, wd), F32), pltpu.VMEM((tl, wd), F32), pltpu.VMEM((tl, wd), F32),
                        pltpu.VMEM((1, wd), F32),
                        pltpu.VMEM((LRU_BLOCKS, bd, bd), BF16), pltpu.VMEM((LRU_BLOCKS, bd, bd), BF16)],
        compiler_params=_cparams("arbitrary", "arbitrary"),
        name="lru_core",
    )(*args)
    return y, last.reshape(n, wd)


def _pad_buf8(buf):
    return jnp.pad(buf, ((0, 0), (5, 0), (0, 0)))


def _trunk(seq, x, mod, pos, cache, p):
    n, l, m = seq.n, seq.l, seq.m
    prompt = seq.prompt
    d = x.shape[1]
    act = BF16 if prompt else F32
    new = {}
    reps = 1 if prompt else n

    h = norm_mod(seq, x, p["g_mix"], 0, mod, 0)
    rope64 = rope_tables(pos, SWA_HD, reps)
    nq = SWA_QH * SWA_HD
    q = matmul(seq, h, p["swa_w_qkv"], 0, col0=0, ncols=nq, tn=1024, out_dtype=act, epi="rope", rope=rope64,
               n_rope=nq // 1024, hd=SWA_HD, name="swa_q")
    kv = matmul(seq, h, p["swa_w_qkv"], 0, col0=nq, tn=256, out_dtype=F32, epi="rope", rope=rope64,
                n_rope=1, hd=SWA_HD, name="swa_kv")
    akw = dict(d=1, hd=SWA_HD, kvh=SWA_KVH, grp=SWA_QH // SWA_KVH, q_blk=0, k_blk=0, v_blk=1,
               sinks=p["swa_sinks"][0], want_lse=False)
    if prompt:
        o, _ = attn_prompt(n, l, q, kv, out_dtype=BF16, **akw)
        new["swa_kv"] = kv.reshape(n, l, 2, SWA_KVH, SWA_HD)[:, l - min(SWA_WINDOW, l):]
    else:
        c = cache["swa_kv"][0]
        o, _ = attn_sample(n, l, q, kv, c.reshape(n, c.shape[1], 2 * SWA_KVH, SWA_HD), **akw)
        new["swa_kv"] = kv.reshape(n, l, 2, SWA_KVH, SWA_HD)
    x = matmul(seq, o, p["swa_w_o"], 0, tn=512, out_dtype=F32, epi="resid", res=x, mod=mod, mod_layer=0, mod_k=2,
               name="swa_o")
    x = ffn(seq, norm_mod(seq, x, p["g_ffn"], 0, mod, 3), p["w_ff1"], p["w_ff2"], 0, x, mod)

    h = norm_mod(seq, x, p["g_mix"], 1, mod, 0)
    nzx = SSM_DI + SSM_DI + 2 * SSM_G * SSM_S
    zx = matmul(seq, h, p["ssd_w_in"], 0, col0=0, ncols=nzx, tn=1024, out_dtype=F32, name="ssd_in")
    w_dt = p["ssd_w_in"][:, :, nzx:]
    dt = matmul(seq, h, w_dt, 0, tn=SSM_HEADS, out_dtype=F32, bias=p["ssd_dt_bias"], epi="softplus", name="ssd_dt")
    zx3 = zx.reshape(n, l, nzx)
    if prompt:
        buf8 = jnp.zeros((n, 8, nzx - SSM_DI), F32)
        h0 = None
        new["ssd_conv"] = zx3[:, l - 3:, SSM_DI:]
    else:
        buf8 = _pad_buf8(cache["ssd_conv"][0])
        h0 = cache["ssd"][0]
        new["ssd_conv"] = jnp.concatenate([cache["ssd_conv"][0], zx3[:, :, SSM_DI:]], axis=1)[:, -3:]
    y, new["ssd"] = ssd_core(n, l, zx, dt, buf8, h0, p, act, ng=4 if prompt else SSM_G)
    x = matmul(seq, y, p["ssd_w_out"], 0, tn=512, tm=min(seq.tm, 512), out_dtype=F32, epi="resid", res=x, mod=mod,
               mod_layer=1, mod_k=2, name="ssd_out")
    x = ffn(seq, norm_mod(seq, x, p["g_ffn"], 1, mod, 3), p["w_ff1"], p["w_ff2"], 1, x, mod)

    h = norm_mod(seq, x, p["g_mix"], 2, mod, 0)
    rope128 = rope_tables(pos, DIL_HD, reps)
    ng = len(DIL_PATTERN)
    nq = ng * DIL_QH * DIL_HD
    q = matmul(seq, h, p["dil_w_qkv"], 0, col0=0, ncols=nq, tn=1024, out_dtype=F32, epi="rope", rope=rope128,
               n_rope=nq // 1024, hd=DIL_HD, name="dil_q")
    ck = DIL_KVH * DIL_HD
    grp = DIL_QH // DIL_KVH
    outs, lses = [], []
    for g, ((w, dil), key) in enumerate(zip(DIL_PATTERN, ("dil_kv_w128", "dil_kv_w512", "dil_kv_w2048"))):
        kv = matmul(seq, h, p["dil_w_qkv"], 0, col0=nq + g * ck, ncols=2 * ck, cstride=ng, tn=ck, out_dtype=F32,
                    epi="rope", rope=rope128, n_rope=1, hd=DIL_HD, name="dil_kv")
        akw = dict(d=dil, hd=DIL_HD, kvh=DIL_KVH, grp=grp, q_blk=g, k_blk=0, v_blk=1)
        kv_g = kv.reshape(n, l, 2, DIL_KVH, DIL_HD)
        if not prompt:
            c = cache[key][0]
            o, lse = attn_sample(n, l, q, kv, c.reshape(n, c.shape[1], 2 * DIL_KVH, DIL_HD), **akw)
            new[key] = kv_g
        else:
            if dil == 1:
                o, lse = attn_prompt(n, l, q, kv, **akw)
            else:
                o, lse = attn_dil_prompt(n, l, q, kv, d=dil, kvh=DIL_KVH, grp=grp, q_head0=g * DIL_QH,
                                         k_head0=0, v_head0=DIL_KVH)
            new[key] = kv_g[:, l - min(w, l):]
        outs.append(o)
        lses.append(lse)
    o = dil_merge(outs, lses, seq.tm if not prompt else 512, act)
    x = matmul(seq, o, p["dil_w_o"], 0, tn=512, out_dtype=F32, epi="resid", res=x, mod=mod, mod_layer=2, mod_k=2,
               name="dil_o")
    x = ffn(seq, norm_mod(seq, x, p["g_ffn"], 2, mod, 3), p["w_ff1"], p["w_ff2"], 2, x, mod)

    h = norm_mod(seq, x, p["g_mix"], 3, mod, 0)
    gx = matmul(seq, h, p["lru_w_in"], 0, tn=1024, out_dtype=F32, bias=p["lru_b_in"], name="lru_in")
    wd = gx.shape[1] // 2
    gx3 = gx.reshape(n, l, 2 * wd)
    if prompt:
        buf8 = jnp.zeros((n, 8, wd), F32)
        h0 = None
        new["lru_conv"] = gx3[:, l - 3:, wd:]
    else:
        buf8 = _pad_buf8(cache["lru_conv"][0])
        h0 = cache["lru"][0]
        new["lru_conv"] = jnp.concatenate([cache["lru_conv"][0], gx3[:, :, wd:]], axis=1)[:, -3:]
    y, new["lru"] = lru_core(n, l, gx, buf8, h0, p, 256 if prompt else l, act)
    x = matmul(seq, y, p["lru_w_out"], 0, tn=512, out_dtype=F32, epi="resid", res=x, mod=mod, mod_layer=3, mod_k=2,
               name="lru_out")
    x = ffn(seq, norm_mod(seq, x, p["g_ffn"], 3, mod, 3), p["w_ff1"], p["w_ff2"], 3, x, mod)

    y = final_norm(x, p["g_final"], min(seq.tm, 512))
    return y.reshape(n, l, d), {k: v[None] for k, v in new.items()}


def kernel(x_prompt, x_sample, cache_swa_kv, state_ssd_conv, state_ssd, cache_dil_kv_w128, cache_dil_kv_w512,
           cache_dil_kv_w2048, state_lru_conv, state_lru, c_prompt, c_sample, w_ada, b_ada, g_mix, g_ffn,
           w_ff1, w_ff2, g_final, swa_w_qkv, swa_sinks, swa_w_o, ssd_w_in, ssd_conv_w, ssd_conv_b,
           ssd_dt_bias, ssd_a_log, ssd_d, ssd_norm, ssd_w_out, dil_w_qkv, dil_w_o, lru_w_in, lru_b_in,
           lru_conv_w, lru_conv_b, lru_w_r, lru_b_r, lru_w_i, lru_b_i, lru_lam, lru_w_out):
    p = dict(g_mix=g_mix, g_ffn=g_ffn, w_ff1=w_ff1.astype(BF16), w_ff2=w_ff2.astype(BF16), g_final=g_final,
             swa_w_qkv=swa_w_qkv, swa_sinks=swa_sinks, swa_w_o=swa_w_o,
             ssd_w_in=ssd_w_in, ssd_conv_w=ssd_conv_w, ssd_conv_b=ssd_conv_b, ssd_dt_bias=ssd_dt_bias,
             ssd_a_log=ssd_a_log, ssd_d=ssd_d, ssd_norm=ssd_norm, ssd_w_out=ssd_w_out,
             dil_w_qkv=dil_w_qkv, dil_w_o=dil_w_o,
             lru_w_in=lru_w_in, lru_b_in=lru_b_in, lru_conv_w=lru_conv_w, lru_conv_b=lru_conv_b,
             lru_w_r=lru_w_r, lru_b_r=lru_b_r, lru_w_i=lru_w_i, lru_b_i=lru_b_i, lru_lam=lru_lam,
             lru_w_out=lru_w_out)
    cache = dict(swa_kv=cache_swa_kv, ssd_conv=state_ssd_conv, ssd=state_ssd, dil_kv_w128=cache_dil_kv_w128,
                 dil_kv_w512=cache_dil_kv_w512, dil_kv_w2048=cache_dil_kv_w2048, lru_conv=state_lru_conv,
                 lru=state_lru)
    nb, l, d = x_prompt.shape
    ns, ls, _ = x_sample.shape
    depth = w_ada.shape[0]
    rows = -(-(nb + ns) // 16) * 16
    c_all = jnp.concatenate([c_prompt, c_sample, jnp.zeros((rows - nb - ns, d), F32)], axis=0)
    mod = ada_mod(c_all, w_ada, b_ada)
    mod_p = mod.reshape(depth, rows, 1, 6 * d)
    mod_s = jnp.repeat(mod[:, nb:nb + ns], ls, axis=1).reshape(depth, 1, ns * ls, 6 * d)

    seq_p = Seq(nb, l, 1024, True)
    seq_s = Seq(ns, ls, ns * ls, False)
    y_p, sp = _trunk(seq_p, x_prompt.reshape(nb * l, d), mod_p, jnp.arange(l, dtype=jnp.int32), None, p)
    y_s, ss = _trunk(seq_s, x_sample.reshape(ns * ls, d), mod_s, PAST_LEN + jnp.arange(ls, dtype=jnp.int32), cache, p)
    return (y_p, y_s,
            sp["swa_kv"], ss["swa_kv"],
            sp["ssd_conv"], ss["ssd_conv"],
            sp["ssd"], ss["ssd"],
            sp["dil_kv_w128"], ss["dil_kv_w128"],
            sp["dil_kv_w512"], ss["dil_kv_w512"],
            sp["dil_kv_w2048"], ss["dil_kv_w2048"],
            sp["lru_conv"], ss["lru_conv"],
            sp["lru"], ss["lru"])
```

```python
import functools
import math

import jax
import jax.numpy as jnp
from jax import lax
from jax.experimental import pallas as pl
from jax.experimental.pallas import tpu as pltpu

F32 = jnp.float32
BF16 = jnp.bfloat16
HI = lax.Precision.HIGHEST

RMS_EPS = 1e-6
ROPE_THETA = 10000.0
NEG_INF = -1e30
LANES = 128
MXU_COLS = 256
VMEM_LIMIT = 56 * 1024 * 1024

PAST_LEN = 16384
ATTN_BLOCK = 128
SWA_WINDOW, SWA_HD, SWA_QH, SWA_KVH = 128, 64, 32, 4
DIL_PATTERN = ((128, 1), (512, 4), (2048, 16))
DIL_HD, DIL_QH, DIL_KVH = 128, 16, 4
SSM_HEADS, SSM_P, SSM_S, SSM_G, SSM_HPG, SSM_CHUNK = 64, 64, 128, 8, 8, 128
SSM_DI = SSM_HEADS * SSM_P
SSM_GW = SSM_HPG * SSM_P
LRU_BLOCKS, LRU_C = 8, 8.0


def _cparams(*sem):
    return pltpu.CompilerParams(dimension_semantics=sem, vmem_limit_bytes=VMEM_LIMIT)


def _nt_dot(a, b):
    return lax.dot_general(a, b, (((1,), (1,)), ((), ())), preferred_element_type=F32)


def _sigmoid(x):
    return 0.5 * jnp.tanh(0.5 * x) + 0.5


def _tn_dot(a, b):
    return lax.dot_general(a, b, (((0,), (0,)), ((), ())), preferred_element_type=F32)


class Seq:
    def __init__(self, n, l, tm, prompt):
        self.n, self.l, self.m, self.tm, self.prompt = n, l, n * l, tm, prompt

    def mod_spec(self, layer, k, d, row_axis):
        tm, l = self.tm, self.l
        if self.prompt:
            return pl.BlockSpec((None, None, 1, d), lambda *g: (layer, (g[row_axis] * tm) // l, 0, k))
        return pl.BlockSpec((None, None, tm, d), lambda *g: (layer, 0, g[row_axis], k))


def _ada_kernel(c_ref, w_ref, b_ref, o_ref):
    c = c_ref[...]
    cond = (c * jax.nn.sigmoid(c)).astype(BF16)
    o_ref[...] = jnp.dot(cond, w_ref[...].astype(BF16), preferred_element_type=F32) + b_ref[...]


def ada_mod(c_all, w_ada, b_ada, tn=1024):
    depth, d, n6 = w_ada.shape
    r = c_all.shape[0]
    return pl.pallas_call(
        _ada_kernel,
        grid=(depth, n6 // tn),
        in_specs=[pl.BlockSpec((r, d), lambda a, j: (0, 0)),
                  pl.BlockSpec((None, d, tn), lambda a, j: (a, 0, j)),
                  pl.BlockSpec((None, 1, tn), lambda a, j: (a, 0, j))],
        out_specs=pl.BlockSpec((None, r, tn), lambda a, j: (a, 0, j)),
        out_shape=jax.ShapeDtypeStruct((depth, r, n6), F32),
        compiler_params=_cparams("arbitrary", "arbitrary"),
        name="ada_mod",
    )(c_all, w_ada, b_ada.reshape(depth, 1, n6))


def _norm_mod_kernel(x_ref, g_ref, sh_ref, sc_ref, o_ref):
    x = x_ref[...]
    y = x * lax.rsqrt(jnp.mean(x * x, axis=-1, keepdims=True) + RMS_EPS) * g_ref[...]
    o_ref[...] = (y * (1.0 + sc_ref[...]) + sh_ref[...]).astype(o_ref.dtype)


def _norm_kernel(x_ref, g_ref, o_ref):
    x = x_ref[...]
    o_ref[...] = x * lax.rsqrt(jnp.mean(x * x, axis=-1, keepdims=True) + RMS_EPS) * g_ref[...]


def norm_mod(seq, x, g, layer, mod, k_shift):
    m, d = x.shape
    tm = min(seq.tm, 512)
    sub = Seq(seq.n, seq.l, tm, seq.prompt)
    return pl.pallas_call(
        _norm_mod_kernel,
        grid=(m // tm,),
        in_specs=[pl.BlockSpec((tm, d), lambda i: (i, 0)),
                  pl.BlockSpec((None, 1, d), lambda i: (layer, 0, 0)),
                  sub.mod_spec(layer, k_shift, d, 0),
                  sub.mod_spec(layer, k_shift + 1, d, 0)],
        out_specs=pl.BlockSpec((tm, d), lambda i: (i, 0)),
        out_shape=jax.ShapeDtypeStruct((m, d), BF16),
        compiler_params=_cparams("arbitrary"),
        name="norm_mod",
    )(x, g.reshape(g.shape[0], 1, d), mod, mod)


def final_norm(x, g, tm):
    m, d = x.shape
    return pl.pallas_call(
        _norm_kernel,
        grid=(m // tm,),
        in_specs=[pl.BlockSpec((tm, d), lambda i: (i, 0)), pl.BlockSpec((1, d), lambda i: (0, 0))],
        out_specs=pl.BlockSpec((tm, d), lambda i: (i, 0)),
        out_shape=jax.ShapeDtypeStruct((m, d), F32),
        compiler_params=_cparams("arbitrary"),
        name="final_norm",
    )(x, g.reshape(1, d))


def _rope_cols(y, cos, sin, hd):
    outs = []
    for c in range(y.shape[1] // LANES):
        yc = y[:, c * LANES:(c + 1) * LANES]
        if hd == LANES:
            partner = pltpu.roll(yc, LANES // 2, axis=1)
        else:
            lane = lax.broadcasted_iota(jnp.int32, yc.shape, 1)
            partner = jnp.where(lane % hd < hd // 2, pltpu.roll(yc, LANES - hd // 2, axis=1),
                                pltpu.roll(yc, hd // 2, axis=1))
        outs.append(yc * cos + partner * sin)
    return outs[0] if len(outs) == 1 else jnp.concatenate(outs, axis=1)


def _mm_kernel(*refs, epi, n_rope, all_rope, hd, has_bias):
    x_ref, w_ref = refs[0], refs[1]
    o_ref, wb_ref = refs[-2], refs[-1]
    extra = refs[2:-2]

    @pl.when(pl.program_id(1) == 0)
    def _():
        wb_ref[...] = w_ref[...].astype(BF16)

    if has_bias:
        bias_ref, extra = extra[0], extra[1:]
    xb = x_ref[...].astype(BF16)
    tn = o_ref.shape[1]
    sub = min(tn, MXU_COLS)
    for c in range(tn // sub):
        cs = slice(c * sub, (c + 1) * sub)
        y = jnp.dot(xb, wb_ref[:, cs], preferred_element_type=F32)
        if has_bias:
            y = y + bias_ref[:, cs]
        if epi == "none":
            out = y
        elif epi == "softplus":
            out = jax.nn.softplus(y)
        elif epi == "resid":
            res_ref, gate_ref = extra
            out = res_ref[:, cs] + gate_ref[:, cs] * y
        elif epi == "rope":
            cos_ref, sin_ref = extra
            out = _rope_cols(y, cos_ref[...], sin_ref[...], hd)
            if not all_rope:
                out = jnp.where(pl.program_id(0) < n_rope, out, y)
        else:
            raise ValueError(epi)
        o_ref[:, cs] = out.astype(o_ref.dtype)


def matmul(seq, x, w, layer, *, col0=0, ncols=None, tn, out_dtype, tm=None, bias=None, epi="none",
           rope=None, n_rope=0, hd=LANES, res=None, mod=None, mod_layer=0, mod_k=0, cstride=1, name="mm"):
    m, k = x.shape
    ntot = w.shape[2]
    ncols = ntot - col0 if ncols is None else ncols
    tm = seq.tm if tm is None else tm
    cb = col0 // tn
    assert col0 % tn == 0 and ncols % tn == 0 and m % tm == 0
    in_specs = [pl.BlockSpec((tm, k), lambda j, i: (i, 0)),
                pl.BlockSpec((None, k, tn), lambda j, i: (layer, 0, cb + j * cstride))]
    args = [x, w]
    if bias is not None:
        in_specs.append(pl.BlockSpec((None, 1, tn), lambda j, i: (0, 0, cb + j * cstride)))
        args.append(bias.reshape(1, 1, -1))
    if epi == "rope":
        cos, sin = rope
        nrt = cos.shape[0] // tm
        for t in (cos, sin):
            in_specs.append(pl.BlockSpec((tm, LANES), lambda j, i: (i % nrt, 0)))
            args.append(t)
    if epi == "resid":
        sub = Seq(seq.n, seq.l, tm, seq.prompt)
        base = sub.mod_spec(mod_layer, mod_k, tn, 1)
        nk = w.shape[2] // tn
        gate_spec = pl.BlockSpec(base.block_shape,
                                 lambda j, i, f=base.index_map: f(j, i)[:3] + (f(j, i)[3] * nk + j,))
        in_specs += [pl.BlockSpec((tm, tn), lambda j, i: (i, j)), gate_spec]
        args += [res, mod]
    return pl.pallas_call(
        functools.partial(_mm_kernel, epi=epi, n_rope=n_rope, all_rope=n_rope >= ncols // tn, hd=hd,
                          has_bias=bias is not None),
        grid=(ncols // tn, m // tm),
        in_specs=in_specs,
        out_specs=pl.BlockSpec((tm, tn), lambda j, i: (i, j)),
        out_shape=jax.ShapeDtypeStruct((m, ncols), out_dtype),
        scratch_shapes=[pltpu.VMEM((k, tn), BF16)],
        compiler_params=_cparams("arbitrary", "arbitrary"),
        name=name,
    )(*args)


def _ffn_kernel(x_ref, w1_ref, w2_ref, res_ref, gate_ref, o_ref, h_ref, *, na, tf):
    s = pl.program_id(1)

    @pl.when(s < na)
    def _():
        h = jnp.dot(x_ref[...], w1_ref[...], preferred_element_type=F32)
        col = pl.multiple_of(s * tf, tf)
        h_ref[:, pl.ds(col, tf)] = jnp.square(jnp.maximum(h, 0.0)).astype(BF16)

    @pl.when(s >= na)
    def _():
        y = jnp.dot(h_ref[...], w2_ref[...], preferred_element_type=F32)
        o_ref[...] = res_ref[...] + gate_ref[...] * y


def ffn(seq, h, w1, w2, layer, res, mod, tf=1024, tn=256):
    m, d = h.shape
    f = w1.shape[2]
    tm = seq.tm
    na, nb = f // tf, d // tn
    sub = Seq(seq.n, seq.l, tm, seq.prompt)
    base = sub.mod_spec(layer, 5, tn, 0)
    ocol = lambda s: jnp.maximum(s - na, 0)
    gate_spec = pl.BlockSpec(base.block_shape,
                             lambda i, s, fm=base.index_map: fm(i, s)[:3] + (5 * nb + ocol(s),))
    return pl.pallas_call(
        functools.partial(_ffn_kernel, na=na, tf=tf),
        grid=(m // tm, na + nb),
        in_specs=[pl.BlockSpec((tm, d), lambda i, s: (i, 0)),
                  pl.BlockSpec((None, d, tf), lambda i, s: (layer, 0, jnp.minimum(s, na - 1))),
                  pl.BlockSpec((None, f, tn), lambda i, s: (layer, 0, ocol(s))),
                  pl.BlockSpec((tm, tn), lambda i, s: (i, ocol(s))),
                  gate_spec],
        out_specs=pl.BlockSpec((tm, tn), lambda i, s: (i, ocol(s))),
        out_shape=jax.ShapeDtypeStruct((m, d), F32),
        scratch_shapes=[pltpu.VMEM((tm, f), BF16)],
        compiler_params=_cparams("arbitrary", "arbitrary"),
        name="ffn",
    )(h, w1, w2, res, mod)


def rope_tables(pos, hd, reps):
    half = hd // 2
    inv = ROPE_THETA ** (-jnp.arange(half, dtype=F32) / half)
    ang = pos.astype(F32)[:, None] * inv[None, :]
    cos = jnp.concatenate([jnp.cos(ang), jnp.cos(ang)], axis=-1)
    sin = jnp.concatenate([-jnp.sin(ang), jnp.sin(ang)], axis=-1)
    lane_reps = LANES // hd
    return jnp.tile(cos, (reps, lane_reps)), jnp.tile(sin, (reps, lane_reps))


def _attn_prompt_kernel(*refs, hd, kvh, grp, scale, has_sinks, want_lse):
    if has_sinks:
        sink_ref, refs = refs[0], refs[1:]
    q_ref, kp_ref, kc_ref, vp_ref, vc_ref = refs[:5]
    o_ref = refs[5]
    lse_ref = refs[6] if want_lse else None
    ub = pl.program_id(2)
    bq = q_ref.shape[0]
    rows = grp * bq
    iq = lax.broadcasted_iota(jnp.int32, (rows, 2 * bq), 0) % bq
    jk = lax.broadcasted_iota(jnp.int32, (rows, 2 * bq), 1)
    mask = (jk >= iq) & (jk <= iq + bq) & ((jk >= bq) | (ub > 0))
    rcol = lax.broadcasted_iota(jnp.int32, (rows, 1), 0)
    lane = lax.broadcasted_iota(jnp.int32, (bq, LANES), 1)
    lse_tile = jnp.zeros((bq, LANES), F32)
    for kh in range(kvh):
        ksl = slice(kh * hd, (kh + 1) * hd)
        kk = jnp.concatenate([kp_ref[:, ksl], kc_ref[:, ksl]], axis=0).astype(BF16)
        vv = jnp.concatenate([vp_ref[:, ksl], vc_ref[:, ksl]], axis=0).astype(BF16)
        qs = jnp.concatenate([q_ref[:, (kh * grp + g) * hd:(kh * grp + g + 1) * hd] for g in range(grp)], axis=0)
        s = jnp.where(mask, _nt_dot(qs.astype(BF16), kk) * scale, NEG_INF)
        mx = jnp.max(s, axis=-1, keepdims=True)
        if has_sinks:
            sk = jnp.zeros((rows, 1), F32)
            for g in range(grp):
                sk = jnp.where(rcol // bq == g, sink_ref[kh * grp + g], sk)
            mx = jnp.maximum(mx, sk)
        p = jnp.exp(s - mx)
        den = jnp.sum(p, axis=-1, keepdims=True)
        if has_sinks:
            den = den + jnp.exp(sk - mx)
        o = jnp.dot(p.astype(BF16), vv, preferred_element_type=F32) / den
        lse = mx + jnp.log(den)
        for g in range(grp):
            h = kh * grp + g
            o_ref[:, h * hd:(h + 1) * hd] = o[g * bq:(g + 1) * bq, :].astype(o_ref.dtype)
            if want_lse:
                lse_tile = jnp.where(lane == h, lse[g * bq:(g + 1) * bq, :], lse_tile)
    if want_lse:
        lse_ref[...] = lse_tile


def attn_prompt(n, l, q, kv, *, d, hd, kvh, grp, q_blk, k_blk, v_blk, sinks=None, want_lse=True, out_dtype=F32):
    bq = ATTN_BLOCK
    cq, ck = grp * kvh * hd, kvh * hd
    nq, nk = q.shape[1] // cq, kv.shape[1] // ck
    lu = l // d
    qv = q.reshape(n, lu, d * q.shape[1])
    kvv = kv.reshape(n, lu, d * kv.shape[1])
    prev = lambda u: jnp.maximum(u - 1, 0)
    in_specs = [pl.BlockSpec((None, bq, cq), lambda b, r, u: (b, u, r * nq + q_blk)),
                pl.BlockSpec((None, bq, ck), lambda b, r, u: (b, prev(u), r * nk + k_blk)),
                pl.BlockSpec((None, bq, ck), lambda b, r, u: (b, u, r * nk + k_blk)),
                pl.BlockSpec((None, bq, ck), lambda b, r, u: (b, prev(u), r * nk + v_blk)),
                pl.BlockSpec((None, bq, ck), lambda b, r, u: (b, u, r * nk + v_blk))]
    args = [qv, kvv, kvv, kvv, kvv]
    if sinks is not None:
        in_specs.insert(0, pl.BlockSpec(memory_space=pltpu.SMEM))
        args.insert(0, sinks)
    out_shape = [jax.ShapeDtypeStruct((n, lu, d * cq), out_dtype)]
    out_specs = [pl.BlockSpec((None, bq, cq), lambda b, r, u: (b, u, r))]
    if want_lse:
        out_shape.append(jax.ShapeDtypeStruct((n, lu, d * LANES), F32))
        out_specs.append(pl.BlockSpec((None, bq, LANES), lambda b, r, u: (b, u, r)))
    outs = pl.pallas_call(
        functools.partial(_attn_prompt_kernel, hd=hd, kvh=kvh, grp=grp, scale=hd ** -0.5,
                          has_sinks=sinks is not None, want_lse=want_lse),
        grid=(n, d, lu // bq),
        in_specs=in_specs,
        out_specs=out_specs,
        out_shape=out_shape,
        compiler_params=_cparams("arbitrary", "arbitrary", "arbitrary"),
        name=f"attn_prompt_d{d}",
    )(*args)
    o = outs[0].reshape(n * l, cq)
    return (o, outs[1].reshape(n * l, LANES)) if want_lse else (o, None)


def _attn_dil_kernel(*refs, d, grp, scale, has_prev):
    q_refs = refs[:grp]
    kp_ref, kc_ref, vp_ref, vc_ref = refs[grp:grp + 4]
    o_refs = refs[grp + 4:2 * grp + 4]
    lse_ref = refs[2 * grp + 4]
    ub, kh = pl.program_id(1), pl.program_id(2)
    bq = ATTN_BLOCK
    nk = 2 * bq if has_prev else bq
    iq = lax.broadcasted_iota(jnp.int32, (grp * bq, nk), 0) % bq
    jk = lax.broadcasted_iota(jnp.int32, (grp * bq, nk), 1)
    if has_prev:
        mask = (jk >= iq) & (jk <= iq + bq) & ((jk >= bq) | (ub > 0))
    else:
        mask = jk <= iq
    lane = lax.broadcasted_iota(jnp.int32, (bq, LANES), 1)

    @pl.when(kh == 0)
    def _():
        lse_ref[...] = jnp.zeros(lse_ref.shape, F32)

    for r in range(d):
        rows = pl.ds(r, bq, stride=d)
        qs = jnp.concatenate([qr[rows, :] for qr in q_refs], axis=0).astype(BF16)
        if has_prev:
            kk = jnp.concatenate([kp_ref[rows, :], kc_ref[rows, :]], axis=0).astype(BF16)
            vv = jnp.concatenate([vp_ref[rows, :], vc_ref[rows, :]], axis=0).astype(BF16)
        else:
            kk = kc_ref[rows, :].astype(BF16)
            vv = vc_ref[rows, :].astype(BF16)
        s = jnp.where(mask, _nt_dot(qs, kk) * scale, NEG_INF)
        mx = jnp.max(s, axis=-1, keepdims=True)
        p = jnp.exp(s - mx)
        den = jnp.sum(p, axis=-1, keepdims=True)
        o = jnp.dot(p.astype(BF16), vv, preferred_element_type=F32) / den
        lse = mx + jnp.log(den)
        tile = lse_ref[r * bq:(r + 1) * bq, :]
        for g in range(grp):
            o_refs[g][rows, :] = o[g * bq:(g + 1) * bq, :]
            tile = jnp.where(lane == kh * grp + g, lse[g * bq:(g + 1) * bq, :], tile)
        lse_ref[r * bq:(r + 1) * bq, :] = tile


def attn_dil_prompt(n, l, q, kv, *, d, kvh, grp, q_head0, k_head0, v_head0):
    hd = LANES
    rt = d * ATTN_BLOCK
    nub = l // rt
    has_prev = nub > 1
    prev = lambda u: jnp.maximum(u - 1, 0)
    cur = lambda u: u
    blk = lambda col, rowf: pl.BlockSpec((rt, hd), lambda b, u, kh: (b * nub + rowf(u), col(kh)))
    q_specs = [blk(lambda kh, g=g: q_head0 + kh * grp + g, cur) for g in range(grp)]
    outs = pl.pallas_call(
        functools.partial(_attn_dil_kernel, d=d, grp=grp, scale=hd ** -0.5, has_prev=has_prev),
        grid=(n, nub, kvh),
        in_specs=q_specs + [blk(lambda kh: k_head0 + kh, prev), blk(lambda kh: k_head0 + kh, cur),
                            blk(lambda kh: v_head0 + kh, prev), blk(lambda kh: v_head0 + kh, cur)],
        out_specs=[blk(lambda kh: kh, cur)] * grp + [blk(lambda kh: 0, cur)],
        out_shape=[jax.ShapeDtypeStruct((n * l, kvh * hd), F32)] * grp + [jax.ShapeDtypeStruct((n * l, LANES), F32)],
        compiler_params=_cparams("arbitrary", "arbitrary", "arbitrary"),
        name=f"attn_dil_d{d}",
    )(*([q] * grp), kv, kv, kv, kv)
    lse = outs[grp].reshape(n * nub, d, ATTN_BLOCK, LANES).transpose(0, 2, 1, 3).reshape(n * l, LANES)
    return list(outs[:grp]), lse


def _attn_sample_kernel(*refs, hd, kvh, grp, scale, d, has_sinks, want_lse):
    if has_sinks:
        sink_ref, refs = refs[0], refs[1:]
    q_ref, kn_ref, vn_ref, c_ref = refs[:4]
    o_ref = refs[4]
    lse_ref = refs[5] if want_lse else None
    lq = q_ref.shape[0]
    flat = len(c_ref.shape) == 2
    w = c_ref.shape[0] // (2 * kvh) if flat else c_ref.shape[0]

    def cache_head(idx):
        return c_ref[pl.ds(idx, w, stride=2 * kvh), :] if flat else c_ref[:, idx, :]

    rows = grp * lq
    row = lax.broadcasted_iota(jnp.int32, (rows, w + lq), 0)
    jk = lax.broadcasted_iota(jnp.int32, (rows, w + lq), 1)
    dist = row % lq + w - jk
    mask = (dist >= 0) & (dist <= w) & (dist % d == 0)
    rcol = lax.broadcasted_iota(jnp.int32, (rows, 1), 0)
    lane = lax.broadcasted_iota(jnp.int32, (lq, LANES), 1)
    lse_tile = jnp.zeros((lq, LANES), F32)
    for kh in range(kvh):
        ksl = slice(kh * hd, (kh + 1) * hd)
        kk = jnp.concatenate([cache_head(kh), kn_ref[:, ksl]], axis=0).astype(BF16)
        vv = jnp.concatenate([cache_head(kvh + kh), vn_ref[:, ksl]], axis=0).astype(BF16)
        qs = jnp.concatenate([q_ref[:, (kh * grp + g) * hd:(kh * grp + g + 1) * hd] for g in range(grp)], axis=0)
        s = jnp.where(mask, _nt_dot(qs.astype(BF16), kk) * scale, NEG_INF)
        mx = jnp.max(s, axis=-1, keepdims=True)
        if has_sinks:
            sk = jnp.zeros((rows, 1), F32)
            for g in range(grp):
                sk = jnp.where(rcol // lq == g, sink_ref[kh * grp + g], sk)
            mx = jnp.maximum(mx, sk)
        p = jnp.exp(s - mx)
        den = jnp.sum(p, axis=-1, keepdims=True)
        if has_sinks:
            den = den + jnp.exp(sk - mx)
        o = jnp.dot(p.astype(BF16), vv, preferred_element_type=F32) / den
        lse = mx + jnp.log(den)
        for g in range(grp):
            h = kh * grp + g
            o_ref[:, h * hd:(h + 1) * hd] = o[g * lq:(g + 1) * lq, :]
            if want_lse:
                lse_tile = jnp.where(lane == h, lse[g * lq:(g + 1) * lq, :], lse_tile)
    if want_lse:
        lse_ref[...] = lse_tile


def attn_sample(n, l, q, kv, cache, *, d, hd, kvh, grp, q_blk, k_blk, v_blk, sinks=None, want_lse=True):
    cq, ck = grp * kvh * hd, kvh * hd
    w = cache.shape[1]
    if hd == LANES:
        cache = cache.reshape(n, w * 2 * kvh, hd)
        cache_spec = pl.BlockSpec((None, w * 2 * kvh, hd), lambda b: (b, 0, 0))
    else:
        cache_spec = pl.BlockSpec((None, w, 2 * kvh, hd), lambda b: (b, 0, 0, 0))
    in_specs = [pl.BlockSpec((l, cq), lambda b: (b, q_blk)),
                pl.BlockSpec((l, ck), lambda b: (b, k_blk)),
                pl.BlockSpec((l, ck), lambda b: (b, v_blk)),
                cache_spec]
    args = [q, kv, kv, cache]
    if sinks is not None:
        in_specs.insert(0, pl.BlockSpec(memory_space=pltpu.SMEM))
        args.insert(0, sinks)
    out_shape = [jax.ShapeDtypeStruct((n * l, cq), F32)]
    out_specs = [pl.BlockSpec((l, cq), lambda b: (b, 0))]
    if want_lse:
        out_shape.append(jax.ShapeDtypeStruct((n * l, LANES), F32))
        out_specs.append(pl.BlockSpec((l, LANES), lambda b: (b, 0)))
    outs = pl.pallas_call(
        functools.partial(_attn_sample_kernel, hd=hd, kvh=kvh, grp=grp, scale=hd ** -0.5, d=d,
                          has_sinks=sinks is not None, want_lse=want_lse),
        grid=(n,),
        in_specs=in_specs,
        out_specs=out_specs,
        out_shape=out_shape,
        compiler_params=_cparams("arbitrary"),
        name=f"attn_sample_d{d}",
    )(*args)
    return (outs[0], outs[1]) if want_lse else (outs[0], None)


def _dil_merge_kernel(*refs, counts, qh, hd):
    no = sum(counts)
    o_refs, l_refs, out_ref = refs[:no], refs[no:no + len(counts)], refs[-1]
    ls = [r[...] for r in l_refs]
    mx = functools.reduce(jnp.maximum, ls)
    es = [jnp.exp(v - mx) for v in ls]
    tot = functools.reduce(lambda a, b: a + b, es)
    wts = [e / tot for e in es]
    for h in range(qh):
        acc, first = None, 0
        for cnt, wt in zip(counts, wts):
            if cnt == 1:
                piece = o_refs[first][:, h * hd:(h + 1) * hd]
            else:
                piece = o_refs[first + h % cnt][:, (h // cnt) * hd:(h // cnt + 1) * hd]
            first += cnt
            term = wt[:, h:h + 1] * piece
            acc = term if acc is None else acc + term
        out_ref[:, h * hd:(h + 1) * hd] = acc.astype(out_ref.dtype)


def dil_merge(outs, lses, tm, out_dtype):
    groups = [o if isinstance(o, (list, tuple)) else [o] for o in outs]
    flat = [a for grp_arrays in groups for a in grp_arrays]
    m = flat[0].shape[0]
    c = DIL_QH * DIL_HD
    return pl.pallas_call(
        functools.partial(_dil_merge_kernel, counts=tuple(len(g) for g in groups), qh=DIL_QH, hd=DIL_HD),
        grid=(m // tm,),
        in_specs=[pl.BlockSpec((tm, a.shape[1]), lambda i: (i, 0)) for a in flat]
        + [pl.BlockSpec((tm, LANES), lambda i: (i, 0))] * len(lses),
        out_specs=pl.BlockSpec((tm, c), lambda i: (i, 0)),
        out_shape=jax.ShapeDtypeStruct((m, c), out_dtype),
        compiler_params=_cparams("arbitrary"),
        name="dil_merge",
    )(*flat, *lses)


def _conv_silu(xp_ref, tail_ref, x_ref, w_ref, b_ref, q):
    xp_ref[0:8, :] = tail_ref[...]
    xp_ref[8:8 + q, :] = x_ref[...]
    y = b_ref[...]
    for i in range(4):
        y = y + xp_ref[5 + i:5 + i + q, :] * w_ref[i:i + 1, :]
    tail_ref[...] = xp_ref[q:q + 8, :]
    return y * _sigmoid(y)


def _ssd_kernel(*refs, q, ng, has_state):
    (z_ref, x_ref, b_ref, c_ref, bufx_ref, bufb_ref, bufc_ref, dtc_ref, dtr_ref, alr_ref, alc_ref, dpar_ref,
     cwx_ref, cwb_ref, cwc_ref, cbx_ref, cbb_ref, cbc_ref, nw_ref) = refs[:19]
    refs = refs[19:]
    if has_state:
        h0_ref, refs = refs[0], refs[1:]
    y_ref, st_ref, xpx, xpb, xpc, tlx, tlb, tlc, state, ysc = refs
    c = pl.program_id(2)
    gw, s = SSM_GW, SSM_S

    @pl.when(c == 0)
    def _():
        tlx[...] = bufx_ref[...]
        tlb[...] = bufb_ref[...]
        tlc[...] = bufc_ref[...]
        state[...] = h0_ref[...] if has_state else jnp.zeros(state.shape, F32)

    xs_all = _conv_silu(xpx, tlx, x_ref, cwx_ref, cbx_ref, q)
    bm_all = _conv_silu(xpb, tlb, b_ref, cwb_ref, cbb_ref, q)
    cm_all = _conv_silu(xpc, tlc, c_ref, cwc_ref, cbc_ref, q)

    li = lax.broadcasted_iota(jnp.int32, (q, q), 0)
    mi = lax.broadcasted_iota(jnp.int32, (q, q), 1)
    causal = li >= mi
    tri_c = causal.astype(F32)
    tri_r = (li <= mi).astype(F32)
    low = lax.broadcasted_iota(jnp.int32, (q, LANES), 1) < SSM_P

    def widen(v):
        return jnp.concatenate([jnp.where(low, v[:, 2 * j:2 * j + 1], v[:, 2 * j + 1:2 * j + 2])
                                for j in range(gw // LANES)], axis=1)

    for k in range(ng):
        xs = xs_all[:, k * gw:(k + 1) * gw]
        bmb = bm_all[:, k * s:(k + 1) * s].astype(BF16)
        cmb = cm_all[:, k * s:(k + 1) * s].astype(BF16)
        dt_c = dtc_ref[k]
        dt_r = dtr_ref[k]
        a_r = -jnp.exp(alr_ref[k])
        a_c = -jnp.exp(alc_ref[k])
        cs_c = jnp.dot(tri_c, dt_c * a_r, precision=HI, preferred_element_type=F32)
        cs_r = jnp.dot(dt_r * a_c, tri_r, precision=HI, preferred_element_type=F32)
        cs_last = cs_c[q - 1:q, :]

        xdt = xs * widen(dt_c)
        xe = (xdt * widen(jnp.exp(cs_last - cs_c))).astype(BF16)
        xdtb = xdt.astype(BF16)
        cb = _nt_dot(cmb, bmb)
        st = state[k]
        y_off = _nt_dot(cmb, st.astype(BF16)) * widen(jnp.exp(cs_c))
        for h in range(SSM_HPG):
            seg = cs_c[:, h:h + 1] - cs_r[h:h + 1, :]
            gm = (cb * jnp.exp(jnp.where(causal, seg, NEG_INF))).astype(BF16)
            ysc[:, k * gw + h * SSM_P:k * gw + (h + 1) * SSM_P] = jnp.dot(
                gm, xdtb[:, h * SSM_P:(h + 1) * SSM_P], preferred_element_type=F32)
        y = ysc[:, k * gw:(k + 1) * gw] + y_off + xs * dpar_ref[:, k * gw:(k + 1) * gw]
        new_st = _tn_dot(xe, bmb)
        dec_last = jnp.exp(cs_r[:, q - 1:q])
        st = jnp.concatenate([st[h * SSM_P:(h + 1) * SSM_P, :] * dec_last[h:h + 1, :] for h in range(SSM_HPG)],
                             axis=0) + new_st
        state[k] = st
        st_ref[k] = st

        z = z_ref[:, k * gw:(k + 1) * gw]
        y = y * (z * _sigmoid(z))
        y = y * lax.rsqrt(jnp.mean(y * y, axis=-1, keepdims=True) + RMS_EPS) * nw_ref[:, k * gw:(k + 1) * gw]
        y_ref[:, k * gw:(k + 1) * gw] = y.astype(y_ref.dtype)


def ssd_core(n, l, zx, dt, conv_buf8, h0, p, out_dtype, ng):
    q = SSM_CHUNK if l % SSM_CHUNK == 0 else l
    nc = l // q
    g = SSM_G
    gw, s = ng * SSM_GW, ng * SSM_S
    xb0 = SSM_DI // gw
    bb0 = 2 * SSM_DI // s
    cb0 = bb0 + g // ng
    kb0 = SSM_DI // s
    kc0 = kb0 + g // ng
    dt4 = dt.reshape(n, l, g, SSM_HPG)
    dt_c = jnp.transpose(dt4, (0, 2, 1, 3))
    dt_r = jnp.transpose(dt4, (0, 2, 3, 1))
    alog = p["ssd_a_log"].reshape(g, SSM_HPG)
    row = lambda b, gi, c: (b * nc + c)
    in_specs = [
        pl.BlockSpec((q, gw), lambda b, gi, c: (row(b, gi, c), gi)),
        pl.BlockSpec((q, gw), lambda b, gi, c: (row(b, gi, c), xb0 + gi)),
        pl.BlockSpec((q, s), lambda b, gi, c: (row(b, gi, c), bb0 + gi)),
        pl.BlockSpec((q, s), lambda b, gi, c: (row(b, gi, c), cb0 + gi)),
        pl.BlockSpec((None, 8, gw), lambda b, gi, c: (b, 0, gi)),
        pl.BlockSpec((None, 8, s), lambda b, gi, c: (b, 0, kb0 + gi)),
        pl.BlockSpec((None, 8, s), lambda b, gi, c: (b, 0, kc0 + gi)),
        pl.BlockSpec((None, ng, q, SSM_HPG), lambda b, gi, c: (b, gi, c, 0)),
        pl.BlockSpec((None, ng, SSM_HPG, q), lambda b, gi, c: (b, gi, 0, c)),
        pl.BlockSpec((ng, 1, SSM_HPG), lambda b, gi, c: (gi, 0, 0)),
        pl.BlockSpec((ng, SSM_HPG, 1), lambda b, gi, c: (gi, 0, 0)),
        pl.BlockSpec((1, gw), lambda b, gi, c: (0, gi)),
        pl.BlockSpec((4, gw), lambda b, gi, c: (0, gi)),
        pl.BlockSpec((4, s), lambda b, gi, c: (0, kb0 + gi)),
        pl.BlockSpec((4, s), lambda b, gi, c: (0, kc0 + gi)),
        pl.BlockSpec((1, gw), lambda b, gi, c: (0, gi)),
        pl.BlockSpec((1, s), lambda b, gi, c: (0, kb0 + gi)),
        pl.BlockSpec((1, s), lambda b, gi, c: (0, kc0 + gi)),
        pl.BlockSpec((1, gw), lambda b, gi, c: (0, gi)),
    ]
    cw = p["ssd_conv_w"][0]
    cbias = p["ssd_conv_b"]
    args = [zx, zx, zx, zx, conv_buf8, conv_buf8, conv_buf8, dt_c, dt_r,
            alog.reshape(g, 1, SSM_HPG), alog.reshape(g, SSM_HPG, 1),
            jnp.repeat(p["ssd_d"].reshape(-1), SSM_P).reshape(1, SSM_DI),
            cw, cw, cw, cbias, cbias, cbias, p["ssd_norm"]]
    st_spec = pl.BlockSpec((None, ng, SSM_GW, SSM_S), lambda b, gi, c: (b, gi, 0, 0))
    if h0 is not None:
        in_specs.append(st_spec)
        args.append(h0.reshape(n, g, SSM_GW, SSM_S))
    y, st = pl.pallas_call(
        functools.partial(_ssd_kernel, q=q, ng=ng, has_state=h0 is not None),
        grid=(n, g // ng, nc),
        in_specs=in_specs,
        out_specs=[pl.BlockSpec((q, gw), lambda b, gi, c: (row(b, gi, c), gi)), st_spec],
        out_shape=[jax.ShapeDtypeStruct((n * l, SSM_DI), out_dtype),
                   jax.ShapeDtypeStruct((n, g, SSM_GW, SSM_S), F32)],
        scratch_shapes=[pltpu.VMEM((8 + q, gw), F32), pltpu.VMEM((8 + q, s), F32), pltpu.VMEM((8 + q, s), F32),
                        pltpu.VMEM((8, gw), F32), pltpu.VMEM((8, s), F32), pltpu.VMEM((8, s), F32),
                        pltpu.VMEM((ng, SSM_GW, SSM_S), F32), pltpu.VMEM((q, gw), F32)],
        compiler_params=_cparams("arbitrary", "arbitrary", "arbitrary"),
        name="ssd_core",
    )(*args)
    return y, st.reshape(n, SSM_HEADS, SSM_P, SSM_S)


def _lru_kernel(*refs, tl, has_state):
    (gate_ref, xb_ref, buf_ref, cw_ref, cb_ref, wr_ref, wi_ref, br_ref, bi_ref, lam_ref) = refs[:10]
    refs = refs[10:]
    if has_state:
        h0_ref, refs = refs[0], refs[1:]
    y_ref, last_ref, xp, tail, a_sc, u_sc, hs_sc, hcar, wrb, wib = refs
    t = pl.program_id(1)

    @pl.when((pl.program_id(0) == 0) & (t == 0))
    def _():
        wrb[...] = wr_ref[...].astype(BF16)
        wib[...] = wi_ref[...].astype(BF16)

    @pl.when(t == 0)
    def _():
        tail[...] = buf_ref[...]
        hcar[...] = h0_ref[...] if has_state else jnp.zeros(hcar.shape, F32)

    xp[0:8, :] = tail[...]
    xp[8:8 + tl, :] = xb_ref[...]
    xc = cb_ref[...]
    for i in range(4):
        xc = xc + xp[5 + i:5 + i + tl, :] * cw_ref[i:i + 1, :]
    tail[...] = xp[tl:tl + 8, :]

    xcb = xc.astype(BF16)
    bd = xc.shape[1] // LRU_BLOCKS
    rs, is_ = [], []
    for b in range(LRU_BLOCKS):
        xblk = xcb[:, b * bd:(b + 1) * bd]
        rs.append(jnp.dot(xblk, wrb[b], preferred_element_type=F32))
        is_.append(jnp.dot(xblk, wib[b], preferred_element_type=F32))
    r = _sigmoid(jnp.concatenate(rs, axis=1) + br_ref[...])
    ig = _sigmoid(jnp.concatenate(is_, axis=1) + bi_ref[...])
    log_a = -LRU_C * r * jax.nn.softplus(-lam_ref[...])
    a = jnp.exp(log_a)
    a_sc[...] = a
    u_sc[...] = jnp.sqrt(-jnp.tanh(log_a) * (a * a + 1.0)) * (ig * xc)

    def step(i, h):
        h = a_sc[pl.ds(i, 1), :] * h + u_sc[pl.ds(i, 1), :]
        hs_sc[pl.ds(i, 1), :] = h
        return h

    h = lax.fori_loop(0, tl, step, hcar[...], unroll=8)
    hcar[...] = h
    last_ref[...] = h
    y_ref[...] = (hs_sc[...] * jax.nn.gelu(gate_ref[...])).astype(y_ref.dtype)


def lru_core(n, l, gx, conv_buf8, h0, p, tl, out_dtype):
    wd = gx.shape[1] // 2
    nt = l // tl
    bd = wd // LRU_BLOCKS
    vec = lambda a: a.reshape(1, wd)
    cst2 = lambda b, t: (0, 0)
    in_specs = [pl.BlockSpec((tl, wd), lambda b, t: (b * nt + t, 0)),
                pl.BlockSpec((tl, wd), lambda b, t: (b * nt + t, 1)),
                pl.BlockSpec((None, 8, wd), lambda b, t: (b, 0, 0)),
                pl.BlockSpec((4, wd), cst2), pl.BlockSpec((1, wd), cst2),
                pl.BlockSpec((LRU_BLOCKS, bd, bd), lambda b, t: (0, 0, 0)),
                pl.BlockSpec((LRU_BLOCKS, bd, bd), lambda b, t: (0, 0, 0)),
                pl.BlockSpec((1, wd), cst2), pl.BlockSpec((1, wd), cst2), pl.BlockSpec((1, wd), cst2)]
    args = [gx, gx, conv_buf8, p["lru_conv_w"][0], vec(p["lru_conv_b"]), p["lru_w_r"][0], p["lru_w_i"][0],
            vec(p["lru_b_r"]), vec(p["lru_b_i"]), vec(p["lru_lam"])]
    if h0 is not None:
        in_specs.append(pl.BlockSpec((None, 1, wd), lambda b, t: (b, 0, 0)))
        args.append(h0.reshape(n, 1, wd))
    y, last = pl.pallas_call(
        functools.partial(_lru_kernel, tl=tl, has_state=h0 is not None),
        grid=(n, nt),
        in_specs=in_specs,
        out_specs=[pl.BlockSpec((tl, wd), lambda b, t: (b * nt + t, 0)),
                   pl.BlockSpec((None, 1, wd), lambda b, t: (b, 0, 0))],
        out_shape=[jax.ShapeDtypeStruct((n * l, wd), out_dtype), jax.ShapeDtypeStruct((n, 1, wd), F32)],
        scratch_shapes=[pltpu.VMEM((8 + tl, wd), F32), pltpu.VMEM((8, wd), F32),
                        pltpu.VMEM((tl, wd), F32), pltpu.VMEM((tl, wd), F32), pltpu.VMEM((tl, wd), F32),
                        pltpu.VMEM((1, wd), F32),
                        pltpu.VMEM((LRU_BLOCKS, bd, bd), BF16), pltpu.VMEM((LRU_BLOCKS, bd, bd), BF16)],
        compiler_params=_cparams("arbitrary", "arbitrary"),
        name="lru_core",
    )(*args)
    return y, last.reshape(n, wd)


def _pad_buf8(buf):
    return jnp.pad(buf, ((0, 0), (5, 0), (0, 0)))


def _trunk(seq, x, mod, pos, cache, p):
    n, l, m = seq.n, seq.l, seq.m
    prompt = seq.prompt
    d = x.shape[1]
    act = BF16 if prompt else F32
    new = {}
    reps = 1 if prompt else n

    h = norm_mod(seq, x, p["g_mix"], 0, mod, 0)
    rope64 = rope_tables(pos, SWA_HD, reps)
    nq = SWA_QH * SWA_HD
    q = matmul(seq, h, p["swa_w_qkv"], 0, col0=0, ncols=nq, tn=1024, out_dtype=act, epi="rope", rope=rope64,
               n_rope=nq // 1024, hd=SWA_HD, name="swa_q")
    kv = matmul(seq, h, p["swa_w_qkv"], 0, col0=nq, tn=256, out_dtype=F32, epi="rope", rope=rope64,
                n_rope=1, hd=SWA_HD, name="swa_kv")
    akw = dict(d=1, hd=SWA_HD, kvh=SWA_KVH, grp=SWA_QH // SWA_KVH, q_blk=0, k_blk=0, v_blk=1,
               sinks=p["swa_sinks"][0], want_lse=False)
    if prompt:
        o, _ = attn_prompt(n, l, q, kv, out_dtype=BF16, **akw)
        new["swa_kv"] = kv.reshape(n, l, 2, SWA_KVH, SWA_HD)[:, l - min(SWA_WINDOW, l):]
    else:
        c = cache["swa_kv"][0]
        o, _ = attn_sample(n, l, q, kv, c.reshape(n, c.shape[1], 2 * SWA_KVH, SWA_HD), **akw)
        new["swa_kv"] = kv.reshape(n, l, 2, SWA_KVH, SWA_HD)
    x = matmul(seq, o, p["swa_w_o"], 0, tn=512, out_dtype=F32, epi="resid", res=x, mod=mod, mod_layer=0, mod_k=2,
               name="swa_o")
    x = ffn(seq, norm_mod(seq, x, p["g_ffn"], 0, mod, 3), p["w_ff1"], p["w_ff2"], 0, x, mod)

    h = norm_mod(seq, x, p["g_mix"], 1, mod, 0)
    nzx = SSM_DI + SSM_DI + 2 * SSM_G * SSM_S
    zx = matmul(seq, h, p["ssd_w_in"], 0, col0=0, ncols=nzx, tn=1024, out_dtype=F32, name="ssd_in")
    w_dt = p["ssd_w_in"][:, :, nzx:]
    dt = matmul(seq, h, w_dt, 0, tn=SSM_HEADS, out_dtype=F32, bias=p["ssd_dt_bias"], epi="softplus", name="ssd_dt")
    zx3 = zx.reshape(n, l, nzx)
    if prompt:
        buf8 = jnp.zeros((n, 8, nzx - SSM_DI), F32)
        h0 = None
        new["ssd_conv"] = zx3[:, l - 3:, SSM_DI:]
    else:
        buf8 = _pad_buf8(cache["ssd_conv"][0])
        h0 = cache["ssd"][0]
        new["ssd_conv"] = jnp.concatenate([cache["ssd_conv"][0], zx3[:, :, SSM_DI:]], axis=1)[:, -3:]
    y, new["ssd"] = ssd_core(n, l, zx, dt, buf8, h0, p, act, ng=4 if prompt else SSM_G)
    x = matmul(seq, y, p["ssd_w_out"], 0, tn=512, tm=min(seq.tm, 512), out_dtype=F32, epi="resid", res=x, mod=mod,
               mod_layer=1, mod_k=2, name="ssd_out")
    x = ffn(seq, norm_mod(seq, x, p["g_ffn"], 1, mod, 3), p["w_ff1"], p["w_ff2"], 1, x, mod)

    h = norm_mod(seq, x, p["g_mix"], 2, mod, 0)
    rope128 = rope_tables(pos, DIL_HD, reps)
    ng = len(DIL_PATTERN)
    nq = ng * DIL_QH * DIL_HD
    q = matmul(seq, h, p["dil_w_qkv"], 0, col0=0, ncols=nq, tn=1024, out_dtype=F32, epi="rope", rope=rope128,
               n_rope=nq // 1024, hd=DIL_HD, name="dil_q")
    ck = DIL_KVH * DIL_HD
    grp = DIL_QH // DIL_KVH
    outs, lses = [], []
    for g, ((w, dil), key) in enumerate(zip(DIL_PATTERN, ("dil_kv_w128", "dil_kv_w512", "dil_kv_w2048"))):
        kv = matmul(seq, h, p["dil_w_qkv"], 0, col0=nq + g * ck, ncols=2 * ck, cstride=ng, tn=ck, out_dtype=F32,
                    epi="rope", rope=rope128, n_rope=1, hd=DIL_HD, name="dil_kv")
        akw = dict(d=dil, hd=DIL_HD, kvh=DIL_KVH, grp=grp, q_blk=g, k_blk=0, v_blk=1)
        kv_g = kv.reshape(n, l, 2, DIL_KVH, DIL_HD)
        if not prompt:
            c = cache[key][0]
            o, lse = attn_sample(n, l, q, kv, c.reshape(n, c.shape[1], 2 * DIL_KVH, DIL_HD), **akw)
            new[key] = kv_g
        else:
            if dil == 1:
                o, lse = attn_prompt(n, l, q, kv, **akw)
            else:
                o, lse = attn_dil_prompt(n, l, q, kv, d=dil, kvh=DIL_KVH, grp=grp, q_head0=g * DIL_QH,
                                         k_head0=0, v_head0=DIL_KVH)
            new[key] = kv_g[:, l - min(w, l):]
        outs.append(o)
        lses.append(lse)
    o = dil_merge(outs, lses, seq.tm if not prompt else 512, act)
    x = matmul(seq, o, p["dil_w_o"], 0, tn=512, out_dtype=F32, epi="resid", res=x, mod=mod, mod_layer=2, mod_k=2,
               name="dil_o")
    x = ffn(seq, norm_mod(seq, x, p["g_ffn"], 2, mod, 3), p["w_ff1"], p["w_ff2"], 2, x, mod)

    h = norm_mod(seq, x, p["g_mix"], 3, mod, 0)
    gx = matmul(seq, h, p["lru_w_in"], 0, tn=1024, out_dtype=F32, bias=p["lru_b_in"], name="lru_in")
    wd = gx.shape[1] // 2
    gx3 = gx.reshape(n, l, 2 * wd)
    if prompt:
        buf8 = jnp.zeros((n, 8, wd), F32)
        h0 = None
        new["lru_conv"] = gx3[:, l - 3:, wd:]
    else:
        buf8 = _pad_buf8(cache["lru_conv"][0])
        h0 = cache["lru"][0]
        new["lru_conv"] = jnp.concatenate([cache["lru_conv"][0], gx3[:, :, wd:]], axis=1)[:, -3:]
    y, new["lru"] = lru_core(n, l, gx, buf8, h0, p, 256 if prompt else l, act)
    x = matmul(seq, y, p["lru_w_out"], 0, tn=512, out_dtype=F32, epi="resid", res=x, mod=mod, mod_layer=3, mod_k=2,
               name="lru_out")
    x = ffn(seq, norm_mod(seq, x, p["g_ffn"], 3, mod, 3), p["w_ff1"], p["w_ff2"], 3, x, mod)

    y = final_norm(x, p["g_final"], min(seq.tm, 512))
    return y.reshape(n, l, d), {k: v[None] for k, v in new.items()}


def kernel(x_prompt, x_sample, cache_swa_kv, state_ssd_conv, state_ssd, cache_dil_kv_w128, cache_dil_kv_w512,
           cache_dil_kv_w2048, state_lru_conv, state_lru, c_prompt, c_sample, w_ada, b_ada, g_mix, g_ffn,
           w_ff1, w_ff2, g_final, swa_w_qkv, swa_sinks, swa_w_o, ssd_w_in, ssd_conv_w, ssd_conv_b,
           ssd_dt_bias, ssd_a_log, ssd_d, ssd_norm, ssd_w_out, dil_w_qkv, dil_w_o, lru_w_in, lru_b_in,
           lru_conv_w, lru_conv_b, lru_w_r, lru_b_r, lru_w_i, lru_b_i, lru_lam, lru_w_out):
    p = dict(g_mix=g_mix, g_ffn=g_ffn, w_ff1=w_ff1.astype(BF16), w_ff2=w_ff2.astype(BF16), g_final=g_final,
             swa_w_qkv=swa_w_qkv, swa_sinks=swa_sinks, swa_w_o=swa_w_o,
             ssd_w_in=ssd_w_in, ssd_conv_w=ssd_conv_w, ssd_conv_b=ssd_conv_b, ssd_dt_bias=ssd_dt_bias,
             ssd_a_log=ssd_a_log, ssd_d=ssd_d, ssd_norm=ssd_norm, ssd_w_out=ssd_w_out,
             dil_w_qkv=dil_w_qkv, dil_w_o=dil_w_o,
             lru_w_in=lru_w_in, lru_b_in=lru_b_in, lru_conv_w=lru_conv_w, lru_conv_b=lru_conv_b,
             lru_w_r=lru_w_r, lru_b_r=lru_b_r, lru_w_i=lru_w_i, lru_b_i=lru_b_i, lru_lam=lru_lam,
             lru_w_out=lru_w_out)
    cache = dict(swa_kv=cache_swa_kv, ssd_conv=state_ssd_conv, ssd=state_ssd, dil_kv_w128=cache_dil_kv_w128,
                 dil_kv_w512=cache_dil_kv_w512, dil_kv_w2048=cache_dil_kv_w2048, lru_conv=state_lru_conv,
                 lru=state_lru)
    nb, l, d = x_prompt.shape
    ns, ls, _ = x_sample.shape
    depth = w_ada.shape[0]
    rows = -(-(nb + ns) // 16) * 16
    c_all = jnp.concatenate([c_prompt, c_sample, jnp.zeros((rows - nb - ns, d), F32)], axis=0)
    mod = ada_mod(c_all, w_ada, b_ada)
    mod_p = mod.reshape(depth, rows, 1, 6 * d)
    mod_s = jnp.repeat(mod[:, nb:nb + ns], ls, axis=1).reshape(depth, 1, ns * ls, 6 * d)

    seq_p = Seq(nb, l, 1024, True)
    seq_s = Seq(ns, ls, ns * ls, False)
    y_p, sp = _trunk(seq_p, x_prompt.reshape(nb * l, d), mod_p, jnp.arange(l, dtype=jnp.int32), None, p)
    y_s, ss = _trunk(seq_s, x_sample.reshape(ns * ls, d), mod_s, PAST_LEN + jnp.arange(ls, dtype=jnp.int32), cache, p)
    return (y_p, y_s,
            sp["swa_kv"], ss["swa_kv"],
            sp["ssd_conv"], ss["ssd_conv"],
            sp["ssd"], ss["ssd"],
            sp["dil_kv_w128"], ss["dil_kv_w128"],
            sp["dil_kv_w512"], ss["dil_kv_w512"],
            sp["dil_kv_w2048"], ss["dil_kv_w2048"],
            sp["lru_conv"], ss["lru_conv"],
            sp["lru"], ss["lru"])
```

```python
import functools
import math

import jax
import jax.numpy as jnp
from jax import lax
from jax.experimental import pallas as pl
from jax.experimental.pallas import tpu as pltpu

F32 = jnp.float32
BF16 = jnp.bfloat16
HI = lax.Precision.HIGHEST

RMS_EPS = 1e-6
ROPE_THETA = 10000.0
NEG_INF = -1e30
LANES = 128
MXU_COLS = 256
VMEM_LIMIT = 56 * 1024 * 1024

PAST_LEN = 16384
ATTN_BLOCK = 128
SWA_WINDOW, SWA_HD, SWA_QH, SWA_KVH = 128, 64, 32, 4
DIL_PATTERN = ((128, 1), (512, 4), (2048, 16))
DIL_HD, DIL_QH, DIL_KVH = 128, 16, 4
SSM_HEADS, SSM_P, SSM_S, SSM_G, SSM_HPG, SSM_CHUNK = 64, 64, 128, 8, 8, 128
SSM_DI = SSM_HEADS * SSM_P
SSM_GW = SSM_HPG * SSM_P
LRU_BLOCKS, LRU_C = 8, 8.0


def _cparams(*sem):
    return pltpu.CompilerParams(dimension_semantics=sem, vmem_limit_bytes=VMEM_LIMIT)


def _nt_dot(a, b):
    return lax.dot_general(a, b, (((1,), (1,)), ((), ())), preferred_element_type=F32)


def _sigmoid(x):
    return 0.5 * jnp.tanh(0.5 * x) + 0.5


def _tn_dot(a, b):
    return lax.dot_general(a, b, (((0,), (0,)), ((), ())), preferred_element_type=F32)


class Seq:
    def __init__(self, n, l, tm, prompt):
        self.n, self.l, self.m, self.tm, self.prompt = n, l, n * l, tm, prompt

    def mod_spec(self, layer, k, d, row_axis):
        tm, l = self.tm, self.l
        if self.prompt:
            return pl.BlockSpec((None, None, 1, d), lambda *g: (layer, (g[row_axis] * tm) // l, 0, k))
        return pl.BlockSpec((None, None, tm, d), lambda *g: (layer, 0, g[row_axis], k))


def _ada_kernel(c_ref, w_ref, b_ref, o_ref):
    c = c_ref[...]
    cond = (c * jax.nn.sigmoid(c)).astype(BF16)
    o_ref[...] = jnp.dot(cond, w_ref[...].astype(BF16), preferred_element_type=F32) + b_ref[...]


def ada_mod(c_all, w_ada, b_ada, tn=1024):
    depth, d, n6 = w_ada.shape
    r = c_all.shape[0]
    return pl.pallas_call(
        _ada_kernel,
        grid=(depth, n6 // tn),
        in_specs=[pl.BlockSpec((r, d), lambda a, j: (0, 0)),
                  pl.BlockSpec((None, d, tn), lambda a, j: (a, 0, j)),
                  pl.BlockSpec((None, 1, tn), lambda a, j: (a, 0, j))],
        out_specs=pl.BlockSpec((None, r, tn), lambda a, j: (a, 0, j)),
        out_shape=jax.ShapeDtypeStruct((depth, r, n6), F32),
        compiler_params=_cparams("arbitrary", "arbitrary"),
        name="ada_mod",
    )(c_all, w_ada, b_ada.reshape(depth, 1, n6))


def _norm_mod_kernel(x_ref, g_ref, sh_ref, sc_ref, o_ref):
    x = x_ref[...]
    y = x * lax.rsqrt(jnp.mean(x * x, axis=-1, keepdims=True) + RMS_EPS) * g_ref[...]
    o_ref[...] = (y * (1.0 + sc_ref[...]) + sh_ref[...]).astype(o_ref.dtype)


def _norm_kernel(x_ref, g_ref, o_ref):
    x = x_ref[...]
    o_ref[...] = x * lax.rsqrt(jnp.mean(x * x, axis=-1, keepdims=True) + RMS_EPS) * g_ref[...]


def norm_mod(seq, x, g, layer, mod, k_shift):
    m, d = x.shape
    tm = min(seq.tm, 512)
    sub = Seq(seq.n, seq.l, tm, seq.prompt)
    return pl.pallas_call(
        _norm_mod_kernel,
        grid=(m // tm,),
        in_specs=[pl.BlockSpec((tm, d), lambda i: (i, 0)),
                  pl.BlockSpec((None, 1, d), lambda i: (layer, 0, 0)),
                  sub.mod_spec(layer, k_shift, d, 0),
                  sub.mod_spec(layer, k_shift + 1, d, 0)],
        out_specs=pl.BlockSpec((tm, d), lambda i: (i, 0)),
        out_shape=jax.ShapeDtypeStruct((m, d), BF16),
        compiler_params=_cparams("arbitrary"),
        name="norm_mod",
    )(x, g.reshape(g.shape[0], 1, d), mod, mod)


def final_norm(x, g, tm):
    m, d = x.shape
    return pl.pallas_call(
        _norm_kernel,
        grid=(m // tm,),
        in_specs=[pl.BlockSpec((tm, d), lambda i: (i, 0)), pl.BlockSpec((1, d), lambda i: (0, 0))],
        out_specs=pl.BlockSpec((tm, d), lambda i: (i, 0)),
        out_shape=jax.ShapeDtypeStruct((m, d), F32),
        compiler_params=_cparams("arbitrary"),
        name="final_norm",
    )(x, g.reshape(1, d))


def _rope_cols(y, cos, sin, hd):
    outs = []
    for c in range(y.shape[1] // LANES):
        yc = y[:, c * LANES:(c + 1) * LANES]
        if hd == LANES:
            partner = pltpu.roll(yc, LANES // 2, axis=1)
        else:
            lane = lax.broadcasted_iota(jnp.int32, yc.shape, 1)
            partner = jnp.where(lane % hd < hd // 2, pltpu.roll(yc, LANES - hd // 2, axis=1),
                                pltpu.roll(yc, hd // 2, axis=1))
        outs.append(yc * cos + partner * sin)
    return outs[0] if len(outs) == 1 else jnp.concatenate(outs, axis=1)


def _mm_kernel(*refs, epi, n_rope, all_rope, hd, has_bias):
    x_ref, w_ref = refs[0], refs[1]
    o_ref, wb_ref = refs[-2], refs[-1]
    extra = refs[2:-2]

    @pl.when(pl.program_id(1) == 0)
    def _():
        wb_ref[...] = w_ref[...].astype(BF16)

    if has_bias:
        bias_ref, extra = extra[0], extra[1:]
    xb = x_ref[...].astype(BF16)
    tn = o_ref.shape[1]
    sub = min(tn, MXU_COLS)
    for c in range(tn // sub):
        cs = slice(c * sub, (c + 1) * sub)
        y = jnp.dot(xb, wb_ref[:, cs], preferred_element_type=F32)
        if has_bias:
            y = y + bias_ref[:, cs]
        if epi == "none":
            out = y
        elif epi == "softplus":
            out = jax.nn.softplus(y)
        elif epi == "resid":
            res_ref, gate_ref = extra
            out = res_ref[:, cs] + gate_ref[:, cs] * y
        elif epi == "rope":
            cos_ref, sin_ref = extra
            out = _rope_cols(y, cos_ref[...], sin_ref[...], hd)
            if not all_rope:
                out = jnp.where(pl.program_id(0) < n_rope, out, y)
        else:
            raise ValueError(epi)
        o_ref[:, cs] = out.astype(o_ref.dtype)


def matmul(seq, x, w, layer, *, col0=0, ncols=None, tn, out_dtype, tm=None, bias=None, epi="none",
           rope=None, n_rope=0, hd=LANES, res=None, mod=None, mod_layer=0, mod_k=0, cstride=1, name="mm"):
    m, k = x.shape
    ntot = w.shape[2]
    ncols = ntot - col0 if ncols is None else ncols
    tm = seq.tm if tm is None else tm
    cb = col0 // tn
    assert col0 % tn == 0 and ncols % tn == 0 and m % tm == 0
    in_specs = [pl.BlockSpec((tm, k), lambda j, i: (i, 0)),
                pl.BlockSpec((None, k, tn), lambda j, i: (layer, 0, cb + j * cstride))]
    args = [x, w]
    if bias is not None:
        in_specs.append(pl.BlockSpec((None, 1, tn), lambda j, i: (0, 0, cb + j * cstride)))
        args.append(bias.reshape(1, 1, -1))
    if epi == "rope":
        cos, sin = rope
        nrt = cos.shape[0] // tm
        for t in (cos, sin):
            in_specs.append(pl.BlockSpec((tm, LANES), lambda j, i: (i % nrt, 0)))
            args.append(t)
    if epi == "resid":
        sub = Seq(seq.n, seq.l, tm, seq.prompt)
        base = sub.mod_spec(mod_layer, mod_k, tn, 1)
        nk = w.shape[2] // tn
        gate_spec = pl.BlockSpec(base.block_shape,
                                 lambda j, i, f=base.index_map: f(j, i)[:3] + (f(j, i)[3] * nk + j,))
        in_specs += [pl.BlockSpec((tm, tn), lambda j, i: (i, j)), gate_spec]
        args += [res, mod]
    return pl.pallas_call(
        functools.partial(_mm_kernel, epi=epi, n_rope=n_rope, all_rope=n_rope >= ncols // tn, hd=hd,
                          has_bias=bias is not None),
        grid=(ncols // tn, m // tm),
        in_specs=in_specs,
        out_specs=pl.BlockSpec((tm, tn), lambda j, i: (i, j)),
        out_shape=jax.ShapeDtypeStruct((m, ncols), out_dtype),
        scratch_shapes=[pltpu.VMEM((k, tn), BF16)],
        compiler_params=_cparams("arbitrary", "arbitrary"),
        name=name,
    )(*args)


def _mm_conv_kernel(*refs, tiles_per_seq, silu, has_bias):
    x_ref, w_ref = refs[0], refs[1]
    k = 3 if has_bias else 2
    cw_ref, cb_ref = refs[k], refs[k + 1]
    o_ref, tails_ref, wb_ref, xp_ref, tail_ref = refs[k + 2:]
    i = pl.program_id(1)

    @pl.when(i == 0)
    def _():
        wb_ref[...] = w_ref[...].astype(BF16)

    @pl.when(i % tiles_per_seq == 0)
    def _():
        tail_ref[...] = jnp.zeros(tail_ref.shape, F32)

    xb = x_ref[...].astype(BF16)
    tm, tn = o_ref.shape
    sub = min(tn, MXU_COLS)
    for c in range(tn // sub):
        cs = slice(c * sub, (c + 1) * sub)
        y = jnp.dot(xb, wb_ref[:, cs], preferred_element_type=F32)
        if has_bias:
            y = y + refs[2][:, cs]
        xp_ref[c, 0:8, :] = tail_ref[:, cs]
        xp_ref[c, 8:8 + tm, :] = y
        acc = cb_ref[:, cs]
        for t in range(4):
            acc = acc + xp_ref[c, 5 + t:5 + t + tm, :] * cw_ref[t:t + 1, cs]
        last = y[tm - 8:tm, :]
        tail_ref[:, cs] = last
        tails_ref[:, cs] = last
        o_ref[:, cs] = (acc * _sigmoid(acc) if silu else acc).astype(o_ref.dtype)


def matmul_conv(seq, x, w, layer, *, col0, ncols, tn, conv_w, conv_b, conv_col0, silu, bias=None, name="mm_conv"):
    m, k = x.shape
    tm = seq.tm
    assert seq.prompt and seq.l % tm == 0 and col0 % tn == 0 and ncols % tn == 0 and conv_col0 % tn == 0
    cb, ccb, tps = col0 // tn, conv_col0 // tn, seq.l // tm
    sub = min(tn, MXU_COLS)
    in_specs = [pl.BlockSpec((tm, k), lambda j, i: (i, 0)),
                pl.BlockSpec((None, k, tn), lambda j, i: (layer, 0, cb + j))]
    args = [x, w]
    if bias is not None:
        in_specs.append(pl.BlockSpec((None, 1, tn), lambda j, i: (0, 0, cb + j)))
        args.append(bias.reshape(1, 1, -1))
    in_specs += [pl.BlockSpec((4, tn), lambda j, i: (0, ccb + j)), pl.BlockSpec((1, tn), lambda j, i: (0, ccb + j))]
    args += [conv_w, conv_b]
    return pl.pallas_call(
        functools.partial(_mm_conv_kernel, tiles_per_seq=tps, silu=silu, has_bias=bias is not None),
        grid=(ncols // tn, m // tm),
        in_specs=in_specs,
        out_specs=[pl.BlockSpec((tm, tn), lambda j, i: (i, j)),
                   pl.BlockSpec((None, 8, tn), lambda j, i: (i // tps, 0, j))],
        out_shape=[jax.ShapeDtypeStruct((m, ncols), F32), jax.ShapeDtypeStruct((seq.n, 8, ncols), F32)],
        scratch_shapes=[pltpu.VMEM((k, tn), BF16), pltpu.VMEM((tn // sub, 8 + tm, sub), F32),
                        pltpu.VMEM((8, tn), F32)],
        compiler_params=_cparams("arbitrary", "arbitrary"),
        name=name,
    )(*args)


def _ffn_kernel(x_ref, w1_ref, w2_ref, res_ref, gate_ref, o_ref, h_ref, *, na, tf):
    s = pl.program_id(1)

    @pl.when(s < na)
    def _():
        h = jnp.dot(x_ref[...], w1_ref[...], preferred_element_type=F32)
        col = pl.multiple_of(s * tf, tf)
        h_ref[:, pl.ds(col, tf)] = jnp.square(jnp.maximum(h, 0.0)).astype(BF16)

    @pl.when(s >= na)
    def _():
        y = jnp.dot(h_ref[...], w2_ref[...], preferred_element_type=F32)
        o_ref[...] = res_ref[...] + gate_ref[...] * y


def ffn(seq, h, w1, w2, layer, res, mod, tf=1024, tn=256):
    m, d = h.shape
    f = w1.shape[2]
    tm = seq.tm
    na, nb = f // tf, d // tn
    sub = Seq(seq.n, seq.l, tm, seq.prompt)
    base = sub.mod_spec(layer, 5, tn, 0)
    ocol = lambda s: jnp.maximum(s - na, 0)
    gate_spec = pl.BlockSpec(base.block_shape,
                             lambda i, s, fm=base.index_map: fm(i, s)[:3] + (5 * nb + ocol(s),))
    return pl.pallas_call(
        functools.partial(_ffn_kernel, na=na, tf=tf),
        grid=(m // tm, na + nb),
        in_specs=[pl.BlockSpec((tm, d), lambda i, s: (i, 0)),
                  pl.BlockSpec((None, d, tf), lambda i, s: (layer, 0, jnp.minimum(s, na - 1))),
                  pl.BlockSpec((None, f, tn), lambda i, s: (layer, 0, ocol(s))),
                  pl.BlockSpec((tm, tn), lambda i, s: (i, ocol(s))),
                  gate_spec],
        out_specs=pl.BlockSpec((tm, tn), lambda i, s: (i, ocol(s))),
        out_shape=jax.ShapeDtypeStruct((m, d), F32),
        scratch_shapes=[pltpu.VMEM((tm, f), BF16)],
        compiler_params=_cparams("arbitrary", "arbitrary"),
        name="ffn",
    )(h, w1, w2, res, mod)


def rope_tables(pos, hd, reps):
    half = hd // 2
    inv = ROPE_THETA ** (-jnp.arange(half, dtype=F32) / half)
    ang = pos.astype(F32)[:, None] * inv[None, :]
    cos = jnp.concatenate([jnp.cos(ang), jnp.cos(ang)], axis=-1)
    sin = jnp.concatenate([-jnp.sin(ang), jnp.sin(ang)], axis=-1)
    lane_reps = LANES // hd
    return jnp.tile(cos, (reps, lane_reps)), jnp.tile(sin, (reps, lane_reps))


def _attn_prompt_kernel(*refs, hd, kvh, grp, scale, has_sinks, want_lse):
    if has_sinks:
        sink_ref, refs = refs[0], refs[1:]
    q_ref, kp_ref, kc_ref, vp_ref, vc_ref = refs[:5]
    o_ref = refs[5]
    lse_ref = refs[6] if want_lse else None
    ub = pl.program_id(2)
    bq = q_ref.shape[0]
    rows = grp * bq
    iq = lax.broadcasted_iota(jnp.int32, (rows, 2 * bq), 0) % bq
    jk = lax.broadcasted_iota(jnp.int32, (rows, 2 * bq), 1)
    mask = (jk >= iq) & (jk <= iq + bq) & ((jk >= bq) | (ub > 0))
    rcol = lax.broadcasted_iota(jnp.int32, (rows, 1), 0)
    lane = lax.broadcasted_iota(jnp.int32, (bq, LANES), 1)
    lse_tile = jnp.zeros((bq, LANES), F32)
    for kh in range(kvh):
        ksl = slice(kh * hd, (kh + 1) * hd)
        kk = jnp.concatenate([kp_ref[:, ksl], kc_ref[:, ksl]], axis=0).astype(BF16)
        vv = jnp.concatenate([vp_ref[:, ksl], vc_ref[:, ksl]], axis=0).astype(BF16)
        qs = jnp.concatenate([q_ref[:, (kh * grp + g) * hd:(kh * grp + g + 1) * hd] for g in range(grp)], axis=0)
        s = jnp.where(mask, _nt_dot(qs.astype(BF16), kk) * scale, NEG_INF)
        mx = jnp.max(s, axis=-1, keepdims=True)
        if has_sinks:
            sk = jnp.zeros((rows, 1), F32)
            for g in range(grp):
                sk = jnp.where(rcol // bq == g, sink_ref[kh * grp + g], sk)
            mx = jnp.maximum(mx, sk)
        p = jnp.exp(s - mx)
        den = jnp.sum(p, axis=-1, keepdims=True)
        if has_sinks:
            den = den + jnp.exp(sk - mx)
        o = jnp.dot(p.astype(BF16), vv, preferred_element_type=F32) / den
        lse = mx + jnp.log(den)
        for g in range(grp):
            h = kh * grp + g
            o_ref[:, h * hd:(h + 1) * hd] = o[g * bq:(g + 1) * bq, :].astype(o_ref.dtype)
            if want_lse:
                lse_tile = jnp.where(lane == h, lse[g * bq:(g + 1) * bq, :], lse_tile)
    if want_lse:
        lse_ref[...] = lse_tile


def attn_prompt(n, l, q, kv, *, d, hd, kvh, grp, q_blk, k_blk, v_blk, sinks=None, want_lse=True, out_dtype=F32):
    bq = ATTN_BLOCK
    cq, ck = grp * kvh * hd, kvh * hd
    nq, nk = q.shape[1] // cq, kv.shape[1] // ck
    lu = l // d
    qv = q.reshape(n, lu, d * q.shape[1])
    kvv = kv.reshape(n, lu, d * kv.shape[1])
    prev = lambda u: jnp.maximum(u - 1, 0)
    in_specs = [pl.BlockSpec((None, bq, cq), lambda b, r, u: (b, u, r * nq + q_blk)),
                pl.BlockSpec((None, bq, ck), lambda b, r, u: (b, prev(u), r * nk + k_blk)),
                pl.BlockSpec((None, bq, ck), lambda b, r, u: (b, u, r * nk + k_blk)),
                pl.BlockSpec((None, bq, ck), lambda b, r, u: (b, prev(u), r * nk + v_blk)),
                pl.BlockSpec((None, bq, ck), lambda b, r, u: (b, u, r * nk + v_blk))]
    args = [qv, kvv, kvv, kvv, kvv]
    if sinks is not None:
        in_specs.insert(0, pl.BlockSpec(memory_space=pltpu.SMEM))
        args.insert(0, sinks)
    out_shape = [jax.ShapeDtypeStruct((n, lu, d * cq), out_dtype)]
    out_specs = [pl.BlockSpec((None, bq, cq), lambda b, r, u: (b, u, r))]
    if want_lse:
        out_shape.append(jax.ShapeDtypeStruct((n, lu, d * LANES), F32))
        out_specs.append(pl.BlockSpec((None, bq, LANES), lambda b, r, u: (b, u, r)))
    outs = pl.pallas_call(
        functools.partial(_attn_prompt_kernel, hd=hd, kvh=kvh, grp=grp, scale=hd ** -0.5,
                          has_sinks=sinks is not None, want_lse=want_lse),
        grid=(n, d, lu // bq),
        in_specs=in_specs,
        out_specs=out_specs,
        out_shape=out_shape,
        compiler_params=_cparams("arbitrary", "arbitrary", "arbitrary"),
        name=f"attn_prompt_d{d}",
    )(*args)
    o = outs[0].reshape(n * l, cq)
    return (o, outs[1].reshape(n * l, LANES)) if want_lse else (o, None)


def _attn_dil_kernel(*refs, d, grp, scale, has_prev):
    q_refs = refs[:grp]
    kp_ref, kc_ref, vp_ref, vc_ref = refs[grp:grp + 4]
    o_refs = refs[grp + 4:2 * grp + 4]
    lse_ref = refs[2 * grp + 4]
    ub, kh = pl.program_id(1), pl.program_id(2)
    bq = ATTN_BLOCK
    nk = 2 * bq if has_prev else bq
    iq = lax.broadcasted_iota(jnp.int32, (grp * bq, nk), 0) % bq
    jk = lax.broadcasted_iota(jnp.int32, (grp * bq, nk), 1)
    if has_prev:
        mask = (jk >= iq) & (jk <= iq + bq) & ((jk >= bq) | (ub > 0))
    else:
        mask = jk <= iq
    lane = lax.broadcasted_iota(jnp.int32, (bq, LANES), 1)

    @pl.when(kh == 0)
    def _():
        lse_ref[...] = jnp.zeros(lse_ref.shape, F32)

    for r in range(d):
        rows = pl.ds(r, bq, stride=d)
        qs = jnp.concatenate([qr[rows, :] for qr in q_refs], axis=0).astype(BF16)
        if has_prev:
            kk = jnp.concatenate([kp_ref[rows, :], kc_ref[rows, :]], axis=0).astype(BF16)
            vv = jnp.concatenate([vp_ref[rows, :], vc_ref[rows, :]], axis=0).astype(BF16)
        else:
            kk = kc_ref[rows, :].astype(BF16)
            vv = vc_ref[rows, :].astype(BF16)
        s = jnp.where(mask, _nt_dot(qs, kk) * scale, NEG_INF)
        mx = jnp.max(s, axis=-1, keepdims=True)
        p = jnp.exp(s - mx)
        den = jnp.sum(p, axis=-1, keepdims=True)
        o = jnp.dot(p.astype(BF16), vv, preferred_element_type=F32) / den
        lse = mx + jnp.log(den)
        tile = lse_ref[r * bq:(r + 1) * bq, :]
        for g in range(grp):
            o_refs[g][rows, :] = o[g * bq:(g + 1) * bq, :]
            tile = jnp.where(lane == kh * grp + g, lse[g * bq:(g + 1) * bq, :], tile)
        lse_ref[r * bq:(r + 1) * bq, :] = tile


def attn_dil_prompt(n, l, q, kv, *, d, kvh, grp, q_head0, k_head0, v_head0):
    hd = LANES
    rt = d * ATTN_BLOCK
    nub = l // rt
    has_prev = nub > 1
    prev = lambda u: jnp.maximum(u - 1, 0)
    cur = lambda u: u
    blk = lambda col, rowf: pl.BlockSpec((rt, hd), lambda b, u, kh: (b * nub + rowf(u), col(kh)))
    q_specs = [blk(lambda kh, g=g: q_head0 + kh * grp + g, cur) for g in range(grp)]
    outs = pl.pallas_call(
        functools.partial(_attn_dil_kernel, d=d, grp=grp, scale=hd ** -0.5, has_prev=has_prev),
        grid=(n, nub, kvh),
        in_specs=q_specs + [blk(lambda kh: k_head0 + kh, prev), blk(lambda kh: k_head0 + kh, cur),
                            blk(lambda kh: v_head0 + kh, prev), blk(lambda kh: v_head0 + kh, cur)],
        out_specs=[blk(lambda kh: kh, cur)] * grp + [blk(lambda kh: 0, cur)],
        out_shape=[jax.ShapeDtypeStruct((n * l, kvh * hd), F32)] * grp + [jax.ShapeDtypeStruct((n * l, LANES), F32)],
        compiler_params=_cparams("arbitrary", "arbitrary", "arbitrary"),
        name=f"attn_dil_d{d}",
    )(*([q] * grp), kv, kv, kv, kv)
    lse = outs[grp].reshape(n * nub, d, ATTN_BLOCK, LANES).transpose(0, 2, 1, 3).reshape(n * l, LANES)
    return list(outs[:grp]), lse


def _attn_sample_kernel(*refs, hd, kvh, grp, scale, d, has_sinks, want_lse):
    if has_sinks:
        sink_ref, refs = refs[0], refs[1:]
    q_ref, kn_ref, vn_ref, c_ref = refs[:4]
    o_ref = refs[4]
    lse_ref = refs[5] if want_lse else None
    lq = q_ref.shape[0]
    flat = len(c_ref.shape) == 2
    w = c_ref.shape[0] // (2 * kvh) if flat else c_ref.shape[0]

    def cache_head(idx):
        return c_ref[pl.ds(idx, w, stride=2 * kvh), :] if flat else c_ref[:, idx, :]

    rows = grp * lq
    row = lax.broadcasted_iota(jnp.int32, (rows, w + lq), 0)
    jk = lax.broadcasted_iota(jnp.int32, (rows, w + lq), 1)
    dist = row % lq + w - jk
    mask = (dist >= 0) & (dist <= w) & (dist % d == 0)
    rcol = lax.broadcasted_iota(jnp.int32, (rows, 1), 0)
    lane = lax.broadcasted_iota(jnp.int32, (lq, LANES), 1)
    lse_tile = jnp.zeros((lq, LANES), F32)
    for kh in range(kvh):
        ksl = slice(kh * hd, (kh + 1) * hd)
        kk = jnp.concatenate([cache_head(kh), kn_ref[:, ksl]], axis=0).astype(BF16)
        vv = jnp.concatenate([cache_head(kvh + kh), vn_ref[:, ksl]], axis=0).astype(BF16)
        qs = jnp.concatenate([q_ref[:, (kh * grp + g) * hd:(kh * grp + g + 1) * hd] for g in range(grp)], axis=0)
        s = jnp.where(mask, _nt_dot(qs.astype(BF16), kk) * scale, NEG_INF)
        mx = jnp.max(s, axis=-1, keepdims=True)
        if has_sinks:
            sk = jnp.zeros((rows, 1), F32)
            for g in range(grp):
                sk = jnp.where(rcol // lq == g, sink_ref[kh * grp + g], sk)
            mx = jnp.maximum(mx, sk)
        p = jnp.exp(s - mx)
        den = jnp.sum(p, axis=-1, keepdims=True)
        if has_sinks:
            den = den + jnp.exp(sk - mx)
        o = jnp.dot(p.astype(BF16), vv, preferred_element_type=F32) / den
        lse = mx + jnp.log(den)
        for g in range(grp):
            h = kh * grp + g
            o_ref[:, h * hd:(h + 1) * hd] = o[g * lq:(g + 1) * lq, :]
            if want_lse:
                lse_tile = jnp.where(lane == h, lse[g * lq:(g + 1) * lq, :], lse_tile)
    if want_lse:
        lse_ref[...] = lse_tile


def attn_sample(n, l, q, kv, cache, *, d, hd, kvh, grp, q_blk, k_blk, v_blk, sinks=None, want_lse=True):
    cq, ck = grp * kvh * hd, kvh * hd
    w = cache.shape[1]
    if hd == LANES:
        cache = cache.reshape(n, w * 2 * kvh, hd)
        cache_spec = pl.BlockSpec((None, w * 2 * kvh, hd), lambda b: (b, 0, 0))
    else:
        cache_spec = pl.BlockSpec((None, w, 2 * kvh, hd), lambda b: (b, 0, 0, 0))
    in_specs = [pl.BlockSpec((l, cq), lambda b: (b, q_blk)),
                pl.BlockSpec((l, ck), lambda b: (b, k_blk)),
                pl.BlockSpec((l, ck), lambda b: (b, v_blk)),
                cache_spec]
    args = [q, kv, kv, cache]
    if sinks is not None:
        in_specs.insert(0, pl.BlockSpec(memory_space=pltpu.SMEM))
        args.insert(0, sinks)
    out_shape = [jax.ShapeDtypeStruct((n * l, cq), F32)]
    out_specs = [pl.BlockSpec((l, cq), lambda b: (b, 0))]
    if want_lse:
        out_shape.append(jax.ShapeDtypeStruct((n * l, LANES), F32))
        out_specs.append(pl.BlockSpec((l, LANES), lambda b: (b, 0)))
    outs = pl.pallas_call(
        functools.partial(_attn_sample_kernel, hd=hd, kvh=kvh, grp=grp, scale=hd ** -0.5, d=d,
                          has_sinks=sinks is not None, want_lse=want_lse),
        grid=(n,),
        in_specs=in_specs,
        out_specs=out_specs,
        out_shape=out_shape,
        compiler_params=_cparams("arbitrary"),
        name=f"attn_sample_d{d}",
    )(*args)
    return (outs[0], outs[1]) if want_lse else (outs[0], None)


def _dil_merge_kernel(*refs, counts, qh, hd):
    no = sum(counts)
    o_refs, l_refs, out_ref = refs[:no], refs[no:no + len(counts)], refs[-1]
    ls = [r[...] for r in l_refs]
    mx = functools.reduce(jnp.maximum, ls)
    es = [jnp.exp(v - mx) for v in ls]
    tot = functools.reduce(lambda a, b: a + b, es)
    wts = [e / tot for e in es]
    for h in range(qh):
        acc, first = None, 0
        for cnt, wt in zip(counts, wts):
            if cnt == 1:
                piece = o_refs[first][:, h * hd:(h + 1) * hd]
            else:
                piece = o_refs[first + h % cnt][:, (h // cnt) * hd:(h // cnt + 1) * hd]
            first += cnt
            term = wt[:, h:h + 1] * piece
            acc = term if acc is None else acc + term
        out_ref[:, h * hd:(h + 1) * hd] = acc.astype(out_ref.dtype)


def dil_merge(outs, lses, tm, out_dtype):
    groups = [o if isinstance(o, (list, tuple)) else [o] for o in outs]
    flat = [a for grp_arrays in groups for a in grp_arrays]
    m = flat[0].shape[0]
    c = DIL_QH * DIL_HD
    return pl.pallas_call(
        functools.partial(_dil_merge_kernel, counts=tuple(len(g) for g in groups), qh=DIL_QH, hd=DIL_HD),
        grid=(m // tm,),
        in_specs=[pl.BlockSpec((tm, a.shape[1]), lambda i: (i, 0)) for a in flat]
        + [pl.BlockSpec((tm, LANES), lambda i: (i, 0))] * len(lses),
        out_specs=pl.BlockSpec((tm, c), lambda i: (i, 0)),
        out_shape=jax.ShapeDtypeStruct((m, c), out_dtype),
        compiler_params=_cparams("arbitrary"),
        name="dil_merge",
    )(*flat, *lses)


def _conv_silu(xp_ref, tail_ref, x_ref, w_ref, b_ref, q):
    xp_ref[0:8, :] = tail_ref[...]
    xp_ref[8:8 + q, :] = x_ref[...]
    y = b_ref[...]
    for i in range(4):
        y = y + xp_ref[5 + i:5 + i + q, :] * w_ref[i:i + 1, :]
    tail_ref[...] = xp_ref[q:q + 8, :]
    return y * _sigmoid(y)


def _ssd_kernel(*refs, q, ng, has_state):
    (z_ref, x_ref, b_ref, c_ref, bufx_ref, bufb_ref, bufc_ref, dtc_ref, dtr_ref, alr_ref, alc_ref, dpar_ref,
     cwx_ref, cwb_ref, cwc_ref, cbx_ref, cbb_ref, cbc_ref, nw_ref) = refs[:19]
    refs = refs[19:]
    if has_state:
        h0_ref, refs = refs[0], refs[1:]
    y_ref, st_ref, xpx, xpb, xpc, tlx, tlb, tlc, state = refs
    c = pl.program_id(2)
    gw, s = SSM_GW, SSM_S

    @pl.when(c == 0)
    def _():
        tlx[...] = bufx_ref[...]
        tlb[...] = bufb_ref[...]
        tlc[...] = bufc_ref[...]
        state[...] = h0_ref[...] if has_state else jnp.zeros(state.shape, F32)

    xs_all = _conv_silu(xpx, tlx, x_ref, cwx_ref, cbx_ref, q)
    bm_all = _conv_silu(xpb, tlb, b_ref, cwb_ref, cbb_ref, q)
    cm_all = _conv_silu(xpc, tlc, c_ref, cwc_ref, cbc_ref, q)

    li = lax.broadcasted_iota(jnp.int32, (q, q), 0)
    mi = lax.broadcasted_iota(jnp.int32, (q, q), 1)
    causal = li >= mi
    tri_c = causal.astype(F32)
    tri_r = (li <= mi).astype(F32)
    low = lax.broadcasted_iota(jnp.int32, (q, LANES), 1) < SSM_P
    on_mxu = q % LANES == 0

    def onehot(width, block):
        r = lax.broadcasted_iota(jnp.int32, (SSM_HPG, width), 0)
        c = lax.broadcasted_iota(jnp.int32, (SSM_HPG, width), 1)
        return (c // block == r).astype(BF16)

    e_chan = onehot(gw, SSM_P)
    e_time = onehot(SSM_HPG * q, q) if on_mxu else None

    def expand(v, e):
        hi = v.astype(BF16)
        r1 = v - hi.astype(F32)
        mid = r1.astype(BF16)
        lo = (r1 - mid.astype(F32)).astype(BF16)
        return (jnp.dot(hi, e, preferred_element_type=F32) + jnp.dot(mid, e, preferred_element_type=F32)
                + jnp.dot(lo, e, preferred_element_type=F32))

    def widen(v):
        if on_mxu:
            return expand(v, e_chan)
        return jnp.concatenate([jnp.where(low, v[:, 2 * j:2 * j + 1], v[:, 2 * j + 1:2 * j + 2])
                                for j in range(gw // LANES)], axis=1)

    for k in range(ng):
        xs = xs_all[:, k * gw:(k + 1) * gw]
        bmb = bm_all[:, k * s:(k + 1) * s].astype(BF16)
        cmb = cm_all[:, k * s:(k + 1) * s].astype(BF16)
        dt_c = dtc_ref[k]
        dt_r = dtr_ref[k]
        a_r = -jnp.exp(alr_ref[k])
        a_c = -jnp.exp(alc_ref[k])
        cs_c = jnp.dot(tri_c, dt_c * a_r, precision=HI, preferred_element_type=F32)
        cs_r = jnp.dot(dt_r * a_c, tri_r, precision=HI, preferred_element_type=F32)
        cs_last = cs_c[q - 1:q, :]

        tend, ecs = jnp.exp(cs_last - cs_c), jnp.exp(cs_c)
        if on_mxu:
            wide = widen(jnp.concatenate([dt_c, tend, ecs], axis=0))
            dtw, tendw, ecsw = wide[:q], wide[q:2 * q], wide[2 * q:]
            segc = expand(cs_c, e_time)
        else:
            dtw, tendw, ecsw = widen(dt_c), widen(tend), widen(ecs)
        xdt = xs * dtw
        xe = (xdt * tendw).astype(BF16)
        xdtb = xdt.astype(BF16)
        cb = _nt_dot(cmb, bmb)
        st = state[k]
        y_off = _nt_dot(cmb, st.astype(BF16)) * ecsw
        y_diag = []
        for j in range(gw // LANES):
            pair = []
            for h in (2 * j, 2 * j + 1):
                col = segc[:, h * q:(h + 1) * q] if on_mxu else cs_c[:, h:h + 1]
                seg = col - cs_r[h:h + 1, :]
                gm = (cb * jnp.exp(jnp.where(causal, seg, NEG_INF))).astype(BF16)
                pair.append(jnp.dot(gm, xdtb[:, j * LANES:(j + 1) * LANES], preferred_element_type=F32))
            y_diag.append(jnp.where(low, pair[0], pair[1]))
        y = jnp.concatenate(y_diag, axis=1) + y_off + xs * dpar_ref[:, k * gw:(k + 1) * gw]
        new_st = _tn_dot(xe, bmb)
        dec_last = jnp.exp(cs_r[:, q - 1:q])
        st = jnp.concatenate([st[h * SSM_P:(h + 1) * SSM_P, :] * dec_last[h:h + 1, :] for h in range(SSM_HPG)],
                             axis=0) + new_st
        state[k] = st
        st_ref[k] = st

        z = z_ref[:, k * gw:(k + 1) * gw]
        y = y * (z * _sigmoid(z))
        y = y * lax.rsqrt(jnp.mean(y * y, axis=-1, keepdims=True) + RMS_EPS) * nw_ref[:, k * gw:(k + 1) * gw]
        y_ref[:, k * gw:(k + 1) * gw] = y.astype(y_ref.dtype)


def ssd_core(n, l, zx, dt, conv_buf8, h0, p, out_dtype, ng):
    q = SSM_CHUNK if l % SSM_CHUNK == 0 else l
    nc = l // q
    g = SSM_G
    gw, s = ng * SSM_GW, ng * SSM_S
    xb0 = SSM_DI // gw
    bb0 = 2 * SSM_DI // s
    cb0 = bb0 + g // ng
    kb0 = SSM_DI // s
    kc0 = kb0 + g // ng
    dt4 = dt.reshape(n, l, g, SSM_HPG)
    dt_c = jnp.transpose(dt4, (0, 2, 1, 3))
    dt_r = jnp.transpose(dt4, (0, 2, 3, 1))
    alog = p["ssd_a_log"].reshape(g, SSM_HPG)
    row = lambda b, gi, c: (b * nc + c)
    in_specs = [
        pl.BlockSpec((q, gw), lambda b, gi, c: (row(b, gi, c), gi)),
        pl.BlockSpec((q, gw), lambda b, gi, c: (row(b, gi, c), xb0 + gi)),
        pl.BlockSpec((q, s), lambda b, gi, c: (row(b, gi, c), bb0 + gi)),
        pl.BlockSpec((q, s), lambda b, gi, c: (row(b, gi, c), cb0 + gi)),
        pl.BlockSpec((None, 8, gw), lambda b, gi, c: (b, 0, gi)),
        pl.BlockSpec((None, 8, s), lambda b, gi, c: (b, 0, kb0 + gi)),
        pl.BlockSpec((None, 8, s), lambda b, gi, c: (b, 0, kc0 + gi)),
        pl.BlockSpec((None, ng, q, SSM_HPG), lambda b, gi, c: (b, gi, c, 0)),
        pl.BlockSpec((None, ng, SSM_HPG, q), lambda b, gi, c: (b, gi, 0, c)),
        pl.BlockSpec((ng, 1, SSM_HPG), lambda b, gi, c: (gi, 0, 0)),
        pl.BlockSpec((ng, SSM_HPG, 1), lambda b, gi, c: (gi, 0, 0)),
        pl.BlockSpec((1, gw), lambda b, gi, c: (0, gi)),
        pl.BlockSpec((4, gw), lambda b, gi, c: (0, gi)),
        pl.BlockSpec((4, s), lambda b, gi, c: (0, kb0 + gi)),
        pl.BlockSpec((4, s), lambda b, gi, c: (0, kc0 + gi)),
        pl.BlockSpec((1, gw), lambda b, gi, c: (0, gi)),
        pl.BlockSpec((1, s), lambda b, gi, c: (0, kb0 + gi)),
        pl.BlockSpec((1, s), lambda b, gi, c: (0, kc0 + gi)),
        pl.BlockSpec((1, gw), lambda b, gi, c: (0, gi)),
    ]
    cw = p["ssd_conv_w"][0]
    cbias = p["ssd_conv_b"]
    args = [zx, zx, zx, zx, conv_buf8, conv_buf8, conv_buf8, dt_c, dt_r,
            alog.reshape(g, 1, SSM_HPG), alog.reshape(g, SSM_HPG, 1),
            jnp.repeat(p["ssd_d"].reshape(-1), SSM_P).reshape(1, SSM_DI),
            cw, cw, cw, cbias, cbias, cbias, p["ssd_norm"]]
    st_spec = pl.BlockSpec((None, ng, SSM_GW, SSM_S), lambda b, gi, c: (b, gi, 0, 0))
    if h0 is not None:
        in_specs.append(st_spec)
        args.append(h0.reshape(n, g, SSM_GW, SSM_S))
    y, st = pl.pallas_call(
        functools.partial(_ssd_kernel, q=q, ng=ng, has_state=h0 is not None),
        grid=(n, g // ng, nc),
        in_specs=in_specs,
        out_specs=[pl.BlockSpec((q, gw), lambda b, gi, c: (row(b, gi, c), gi)), st_spec],
        out_shape=[jax.ShapeDtypeStruct((n * l, SSM_DI), out_dtype),
                   jax.ShapeDtypeStruct((n, g, SSM_GW, SSM_S), F32)],
        scratch_shapes=[pltpu.VMEM((8 + q, gw), F32), pltpu.VMEM((8 + q, s), F32), pltpu.VMEM((8 + q, s), F32),
                        pltpu.VMEM((8, gw), F32), pltpu.VMEM((8, s), F32), pltpu.VMEM((8, s), F32),
                        pltpu.VMEM((ng, SSM_GW, SSM_S), F32)],
        compiler_params=_cparams("arbitrary", "arbitrary", "arbitrary"),
        name="ssd_core",
    )(*args)
    return y, st.reshape(n, SSM_HEADS, SSM_P, SSM_S)


def _lru_kernel(*refs, tl, has_state, conv_done):
    (gate_ref, xb_ref, buf_ref, cw_ref, cb_ref, wr_ref, wi_ref, br_ref, bi_ref, lam_ref) = refs[:10]
    refs = refs[10:]
    if has_state:
        h0_ref, refs = refs[0], refs[1:]
    y_ref, last_ref, xp, tail, a_sc, u_sc, hs_sc, hcar, wrb, wib = refs
    t = pl.program_id(1)

    @pl.when((pl.program_id(0) == 0) & (t == 0))
    def _():
        wrb[...] = wr_ref[...].astype(BF16)
        wib[...] = wi_ref[...].astype(BF16)

    @pl.when(t == 0)
    def _():
        if not conv_done:
            tail[...] = buf_ref[...]
        hcar[...] = h0_ref[...] if has_state else jnp.zeros(hcar.shape, F32)

    if conv_done:
        xc = xb_ref[...]
    else:
        xp[0:8, :] = tail[...]
        xp[8:8 + tl, :] = xb_ref[...]
        xc = cb_ref[...]
        for i in range(4):
            xc = xc + xp[5 + i:5 + i + tl, :] * cw_ref[i:i + 1, :]
        tail[...] = xp[tl:tl + 8, :]

    xcb = xc.astype(BF16)
    bd = xc.shape[1] // LRU_BLOCKS
    rs, is_ = [], []
    for b in range(LRU_BLOCKS):
        xblk = xcb[:, b * bd:(b + 1) * bd]
        rs.append(jnp.dot(xblk, wrb[b], preferred_element_type=F32))
        is_.append(jnp.dot(xblk, wib[b], preferred_element_type=F32))
    r = _sigmoid(jnp.concatenate(rs, axis=1) + br_ref[...])
    ig = _sigmoid(jnp.concatenate(is_, axis=1) + bi_ref[...])
    log_a = -LRU_C * r * jax.nn.softplus(-lam_ref[...])
    a = jnp.exp(log_a)
    a_sc[...] = a
    u_sc[...] = jnp.sqrt(-jnp.tanh(log_a) * (a * a + 1.0)) * (ig * xc)

    def step(i, h):
        h = a_sc[pl.ds(i, 1), :] * h + u_sc[pl.ds(i, 1), :]
        hs_sc[pl.ds(i, 1), :] = h
        return h

    h = lax.fori_loop(0, tl, step, hcar[...], unroll=8)
    hcar[...] = h
    last_ref[...] = h
    y_ref[...] = (hs_sc[...] * jax.nn.gelu(gate_ref[...])).astype(y_ref.dtype)


def lru_core(n, l, gate, xb, conv_buf8, h0, p, tl, out_dtype, conv_done):
    wd = gate.shape[1]
    nt = l // tl
    bd = wd // LRU_BLOCKS
    vec = lambda a: a.reshape(1, wd)
    cst2 = lambda b, t: (0, 0)
    in_specs = [pl.BlockSpec((tl, wd), lambda b, t: (b * nt + t, 0)),
                pl.BlockSpec((tl, wd), lambda b, t: (b * nt + t, 0)),
                pl.BlockSpec((None, 8, wd), lambda b, t: (b, 0, 0)),
                pl.BlockSpec((4, wd), cst2), pl.BlockSpec((1, wd), cst2),
                pl.BlockSpec((LRU_BLOCKS, bd, bd), lambda b, t: (0, 0, 0)),
                pl.BlockSpec((LRU_BLOCKS, bd, bd), lambda b, t: (0, 0, 0)),
                pl.BlockSpec((1, wd), cst2), pl.BlockSpec((1, wd), cst2), pl.BlockSpec((1, wd), cst2)]
    args = [gate, xb, conv_buf8, p["lru_conv_w"][0], vec(p["lru_conv_b"]), p["lru_w_r"][0], p["lru_w_i"][0],
            vec(p["lru_b_r"]), vec(p["lru_b_i"]), vec(p["lru_lam"])]
    if h0 is not None:
        in_specs.append(pl.BlockSpec((None, 1, wd), lambda b, t: (b, 0, 0)))
        args.append(h0.reshape(n, 1, wd))
    y, last = pl.pallas_call(
        functools.partial(_lru_kernel, tl=tl, has_state=h0 is not None, conv_done=conv_done),
        grid=(n, nt),
        in_specs=in_specs,
        out_specs=[pl.BlockSpec((tl, wd), lambda b, t: (b * nt + t, 0)),
                   pl.BlockSpec((None, 1, wd), lambda b, t: (b, 0, 0))],
        out_shape=[jax.ShapeDtypeStruct((n * l, wd), out_dtype), jax.ShapeDtypeStruct((n, 1, wd), F32)],
        scratch_shapes=[pltpu.VMEM((8 + tl, wd), F32), pltpu.VMEM((8, wd), F32),
                        pltpu.VMEM((tl, wd), F32), pltpu.VMEM((tl, wd), F32), pltpu.VMEM((tl, wd), F32),
                        pltpu.VMEM((1, wd), F32),
                        pltpu.VMEM((LRU_BLOCKS, bd, bd), BF16), pltpu.VMEM((LRU_BLOCKS, bd, bd), BF16)],
        compiler_params=_cparams("arbitrary", "arbitrary"),
        name="lru_core",
    )(*args)
    return y, last.reshape(n, wd)


def _pad_buf8(buf):
    return jnp.pad(buf, ((0, 0), (5, 0), (0, 0)))


def _trunk(seq, x, mod, pos, cache, p):
    n, l, m = seq.n, seq.l, seq.m
    prompt = seq.prompt
    d = x.shape[1]
    act = BF16 if prompt else F32
    new = {}
    reps = 1 if prompt else n

    h = norm_mod(seq, x, p["g_mix"], 0, mod, 0)
    rope64 = rope_tables(pos, SWA_HD, reps)
    nq = SWA_QH * SWA_HD
    q = matmul(seq, h, p["swa_w_qkv"], 0, col0=0, ncols=nq, tn=1024, out_dtype=act, epi="rope", rope=rope64,
               n_rope=nq // 1024, hd=SWA_HD, name="swa_q")
    kv = matmul(seq, h, p["swa_w_qkv"], 0, col0=nq, tn=256, out_dtype=F32, epi="rope", rope=rope64,
                n_rope=1, hd=SWA_HD, name="swa_kv")
    akw = dict(d=1, hd=SWA_HD, kvh=SWA_KVH, grp=SWA_QH // SWA_KVH, q_blk=0, k_blk=0, v_blk=1,
               sinks=p["swa_sinks"][0], want_lse=False)
    if prompt:
        o, _ = attn_prompt(n, l, q, kv, out_dtype=BF16, **akw)
        new["swa_kv"] = kv.reshape(n, l, 2, SWA_KVH, SWA_HD)[:, l - min(SWA_WINDOW, l):]
    else:
        c = cache["swa_kv"][0]
        o, _ = attn_sample(n, l, q, kv, c.reshape(n, c.shape[1], 2 * SWA_KVH, SWA_HD), **akw)
        new["swa_kv"] = kv.reshape(n, l, 2, SWA_KVH, SWA_HD)
    x = matmul(seq, o, p["swa_w_o"], 0, tn=512, out_dtype=F32, epi="resid", res=x, mod=mod, mod_layer=0, mod_k=2,
               name="swa_o")
    x = ffn(seq, norm_mod(seq, x, p["g_ffn"], 0, mod, 3), p["w_ff1"], p["w_ff2"], 0, x, mod)

    h = norm_mod(seq, x, p["g_mix"], 1, mod, 0)
    nzx = SSM_DI + SSM_DI + 2 * SSM_G * SSM_S
    w_dt = p["ssd_w_in"][:, :, nzx:]
    dt = matmul(seq, h, w_dt, 0, tn=SSM_HEADS, out_dtype=F32, bias=p["ssd_dt_bias"], epi="softplus", name="ssd_dt")
    zx = matmul(seq, h, p["ssd_w_in"], 0, col0=0, ncols=nzx, tn=1024, out_dtype=F32, name="ssd_in")
    zx3 = zx.reshape(n, l, nzx)
    if prompt:
        buf8 = jnp.zeros((n, 8, nzx - SSM_DI), F32)
        h0 = None
        new["ssd_conv"] = zx3[:, l - 3:, SSM_DI:]
    else:
        buf8 = _pad_buf8(cache["ssd_conv"][0])
        h0 = cache["ssd"][0]
        new["ssd_conv"] = jnp.concatenate([cache["ssd_conv"][0], zx3[:, :, SSM_DI:]], axis=1)[:, -3:]
    y, new["ssd"] = ssd_core(n, l, zx, dt, buf8, h0, p, act, ng=4 if prompt else SSM_G)
    x = matmul(seq, y, p["ssd_w_out"], 0, tn=512, tm=min(seq.tm, 512), out_dtype=F32, epi="resid", res=x, mod=mod,
               mod_layer=1, mod_k=2, name="ssd_out")
    x = ffn(seq, norm_mod(seq, x, p["g_ffn"], 1, mod, 3), p["w_ff1"], p["w_ff2"], 1, x, mod)

    h = norm_mod(seq, x, p["g_mix"], 2, mod, 0)
    rope128 = rope_tables(pos, DIL_HD, reps)
    ng = len(DIL_PATTERN)
    nq = ng * DIL_QH * DIL_HD
    q = matmul(seq, h, p["dil_w_qkv"], 0, col0=0, ncols=nq, tn=1024, out_dtype=F32, epi="rope", rope=rope128,
               n_rope=nq // 1024, hd=DIL_HD, name="dil_q")
    ck = DIL_KVH * DIL_HD
    grp = DIL_QH // DIL_KVH
    outs, lses = [], []
    for g, ((w, dil), key) in enumerate(zip(DIL_PATTERN, ("dil_kv_w128", "dil_kv_w512", "dil_kv_w2048"))):
        kv = matmul(seq, h, p["dil_w_qkv"], 0, col0=nq + g * ck, ncols=2 * ck, cstride=ng, tn=ck, out_dtype=F32,
                    epi="rope", rope=rope128, n_rope=1, hd=DIL_HD, name="dil_kv")
        akw = dict(d=dil, hd=DIL_HD, kvh=DIL_KVH, grp=grp, q_blk=g, k_blk=0, v_blk=1)
        kv_g = kv.reshape(n, l, 2, DIL_KVH, DIL_HD)
        if not prompt:
            c = cache[key][0]
            o, lse = attn_sample(n, l, q, kv, c.reshape(n, c.shape[1], 2 * DIL_KVH, DIL_HD), **akw)
            new[key] = kv_g
        else:
            if dil == 1:
                o, lse = attn_prompt(n, l, q, kv, **akw)
            else:
                o, lse = attn_dil_prompt(n, l, q, kv, d=dil, kvh=DIL_KVH, grp=grp, q_head0=g * DIL_QH,
                                         k_head0=0, v_head0=DIL_KVH)
            new[key] = kv_g[:, l - min(w, l):]
        outs.append(o)
        lses.append(lse)
    o = dil_merge(outs, lses, seq.tm if not prompt else 512, act)
    x = matmul(seq, o, p["dil_w_o"], 0, tn=512, out_dtype=F32, epi="resid", res=x, mod=mod, mod_layer=2, mod_k=2,
               name="dil_o")
    x = ffn(seq, norm_mod(seq, x, p["g_ffn"], 2, mod, 3), p["w_ff1"], p["w_ff2"], 2, x, mod)

    h = norm_mod(seq, x, p["g_mix"], 3, mod, 0)
    wd = p["lru_w_in"].shape[2] // 2
    if prompt:
        gate = matmul(seq, h, p["lru_w_in"], 0, col0=0, ncols=wd, tn=1024, out_dtype=F32, bias=p["lru_b_in"],
                      name="lru_in_gate")
        xb, tails = matmul_conv(seq, h, p["lru_w_in"], 0, col0=wd, ncols=wd, tn=1024, bias=p["lru_b_in"],
                                conv_w=p["lru_conv_w"][0], conv_b=p["lru_conv_b"].reshape(1, wd), conv_col0=0,
                                silu=False, name="lru_in_xb")
        buf8 = tails
        h0 = None
        new["lru_conv"] = tails[:, 5:, :]
    else:
        gx = matmul(seq, h, p["lru_w_in"], 0, tn=1024, out_dtype=F32, bias=p["lru_b_in"], name="lru_in")
        gate, xb = gx[:, :wd], gx[:, wd:]
        buf8 = _pad_buf8(cache["lru_conv"][0])
        h0 = cache["lru"][0]
        new["lru_conv"] = jnp.concatenate([cache["lru_conv"][0], xb.reshape(n, l, wd)], axis=1)[:, -3:]
    y, new["lru"] = lru_core(n, l, gate, xb, buf8, h0, p, 256 if prompt else l, act, conv_done=prompt)
    x = matmul(seq, y, p["lru_w_out"], 0, tn=512, out_dtype=F32, epi="resid", res=x, mod=mod, mod_layer=3, mod_k=2,
               name="lru_out")
    x = ffn(seq, norm_mod(seq, x, p["g_ffn"], 3, mod, 3), p["w_ff1"], p["w_ff2"], 3, x, mod)

    y = final_norm(x, p["g_final"], min(seq.tm, 512))
    return y.reshape(n, l, d), {k: v[None] for k, v in new.items()}


def kernel(x_prompt, x_sample, cache_swa_kv, state_ssd_conv, state_ssd, cache_dil_kv_w128, cache_dil_kv_w512,
           cache_dil_kv_w2048, state_lru_conv, state_lru, c_prompt, c_sample, w_ada, b_ada, g_mix, g_ffn,
           w_ff1, w_ff2, g_final, swa_w_qkv, swa_sinks, swa_w_o, ssd_w_in, ssd_conv_w, ssd_conv_b,
           ssd_dt_bias, ssd_a_log, ssd_d, ssd_norm, ssd_w_out, dil_w_qkv, dil_w_o, lru_w_in, lru_b_in,
           lru_conv_w, lru_conv_b, lru_w_r, lru_b_r, lru_w_i, lru_b_i, lru_lam, lru_w_out):
    p = dict(g_mix=g_mix, g_ffn=g_ffn, w_ff1=w_ff1.astype(BF16), w_ff2=w_ff2.astype(BF16), g_final=g_final,
             swa_w_qkv=swa_w_qkv, swa_sinks=swa_sinks, swa_w_o=swa_w_o,
             ssd_w_in=ssd_w_in, ssd_conv_w=ssd_conv_w, ssd_conv_b=ssd_conv_b, ssd_dt_bias=ssd_dt_bias,
             ssd_a_log=ssd_a_log, ssd_d=ssd_d, ssd_norm=ssd_norm, ssd_w_out=ssd_w_out,
             dil_w_qkv=dil_w_qkv, dil_w_o=dil_w_o,
             lru_w_in=lru_w_in, lru_b_in=lru_b_in, lru_conv_w=lru_conv_w, lru_conv_b=lru_conv_b,
             lru_w_r=lru_w_r, lru_b_r=lru_b_r, lru_w_i=lru_w_i, lru_b_i=lru_b_i, lru_lam=lru_lam,
             lru_w_out=lru_w_out)
    cache = dict(swa_kv=cache_swa_kv, ssd_conv=state_ssd_conv, ssd=state_ssd, dil_kv_w128=cache_dil_kv_w128,
                 dil_kv_w512=cache_dil_kv_w512, dil_kv_w2048=cache_dil_kv_w2048, lru_conv=state_lru_conv,
                 lru=state_lru)
    nb, l, d = x_prompt.shape
    ns, ls, _ = x_sample.shape
    depth = w_ada.shape[0]
    rows = -(-(nb + ns) // 16) * 16
    c_all = jnp.concatenate([c_prompt, c_sample, jnp.zeros((rows - nb - ns, d), F32)], axis=0)
    mod = ada_mod(c_all, w_ada, b_ada)
    mod_p = mod.reshape(depth, rows, 1, 6 * d)
    mod_s = jnp.repeat(mod[:, nb:nb + ns], ls, axis=1).reshape(depth, 1, ns * ls, 6 * d)

    seq_p = Seq(nb, l, 1024, True)
    seq_s = Seq(ns, ls, ns * ls, False)
    y_p, sp = _trunk(seq_p, x_prompt.reshape(nb * l, d), mod_p, jnp.arange(l, dtype=jnp.int32), None, p)
    y_s, ss = _trunk(seq_s, x_sample.reshape(ns * ls, d), mod_s, PAST_LEN + jnp.arange(ls, dtype=jnp.int32), cache, p)
    return (y_p, y_s,
            sp["swa_kv"], ss["swa_kv"],
            sp["ssd_conv"], ss["ssd_conv"],
            sp["ssd"], ss["ssd"],
            sp["dil_kv_w128"], ss["dil_kv_w128"],
            sp["dil_kv_w512"], ss["dil_kv_w512"],
            sp["dil_kv_w2048"], ss["dil_kv_w2048"],
            sp["lru_conv"], ss["lru_conv"],
            sp["lru"], ss["lru"])
```

```python
import functools
import math

import jax
import jax.numpy as jnp
from jax import lax
from jax.experimental import pallas as pl
from jax.experimental.pallas import tpu as pltpu

F32 = jnp.float32
BF16 = jnp.bfloat16
HI = lax.Precision.HIGHEST

RMS_EPS = 1e-6
ROPE_THETA = 10000.0
NEG_INF = -1e30
LANES = 128
MXU_COLS = 256
VMEM_LIMIT = 56 * 1024 * 1024

PAST_LEN = 16384
ATTN_BLOCK = 128
SWA_WINDOW, SWA_HD, SWA_QH, SWA_KVH = 128, 64, 32, 4
DIL_PATTERN = ((128, 1), (512, 4), (2048, 16))
DIL_HD, DIL_QH, DIL_KVH = 128, 16, 4
SSM_HEADS, SSM_P, SSM_S, SSM_G, SSM_HPG, SSM_CHUNK = 64, 64, 128, 8, 8, 128
SSM_DI = SSM_HEADS * SSM_P
SSM_GW = SSM_HPG * SSM_P
LRU_BLOCKS, LRU_C = 8, 8.0


def _cparams(*sem):
    return pltpu.CompilerParams(dimension_semantics=sem, vmem_limit_bytes=VMEM_LIMIT)


def _nt_dot(a, b):
    return lax.dot_general(a, b, (((1,), (1,)), ((), ())), preferred_element_type=F32)


def _sigmoid(x):
    return 0.5 * jnp.tanh(0.5 * x) + 0.5


def _tn_dot(a, b):
    return lax.dot_general(a, b, (((0,), (0,)), ((), ())), preferred_element_type=F32)


class Seq:
    def __init__(self, n, l, tm, prompt):
        self.n, self.l, self.m, self.tm, self.prompt = n, l, n * l, tm, prompt

    def mod_spec(self, layer, k, d, row_axis):
        tm, l = self.tm, self.l
        if self.prompt:
            return pl.BlockSpec((None, None, 1, d), lambda *g: (layer, (g[row_axis] * tm) // l, 0, k))
        return pl.BlockSpec((None, None, tm, d), lambda *g: (layer, 0, g[row_axis], k))


def _ada_kernel(c_ref, w_ref, b_ref, o_ref):
    c = c_ref[...]
    cond = (c * jax.nn.sigmoid(c)).astype(BF16)
    o_ref[...] = jnp.dot(cond, w_ref[...].astype(BF16), preferred_element_type=F32) + b_ref[...]


def ada_mod(c_all, w_ada, b_ada, tn=1024):
    depth, d, n6 = w_ada.shape
    r = c_all.shape[0]
    return pl.pallas_call(
        _ada_kernel,
        grid=(depth, n6 // tn),
        in_specs=[pl.BlockSpec((r, d), lambda a, j: (0, 0)),
                  pl.BlockSpec((None, d, tn), lambda a, j: (a, 0, j)),
                  pl.BlockSpec((None, 1, tn), lambda a, j: (a, 0, j))],
        out_specs=pl.BlockSpec((None, r, tn), lambda a, j: (a, 0, j)),
        out_shape=jax.ShapeDtypeStruct((depth, r, n6), F32),
        compiler_params=_cparams("arbitrary", "arbitrary"),
        name="ada_mod",
    )(c_all, w_ada, b_ada.reshape(depth, 1, n6))


def _norm_mod_kernel(x_ref, g_ref, sh_ref, sc_ref, o_ref):
    x = x_ref[...]
    y = x * lax.rsqrt(jnp.mean(x * x, axis=-1, keepdims=True) + RMS_EPS) * g_ref[...]
    o_ref[...] = (y * (1.0 + sc_ref[...]) + sh_ref[...]).astype(o_ref.dtype)


def _norm_kernel(x_ref, g_ref, o_ref):
    x = x_ref[...]
    o_ref[...] = x * lax.rsqrt(jnp.mean(x * x, axis=-1, keepdims=True) + RMS_EPS) * g_ref[...]


def norm_mod(seq, x, g, layer, mod, k_shift):
    m, d = x.shape
    tm = seq.tm
    sub = Seq(seq.n, seq.l, tm, seq.prompt)
    return pl.pallas_call(
        _norm_mod_kernel,
        grid=(m // tm,),
        in_specs=[pl.BlockSpec((tm, d), lambda i: (i, 0)),
                  pl.BlockSpec((None, 1, d), lambda i: (layer, 0, 0)),
                  sub.mod_spec(layer, k_shift, d, 0),
                  sub.mod_spec(layer, k_shift + 1, d, 0)],
        out_specs=pl.BlockSpec((tm, d), lambda i: (i, 0)),
        out_shape=jax.ShapeDtypeStruct((m, d), BF16),
        compiler_params=_cparams("arbitrary"),
        name="norm_mod",
    )(x, g.reshape(g.shape[0], 1, d), mod, mod)


def final_norm(x, g, tm):
    m, d = x.shape
    return pl.pallas_call(
        _norm_kernel,
        grid=(m // tm,),
        in_specs=[pl.BlockSpec((tm, d), lambda i: (i, 0)), pl.BlockSpec((1, d), lambda i: (0, 0))],
        out_specs=pl.BlockSpec((tm, d), lambda i: (i, 0)),
        out_shape=jax.ShapeDtypeStruct((m, d), F32),
        compiler_params=_cparams("arbitrary"),
        name="final_norm",
    )(x, g.reshape(1, d))


def _rope_cols(y, cos, sin, hd):
    outs = []
    for c in range(y.shape[1] // LANES):
        yc = y[:, c * LANES:(c + 1) * LANES]
        if hd == LANES:
            partner = pltpu.roll(yc, LANES // 2, axis=1)
        else:
            lane = lax.broadcasted_iota(jnp.int32, yc.shape, 1)
            partner = jnp.where(lane % hd < hd // 2, pltpu.roll(yc, LANES - hd // 2, axis=1),
                                pltpu.roll(yc, hd // 2, axis=1))
        outs.append(yc * cos + partner * sin)
    return outs[0] if len(outs) == 1 else jnp.concatenate(outs, axis=1)


def _mm_kernel(*refs, epi, n_rope, all_rope, hd, has_bias):
    x_ref, w_ref = refs[0], refs[1]
    o_ref, wb_ref = refs[-2], refs[-1]
    extra = refs[2:-2]

    @pl.when(pl.program_id(1) == 0)
    def _():
        wb_ref[...] = w_ref[...].astype(BF16)

    if has_bias:
        bias_ref, extra = extra[0], extra[1:]
    xb = x_ref[...].astype(BF16)
    tn = o_ref.shape[1]
    sub = min(tn, MXU_COLS)
    for c in range(tn // sub):
        cs = slice(c * sub, (c + 1) * sub)
        y = jnp.dot(xb, wb_ref[:, cs], preferred_element_type=F32)
        if has_bias:
            y = y + bias_ref[:, cs]
        if epi == "none":
            out = y
        elif epi == "softplus":
            out = jax.nn.softplus(y)
        elif epi == "resid":
            res_ref, gate_ref = extra
            out = res_ref[:, cs] + gate_ref[:, cs] * y
        elif epi == "rope":
            cos_ref, sin_ref = extra
            out = _rope_cols(y, cos_ref[...], sin_ref[...], hd)
            if not all_rope:
                out = jnp.where(pl.program_id(0) < n_rope, out, y)
        else:
            raise ValueError(epi)
        o_ref[:, cs] = out.astype(o_ref.dtype)


def matmul(seq, x, w, layer, *, col0=0, ncols=None, tn, out_dtype, tm=None, bias=None, epi="none",
           rope=None, n_rope=0, hd=LANES, res=None, mod=None, mod_layer=0, mod_k=0, cstride=1, name="mm"):
    m, k = x.shape
    ntot = w.shape[2]
    ncols = ntot - col0 if ncols is None else ncols
    tm = seq.tm if tm is None else tm
    cb = col0 // tn
    assert col0 % tn == 0 and ncols % tn == 0 and m % tm == 0
    in_specs = [pl.BlockSpec((tm, k), lambda j, i: (i, 0)),
                pl.BlockSpec((None, k, tn), lambda j, i: (layer, 0, cb + j * cstride))]
    args = [x, w]
    if bias is not None:
        in_specs.append(pl.BlockSpec((None, 1, tn), lambda j, i: (0, 0, cb + j * cstride)))
        args.append(bias.reshape(1, 1, -1))
    if epi == "rope":
        cos, sin = rope
        nrt = cos.shape[0] // tm
        for t in (cos, sin):
            in_specs.append(pl.BlockSpec((tm, LANES), lambda j, i: (i % nrt, 0)))
            args.append(t)
    if epi == "resid":
        sub = Seq(seq.n, seq.l, tm, seq.prompt)
        base = sub.mod_spec(mod_layer, mod_k, tn, 1)
        nk = w.shape[2] // tn
        gate_spec = pl.BlockSpec(base.block_shape,
                                 lambda j, i, f=base.index_map: f(j, i)[:3] + (f(j, i)[3] * nk + j,))
        in_specs += [pl.BlockSpec((tm, tn), lambda j, i: (i, j)), gate_spec]
        args += [res, mod]
    return pl.pallas_call(
        functools.partial(_mm_kernel, epi=epi, n_rope=n_rope, all_rope=n_rope >= ncols // tn, hd=hd,
                          has_bias=bias is not None),
        grid=(ncols // tn, m // tm),
        in_specs=in_specs,
        out_specs=pl.BlockSpec((tm, tn), lambda j, i: (i, j)),
        out_shape=jax.ShapeDtypeStruct((m, ncols), out_dtype),
        scratch_shapes=[pltpu.VMEM((k, tn), BF16)],
        compiler_params=_cparams("arbitrary", "arbitrary"),
        name=name,
    )(*args)


def _mm_conv_kernel(*refs, tiles_per_seq, silu, has_bias):
    x_ref, w_ref = refs[0], refs[1]
    k = 3 if has_bias else 2
    cw_ref, cb_ref = refs[k], refs[k + 1]
    o_ref, tails_ref, wb_ref, xp_ref, tail_ref = refs[k + 2:]
    i = pl.program_id(1)

    @pl.when(i == 0)
    def _():
        wb_ref[...] = w_ref[...].astype(BF16)

    @pl.when(i % tiles_per_seq == 0)
    def _():
        tail_ref[...] = jnp.zeros(tail_ref.shape, F32)

    xb = x_ref[...].astype(BF16)
    tm, tn = o_ref.shape
    sub = min(tn, MXU_COLS)
    for c in range(tn // sub):
        cs = slice(c * sub, (c + 1) * sub)
        y = jnp.dot(xb, wb_ref[:, cs], preferred_element_type=F32)
        if has_bias:
            y = y + refs[2][:, cs]
        xp_ref[c, 0:8, :] = tail_ref[:, cs]
        xp_ref[c, 8:8 + tm, :] = y
        acc = cb_ref[:, cs]
        for t in range(4):
            acc = acc + xp_ref[c, 5 + t:5 + t + tm, :] * cw_ref[t:t + 1, cs]
        last = y[tm - 8:tm, :]
        tail_ref[:, cs] = last
        tails_ref[:, cs] = last
        o_ref[:, cs] = (acc * _sigmoid(acc) if silu else acc).astype(o_ref.dtype)


def matmul_conv(seq, x, w, layer, *, col0, ncols, tn, conv_w, conv_b, conv_col0, silu, bias=None, name="mm_conv"):
    m, k = x.shape
    tm = seq.tm
    assert seq.prompt and seq.l % tm == 0 and col0 % tn == 0 and ncols % tn == 0 and conv_col0 % tn == 0
    cb, ccb, tps = col0 // tn, conv_col0 // tn, seq.l // tm
    sub = min(tn, MXU_COLS)
    in_specs = [pl.BlockSpec((tm, k), lambda j, i: (i, 0)),
                pl.BlockSpec((None, k, tn), lambda j, i: (layer, 0, cb + j))]
    args = [x, w]
    if bias is not None:
        in_specs.append(pl.BlockSpec((None, 1, tn), lambda j, i: (0, 0, cb + j)))
        args.append(bias.reshape(1, 1, -1))
    in_specs += [pl.BlockSpec((4, tn), lambda j, i: (0, ccb + j)), pl.BlockSpec((1, tn), lambda j, i: (0, ccb + j))]
    args += [conv_w, conv_b]
    return pl.pallas_call(
        functools.partial(_mm_conv_kernel, tiles_per_seq=tps, silu=silu, has_bias=bias is not None),
        grid=(ncols // tn, m // tm),
        in_specs=in_specs,
        out_specs=[pl.BlockSpec((tm, tn), lambda j, i: (i, j)),
                   pl.BlockSpec((None, 8, tn), lambda j, i: (i // tps, 0, j))],
        out_shape=[jax.ShapeDtypeStruct((m, ncols), F32), jax.ShapeDtypeStruct((seq.n, 8, ncols), F32)],
        scratch_shapes=[pltpu.VMEM((k, tn), BF16), pltpu.VMEM((tn // sub, 8 + tm, sub), F32),
                        pltpu.VMEM((8, tn), F32)],
        compiler_params=_cparams("arbitrary", "arbitrary"),
        name=name,
    )(*args)


def _ffn_kernel(x_ref, w1_ref, w2_ref, res_ref, gate_ref, o_ref, h_ref, *, na, tf):
    s = pl.program_id(1)

    @pl.when(s < na)
    def _():
        h = jnp.dot(x_ref[...], w1_ref[...], preferred_element_type=F32)
        col = pl.multiple_of(s * tf, tf)
        h_ref[:, pl.ds(col, tf)] = jnp.square(jnp.maximum(h, 0.0)).astype(BF16)

    @pl.when(s >= na)
    def _():
        y = jnp.dot(h_ref[...], w2_ref[...], preferred_element_type=F32)
        o_ref[...] = res_ref[...] + gate_ref[...] * y


def ffn(seq, h, w1, w2, layer, res, mod, tf=1024, tn=256):
    m, d = h.shape
    f = w1.shape[2]
    tm = seq.tm
    na, nb = f // tf, d // tn
    sub = Seq(seq.n, seq.l, tm, seq.prompt)
    base = sub.mod_spec(layer, 5, tn, 0)
    ocol = lambda s: jnp.maximum(s - na, 0)
    gate_spec = pl.BlockSpec(base.block_shape,
                             lambda i, s, fm=base.index_map: fm(i, s)[:3] + (5 * nb + ocol(s),))
    return pl.pallas_call(
        functools.partial(_ffn_kernel, na=na, tf=tf),
        grid=(m // tm, na + nb),
        in_specs=[pl.BlockSpec((tm, d), lambda i, s: (i, 0)),
                  pl.BlockSpec((None, d, tf), lambda i, s: (layer, 0, jnp.minimum(s, na - 1))),
                  pl.BlockSpec((None, f, tn), lambda i, s: (layer, 0, ocol(s))),
                  pl.BlockSpec((tm, tn), lambda i, s: (i, ocol(s))),
                  gate_spec],
        out_specs=pl.BlockSpec((tm, tn), lambda i, s: (i, ocol(s))),
        out_shape=jax.ShapeDtypeStruct((m, d), F32),
        scratch_shapes=[pltpu.VMEM((tm, f), BF16)],
        compiler_params=_cparams("arbitrary", "arbitrary"),
        name="ffn",
    )(h, w1, w2, res, mod)


def rope_tables(pos, hd, reps):
    half = hd // 2
    inv = ROPE_THETA ** (-jnp.arange(half, dtype=F32) / half)
    ang = pos.astype(F32)[:, None] * inv[None, :]
    cos = jnp.concatenate([jnp.cos(ang), jnp.cos(ang)], axis=-1)
    sin = jnp.concatenate([-jnp.sin(ang), jnp.sin(ang)], axis=-1)
    lane_reps = LANES // hd
    return jnp.tile(cos, (reps, lane_reps)), jnp.tile(sin, (reps, lane_reps))


def _attn_prompt_kernel(*refs, hd, kvh, grp, scale, has_sinks, want_lse):
    if has_sinks:
        sink_ref, refs = refs[0], refs[1:]
    q_ref, kp_ref, kc_ref, vp_ref, vc_ref = refs[:5]
    o_ref = refs[5]
    lse_ref = refs[6] if want_lse else None
    ub = pl.program_id(2)
    bq = q_ref.shape[0]
    rows = grp * bq
    iq = lax.broadcasted_iota(jnp.int32, (rows, 2 * bq), 0) % bq
    jk = lax.broadcasted_iota(jnp.int32, (rows, 2 * bq), 1)
    mask = (jk >= iq) & (jk <= iq + bq) & ((jk >= bq) | (ub > 0))
    rcol = lax.broadcasted_iota(jnp.int32, (rows, 1), 0)
    lane = lax.broadcasted_iota(jnp.int32, (bq, LANES), 1)
    lse_tile = jnp.zeros((bq, LANES), F32)
    for kh in range(kvh):
        ksl = slice(kh * hd, (kh + 1) * hd)
        kk = jnp.concatenate([kp_ref[:, ksl], kc_ref[:, ksl]], axis=0).astype(BF16)
        vv = jnp.concatenate([vp_ref[:, ksl], vc_ref[:, ksl]], axis=0).astype(BF16)
        qs = jnp.concatenate([q_ref[:, (kh * grp + g) * hd:(kh * grp + g + 1) * hd] for g in range(grp)], axis=0)
        s = jnp.where(mask, _nt_dot(qs.astype(BF16), kk) * scale, NEG_INF)
        mx = jnp.max(s, axis=-1, keepdims=True)
        if has_sinks:
            sk = jnp.zeros((rows, 1), F32)
            for g in range(grp):
                sk = jnp.where(rcol // bq == g, sink_ref[kh * grp + g], sk)
            mx = jnp.maximum(mx, sk)
        p = jnp.exp(s - mx)
        den = jnp.sum(p, axis=-1, keepdims=True)
        if has_sinks:
            den = den + jnp.exp(sk - mx)
        o = jnp.dot(p.astype(BF16), vv, preferred_element_type=F32) / den
        lse = mx + jnp.log(den)
        for g in range(grp):
            h = kh * grp + g
            o_ref[:, h * hd:(h + 1) * hd] = o[g * bq:(g + 1) * bq, :].astype(o_ref.dtype)
            if want_lse:
                lse_tile = jnp.where(lane == h, lse[g * bq:(g + 1) * bq, :], lse_tile)
    if want_lse:
        lse_ref[...] = lse_tile


def attn_prompt(n, l, q, kv, *, d, hd, kvh, grp, q_blk, k_blk, v_blk, sinks=None, want_lse=True, out_dtype=F32):
    bq = ATTN_BLOCK
    cq, ck = grp * kvh * hd, kvh * hd
    nq, nk = q.shape[1] // cq, kv.shape[1] // ck
    lu = l // d
    qv = q.reshape(n, lu, d * q.shape[1])
    kvv = kv.reshape(n, lu, d * kv.shape[1])
    prev = lambda u: jnp.maximum(u - 1, 0)
    in_specs = [pl.BlockSpec((None, bq, cq), lambda b, r, u: (b, u, r * nq + q_blk)),
                pl.BlockSpec((None, bq, ck), lambda b, r, u: (b, prev(u), r * nk + k_blk)),
                pl.BlockSpec((None, bq, ck), lambda b, r, u: (b, u, r * nk + k_blk)),
                pl.BlockSpec((None, bq, ck), lambda b, r, u: (b, prev(u), r * nk + v_blk)),
                pl.BlockSpec((None, bq, ck), lambda b, r, u: (b, u, r * nk + v_blk))]
    args = [qv, kvv, kvv, kvv, kvv]
    if sinks is not None:
        in_specs.insert(0, pl.BlockSpec(memory_space=pltpu.SMEM))
        args.insert(0, sinks)
    out_shape = [jax.ShapeDtypeStruct((n, lu, d * cq), out_dtype)]
    out_specs = [pl.BlockSpec((None, bq, cq), lambda b, r, u: (b, u, r))]
    if want_lse:
        out_shape.append(jax.ShapeDtypeStruct((n, lu, d * LANES), F32))
        out_specs.append(pl.BlockSpec((None, bq, LANES), lambda b, r, u: (b, u, r)))
    outs = pl.pallas_call(
        functools.partial(_attn_prompt_kernel, hd=hd, kvh=kvh, grp=grp, scale=hd ** -0.5,
                          has_sinks=sinks is not None, want_lse=want_lse),
        grid=(n, d, lu // bq),
        in_specs=in_specs,
        out_specs=out_specs,
        out_shape=out_shape,
        compiler_params=_cparams("arbitrary", "arbitrary", "arbitrary"),
        name=f"attn_prompt_d{d}",
    )(*args)
    o = outs[0].reshape(n * l, cq)
    return (o, outs[1].reshape(n * l, LANES)) if want_lse else (o, None)


def _attn_dil_kernel(*refs, d, grp, scale, has_prev):
    q_refs = refs[:grp]
    kp_ref, kc_ref, vp_ref, vc_ref = refs[grp:grp + 4]
    o_refs = refs[grp + 4:2 * grp + 4]
    lse_ref = refs[2 * grp + 4]
    ub, kh = pl.program_id(1), pl.program_id(2)
    bq = ATTN_BLOCK
    nk = 2 * bq if has_prev else bq
    iq = lax.broadcasted_iota(jnp.int32, (grp * bq, nk), 0) % bq
    jk = lax.broadcasted_iota(jnp.int32, (grp * bq, nk), 1)
    if has_prev:
        mask = (jk >= iq) & (jk <= iq + bq) & ((jk >= bq) | (ub > 0))
    else:
        mask = jk <= iq
    lane = lax.broadcasted_iota(jnp.int32, (bq, LANES), 1)

    @pl.when(kh == 0)
    def _():
        lse_ref[...] = jnp.zeros(lse_ref.shape, F32)

    for r in range(d):
        rows = pl.ds(r, bq, stride=d)
        qs = jnp.concatenate([qr[rows, :] for qr in q_refs], axis=0).astype(BF16)
        if has_prev:
            kk = jnp.concatenate([kp_ref[rows, :], kc_ref[rows, :]], axis=0).astype(BF16)
            vv = jnp.concatenate([vp_ref[rows, :], vc_ref[rows, :]], axis=0).astype(BF16)
        else:
            kk = kc_ref[rows, :].astype(BF16)
            vv = vc_ref[rows, :].astype(BF16)
        s = jnp.where(mask, _nt_dot(qs, kk) * scale, NEG_INF)
        mx = jnp.max(s, axis=-1, keepdims=True)
        p = jnp.exp(s - mx)
        den = jnp.sum(p, axis=-1, keepdims=True)
        o = jnp.dot(p.astype(BF16), vv, preferred_element_type=F32) / den
        lse = mx + jnp.log(den)
        tile = lse_ref[r * bq:(r + 1) * bq, :]
        for g in range(grp):
            o_refs[g][rows, :] = o[g * bq:(g + 1) * bq, :]
            tile = jnp.where(lane == kh * grp + g, lse[g * bq:(g + 1) * bq, :], tile)
        lse_ref[r * bq:(r + 1) * bq, :] = tile


def attn_dil_prompt(n, l, q, kv, *, d, kvh, grp, q_head0, k_head0, v_head0):
    hd = LANES
    rt = d * ATTN_BLOCK
    nub = l // rt
    has_prev = nub > 1
    prev = lambda u: jnp.maximum(u - 1, 0)
    cur = lambda u: u
    blk = lambda col, rowf: pl.BlockSpec((rt, hd), lambda b, u, kh: (b * nub + rowf(u), col(kh)))
    q_specs = [blk(lambda kh, g=g: q_head0 + kh * grp + g, cur) for g in range(grp)]
    outs = pl.pallas_call(
        functools.partial(_attn_dil_kernel, d=d, grp=grp, scale=hd ** -0.5, has_prev=has_prev),
        grid=(n, nub, kvh),
        in_specs=q_specs + [blk(lambda kh: k_head0 + kh, prev), blk(lambda kh: k_head0 + kh, cur),
                            blk(lambda kh: v_head0 + kh, prev), blk(lambda kh: v_head0 + kh, cur)],
        out_specs=[blk(lambda kh: kh, cur)] * grp + [blk(lambda kh: 0, cur)],
        out_shape=[jax.ShapeDtypeStruct((n * l, kvh * hd), F32)] * grp + [jax.ShapeDtypeStruct((n * l, LANES), F32)],
        compiler_params=_cparams("arbitrary", "arbitrary", "arbitrary"),
        name=f"attn_dil_d{d}",
    )(*([q] * grp), kv, kv, kv, kv)
    lse = outs[grp].reshape(n * nub, d, ATTN_BLOCK, LANES).transpose(0, 2, 1, 3).reshape(n * l, LANES)
    return list(outs[:grp]), lse


def _attn_sample_kernel(*refs, hd, kvh, grp, scale, d, has_sinks, want_lse):
    if has_sinks:
        sink_ref, refs = refs[0], refs[1:]
    q_ref, kn_ref, vn_ref, c_ref = refs[:4]
    o_ref = refs[4]
    lse_ref = refs[5] if want_lse else None
    nb = c_ref.shape[0]
    lq = q_ref.shape[0] // nb
    flat = len(c_ref.shape) == 3
    w = c_ref.shape[1] // (2 * kvh) if flat else c_ref.shape[1]

    def cache_head(bi, idx):
        return c_ref[bi, pl.ds(idx, w, stride=2 * kvh), :] if flat else c_ref[bi, :, idx, :]

    rows = grp * lq
    row = lax.broadcasted_iota(jnp.int32, (rows, w + lq), 0)
    jk = lax.broadcasted_iota(jnp.int32, (rows, w + lq), 1)
    dist = row % lq + w - jk
    mask = (dist >= 0) & (dist <= w) & (dist % d == 0)
    rcol = lax.broadcasted_iota(jnp.int32, (rows, 1), 0)
    lane = lax.broadcasted_iota(jnp.int32, (lq, LANES), 1)
    for bi in range(nb):
        rsl = slice(bi * lq, (bi + 1) * lq)
        lse_tile = jnp.zeros((lq, LANES), F32)
        for kh in range(kvh):
            ksl = slice(kh * hd, (kh + 1) * hd)
            kk = jnp.concatenate([cache_head(bi, kh), kn_ref[rsl, ksl]], axis=0).astype(BF16)
            vv = jnp.concatenate([cache_head(bi, kvh + kh), vn_ref[rsl, ksl]], axis=0).astype(BF16)
            qs = jnp.concatenate([q_ref[rsl, (kh * grp + g) * hd:(kh * grp + g + 1) * hd] for g in range(grp)],
                                 axis=0)
            s = jnp.where(mask, _nt_dot(qs.astype(BF16), kk) * scale, NEG_INF)
            mx = jnp.max(s, axis=-1, keepdims=True)
            if has_sinks:
                sk = jnp.zeros((rows, 1), F32)
                for g in range(grp):
                    sk = jnp.where(rcol // lq == g, sink_ref[kh * grp + g], sk)
                mx = jnp.maximum(mx, sk)
            p = jnp.exp(s - mx)
            den = jnp.sum(p, axis=-1, keepdims=True)
            if has_sinks:
                den = den + jnp.exp(sk - mx)
            o = jnp.dot(p.astype(BF16), vv, preferred_element_type=F32) / den
            lse = mx + jnp.log(den)
            for g in range(grp):
                h = kh * grp + g
                o_ref[rsl, h * hd:(h + 1) * hd] = o[g * lq:(g + 1) * lq, :]
                if want_lse:
                    lse_tile = jnp.where(lane == h, lse[g * lq:(g + 1) * lq, :], lse_tile)
        if want_lse:
            lse_ref[rsl, :] = lse_tile


def attn_sample(n, l, q, kv, cache, *, d, hd, kvh, grp, q_blk, k_blk, v_blk, sinks=None, want_lse=True):
    cq, ck = grp * kvh * hd, kvh * hd
    w = cache.shape[1]
    nb = 4 if (w <= 512 and n % 4 == 0) else 1
    if hd == LANES:
        cache = cache.reshape(n, w * 2 * kvh, hd)
        cache_spec = pl.BlockSpec((nb, w * 2 * kvh, hd), lambda b: (b, 0, 0))
    else:
        cache_spec = pl.BlockSpec((nb, w, 2 * kvh, hd), lambda b: (b, 0, 0, 0))
    l, n = nb * l, n // nb
    in_specs = [pl.BlockSpec((l, cq), lambda b: (b, q_blk)),
                pl.BlockSpec((l, ck), lambda b: (b, k_blk)),
                pl.BlockSpec((l, ck), lambda b: (b, v_blk)),
                cache_spec]
    args = [q, kv, kv, cache]
    if sinks is not None:
        in_specs.insert(0, pl.BlockSpec(memory_space=pltpu.SMEM))
        args.insert(0, sinks)
    out_shape = [jax.ShapeDtypeStruct((n * l, cq), F32)]
    out_specs = [pl.BlockSpec((l, cq), lambda b: (b, 0))]
    if want_lse:
        out_shape.append(jax.ShapeDtypeStruct((n * l, LANES), F32))
        out_specs.append(pl.BlockSpec((l, LANES), lambda b: (b, 0)))
    outs = pl.pallas_call(
        functools.partial(_attn_sample_kernel, hd=hd, kvh=kvh, grp=grp, scale=hd ** -0.5, d=d,
                          has_sinks=sinks is not None, want_lse=want_lse),
        grid=(n,),
        in_specs=in_specs,
        out_specs=out_specs,
        out_shape=out_shape,
        compiler_params=_cparams("arbitrary"),
        name=f"attn_sample_d{d}",
    )(*args)
    return (outs[0], outs[1]) if want_lse else (outs[0], None)


def _dil_merge_kernel(*refs, counts, qh, hd):
    no = sum(counts)
    o_refs, l_refs, out_ref = refs[:no], refs[no:no + len(counts)], refs[-1]
    ls = [r[...] for r in l_refs]
    mx = functools.reduce(jnp.maximum, ls)
    es = [jnp.exp(v - mx) for v in ls]
    tot = functools.reduce(lambda a, b: a + b, es)
    wts = [e / tot for e in es]
    for h in range(qh):
        acc, first = None, 0
        for cnt, wt in zip(counts, wts):
            if cnt == 1:
                piece = o_refs[first][:, h * hd:(h + 1) * hd]
            else:
                piece = o_refs[first + h % cnt][:, (h // cnt) * hd:(h // cnt + 1) * hd]
            first += cnt
            term = wt[:, h:h + 1] * piece
            acc = term if acc is None else acc + term
        out_ref[:, h * hd:(h + 1) * hd] = acc.astype(out_ref.dtype)


def dil_merge(outs, lses, tm, out_dtype):
    groups = [o if isinstance(o, (list, tuple)) else [o] for o in outs]
    flat = [a for grp_arrays in groups for a in grp_arrays]
    m = flat[0].shape[0]
    c = DIL_QH * DIL_HD
    return pl.pallas_call(
        functools.partial(_dil_merge_kernel, counts=tuple(len(g) for g in groups), qh=DIL_QH, hd=DIL_HD),
        grid=(m // tm,),
        in_specs=[pl.BlockSpec((tm, a.shape[1]), lambda i: (i, 0)) for a in flat]
        + [pl.BlockSpec((tm, LANES), lambda i: (i, 0))] * len(lses),
        out_specs=pl.BlockSpec((tm, c), lambda i: (i, 0)),
        out_shape=jax.ShapeDtypeStruct((m, c), out_dtype),
        compiler_params=_cparams("arbitrary"),
        name="dil_merge",
    )(*flat, *lses)


def _conv_silu(xp_ref, tail_ref, x_ref, w_ref, b_ref, q):
    xp_ref[0:8, :] = tail_ref[...]
    xp_ref[8:8 + q, :] = x_ref[...]
    y = b_ref[...]
    for i in range(4):
        y = y + xp_ref[5 + i:5 + i + q, :] * w_ref[i:i + 1, :]
    tail_ref[...] = xp_ref[q:q + 8, :]
    return y * _sigmoid(y)


def _ssd_kernel(*refs, q, ng, has_state):
    (z_ref, x_ref, b_ref, c_ref, bufx_ref, bufb_ref, bufc_ref, dtc_ref, dtr_ref, alr_ref, alc_ref, dpar_ref,
     cwx_ref, cwb_ref, cwc_ref, cbx_ref, cbb_ref, cbc_ref, nw_ref) = refs[:19]
    refs = refs[19:]
    if has_state:
        h0_ref, refs = refs[0], refs[1:]
    y_ref, st_ref, xpx, xpb, xpc, tlx, tlb, tlc, state = refs
    c = pl.program_id(2)
    gw, s = SSM_GW, SSM_S

    @pl.when(c == 0)
    def _():
        tlx[...] = bufx_ref[...]
        tlb[...] = bufb_ref[...]
        tlc[...] = bufc_ref[...]
        state[...] = h0_ref[...] if has_state else jnp.zeros(state.shape, F32)

    xs_all = _conv_silu(xpx, tlx, x_ref, cwx_ref, cbx_ref, q)
    bm_all = _conv_silu(xpb, tlb, b_ref, cwb_ref, cbb_ref, q)
    cm_all = _conv_silu(xpc, tlc, c_ref, cwc_ref, cbc_ref, q)

    li = lax.broadcasted_iota(jnp.int32, (q, q), 0)
    mi = lax.broadcasted_iota(jnp.int32, (q, q), 1)
    causal = li >= mi
    tri_c = causal.astype(F32)
    tri_r = (li <= mi).astype(F32)
    low = lax.broadcasted_iota(jnp.int32, (q, LANES), 1) < SSM_P
    on_mxu = q % LANES == 0

    def onehot(width, block):
        r = lax.broadcasted_iota(jnp.int32, (SSM_HPG, width), 0)
        c = lax.broadcasted_iota(jnp.int32, (SSM_HPG, width), 1)
        return (c // block == r).astype(BF16)

    e_chan = onehot(gw, SSM_P)
    e_time = onehot(SSM_HPG * q, q) if on_mxu else None

    def expand(v, e):
        hi = v.astype(BF16)
        r1 = v - hi.astype(F32)
        mid = r1.astype(BF16)
        lo = (r1 - mid.astype(F32)).astype(BF16)
        return (jnp.dot(hi, e, preferred_element_type=F32) + jnp.dot(mid, e, preferred_element_type=F32)
                + jnp.dot(lo, e, preferred_element_type=F32))

    def widen(v):
        if on_mxu:
            return expand(v, e_chan)
        return jnp.concatenate([jnp.where(low, v[:, 2 * j:2 * j + 1], v[:, 2 * j + 1:2 * j + 2])
                                for j in range(gw // LANES)], axis=1)

    for k in range(ng):
        xs = xs_all[:, k * gw:(k + 1) * gw]
        bmb = bm_all[:, k * s:(k + 1) * s].astype(BF16)
        cmb = cm_all[:, k * s:(k + 1) * s].astype(BF16)
        dt_c = dtc_ref[k]
        dt_r = dtr_ref[k]
        a_r = -jnp.exp(alr_ref[k])
        a_c = -jnp.exp(alc_ref[k])
        cs_c = jnp.dot(tri_c, dt_c * a_r, precision=HI, preferred_element_type=F32)
        cs_r = jnp.dot(dt_r * a_c, tri_r, precision=HI, preferred_element_type=F32)
        cs_last = cs_c[q - 1:q, :]

        tend, ecs = jnp.exp(cs_last - cs_c), jnp.exp(cs_c)
        if on_mxu:
            wide = widen(jnp.concatenate([dt_c, tend, ecs], axis=0))
            dtw, tendw, ecsw = wide[:q], wide[q:2 * q], wide[2 * q:]
            segc = expand(cs_c, e_time)
        else:
            dtw, tendw, ecsw = widen(dt_c), widen(tend), widen(ecs)
        xdt = xs * dtw
        xe = (xdt * tendw).astype(BF16)
        xdtb = xdt.astype(BF16)
        cb = _nt_dot(cmb, bmb)
        st = state[k]
        y_off = _nt_dot(cmb, st.astype(BF16)) * ecsw
        y_diag = []
        for j in range(gw // LANES):
            pair = []
            for h in (2 * j, 2 * j + 1):
                col = segc[:, h * q:(h + 1) * q] if on_mxu else cs_c[:, h:h + 1]
                seg = col - cs_r[h:h + 1, :]
                gm = (cb * jnp.exp(jnp.where(causal, seg, NEG_INF))).astype(BF16)
                pair.append(jnp.dot(gm, xdtb[:, j * LANES:(j + 1) * LANES], preferred_element_type=F32))
            y_diag.append(jnp.where(low, pair[0], pair[1]))
        y = jnp.concatenate(y_diag, axis=1) + y_off + xs * dpar_ref[:, k * gw:(k + 1) * gw]
        new_st = _tn_dot(xe, bmb)
        dec_last = jnp.exp(cs_r[:, q - 1:q])
        st = jnp.concatenate([st[h * SSM_P:(h + 1) * SSM_P, :] * dec_last[h:h + 1, :] for h in range(SSM_HPG)],
                             axis=0) + new_st
        state[k] = st
        st_ref[k] = st

        z = z_ref[:, k * gw:(k + 1) * gw]
        y = y * (z * _sigmoid(z))
        y = y * lax.rsqrt(jnp.mean(y * y, axis=-1, keepdims=True) + RMS_EPS) * nw_ref[:, k * gw:(k + 1) * gw]
        y_ref[:, k * gw:(k + 1) * gw] = y.astype(y_ref.dtype)


def ssd_core(n, l, zx, dt, conv_buf8, h0, p, out_dtype, ng):
    q = SSM_CHUNK if l % SSM_CHUNK == 0 else l
    nc = l // q
    g = SSM_G
    gw, s = ng * SSM_GW, ng * SSM_S
    xb0 = SSM_DI // gw
    bb0 = 2 * SSM_DI // s
    cb0 = bb0 + g // ng
    kb0 = SSM_DI // s
    kc0 = kb0 + g // ng
    dt4 = dt.reshape(n, l, g, SSM_HPG)
    dt_c = jnp.transpose(dt4, (0, 2, 1, 3))
    dt_r = jnp.transpose(dt4, (0, 2, 3, 1))
    alog = p["ssd_a_log"].reshape(g, SSM_HPG)
    row = lambda b, gi, c: (b * nc + c)
    in_specs = [
        pl.BlockSpec((q, gw), lambda b, gi, c: (row(b, gi, c), gi)),
        pl.BlockSpec((q, gw), lambda b, gi, c: (row(b, gi, c), xb0 + gi)),
        pl.BlockSpec((q, s), lambda b, gi, c: (row(b, gi, c), bb0 + gi)),
        pl.BlockSpec((q, s), lambda b, gi, c: (row(b, gi, c), cb0 + gi)),
        pl.BlockSpec((None, 8, gw), lambda b, gi, c: (b, 0, gi)),
        pl.BlockSpec((None, 8, s), lambda b, gi, c: (b, 0, kb0 + gi)),
        pl.BlockSpec((None, 8, s), lambda b, gi, c: (b, 0, kc0 + gi)),
        pl.BlockSpec((None, ng, q, SSM_HPG), lambda b, gi, c: (b, gi, c, 0)),
        pl.BlockSpec((None, ng, SSM_HPG, q), lambda b, gi, c: (b, gi, 0, c)),
        pl.BlockSpec((ng, 1, SSM_HPG), lambda b, gi, c: (gi, 0, 0)),
        pl.BlockSpec((ng, SSM_HPG, 1), lambda b, gi, c: (gi, 0, 0)),
        pl.BlockSpec((1, gw), lambda b, gi, c: (0, gi)),
        pl.BlockSpec((4, gw), lambda b, gi, c: (0, gi)),
        pl.BlockSpec((4, s), lambda b, gi, c: (0, kb0 + gi)),
        pl.BlockSpec((4, s), lambda b, gi, c: (0, kc0 + gi)),
        pl.BlockSpec((1, gw), lambda b, gi, c: (0, gi)),
        pl.BlockSpec((1, s), lambda b, gi, c: (0, kb0 + gi)),
        pl.BlockSpec((1, s), lambda b, gi, c: (0, kc0 + gi)),
        pl.BlockSpec((1, gw), lambda b, gi, c: (0, gi)),
    ]
    cw = p["ssd_conv_w"][0]
    cbias = p["ssd_conv_b"]
    args = [zx, zx, zx, zx, conv_buf8, conv_buf8, conv_buf8, dt_c, dt_r,
            alog.reshape(g, 1, SSM_HPG), alog.reshape(g, SSM_HPG, 1),
            jnp.repeat(p["ssd_d"].reshape(-1), SSM_P).reshape(1, SSM_DI),
            cw, cw, cw, cbias, cbias, cbias, p["ssd_norm"]]
    st_spec = pl.BlockSpec((None, ng, SSM_GW, SSM_S), lambda b, gi, c: (b, gi, 0, 0))
    if h0 is not None:
        in_specs.append(st_spec)
        args.append(h0.reshape(n, g, SSM_GW, SSM_S))
    y, st = pl.pallas_call(
        functools.partial(_ssd_kernel, q=q, ng=ng, has_state=h0 is not None),
        grid=(n, g // ng, nc),
        in_specs=in_specs,
        out_specs=[pl.BlockSpec((q, gw), lambda b, gi, c: (row(b, gi, c), gi)), st_spec],
        out_shape=[jax.ShapeDtypeStruct((n * l, SSM_DI), out_dtype),
                   jax.ShapeDtypeStruct((n, g, SSM_GW, SSM_S), F32)],
        scratch_shapes=[pltpu.VMEM((8 + q, gw), F32), pltpu.VMEM((8 + q, s), F32), pltpu.VMEM((8 + q, s), F32),
                        pltpu.VMEM((8, gw), F32), pltpu.VMEM((8, s), F32), pltpu.VMEM((8, s), F32),
                        pltpu.VMEM((ng, SSM_GW, SSM_S), F32)],
        compiler_params=_cparams("arbitrary", "arbitrary", "arbitrary"),
        name="ssd_core",
    )(*args)
    return y, st.reshape(n, SSM_HEADS, SSM_P, SSM_S)


def _lru_kernel(*refs, tl, has_state, conv_done):
    (gate_ref, xb_ref, buf_ref, cw_ref, cb_ref, wr_ref, wi_ref, br_ref, bi_ref, lam_ref) = refs[:10]
    refs = refs[10:]
    if has_state:
        h0_ref, refs = refs[0], refs[1:]
    y_ref, last_ref, xp, tail, a_sc, u_sc, hs_sc, hcar, wrb, wib = refs
    t = pl.program_id(1)

    @pl.when((pl.program_id(0) == 0) & (t == 0))
    def _():
        wrb[...] = wr_ref[...].astype(BF16)
        wib[...] = wi_ref[...].astype(BF16)

    @pl.when(t == 0)
    def _():
        if not conv_done:
            tail[...] = buf_ref[...]
        hcar[...] = h0_ref[...] if has_state else jnp.zeros(hcar.shape, F32)

    if conv_done:
        xc = xb_ref[...]
    else:
        xp[0:8, :] = tail[...]
        xp[8:8 + tl, :] = xb_ref[...]
        xc = cb_ref[...]
        for i in range(4):
            xc = xc + xp[5 + i:5 + i + tl, :] * cw_ref[i:i + 1, :]
        tail[...] = xp[tl:tl + 8, :]

    xcb = xc.astype(BF16)
    bd = xc.shape[1] // LRU_BLOCKS
    rs, is_ = [], []
    for b in range(LRU_BLOCKS):
        xblk = xcb[:, b * bd:(b + 1) * bd]
        rs.append(jnp.dot(xblk, wrb[b], preferred_element_type=F32))
        is_.append(jnp.dot(xblk, wib[b], preferred_element_type=F32))
    r = _sigmoid(jnp.concatenate(rs, axis=1) + br_ref[...])
    ig = _sigmoid(jnp.concatenate(is_, axis=1) + bi_ref[...])
    log_a = -LRU_C * r * jax.nn.softplus(-lam_ref[...])
    a = jnp.exp(log_a)
    a_sc[...] = a
    u_sc[...] = jnp.sqrt(-jnp.tanh(log_a) * (a * a + 1.0)) * (ig * xc)

    def step(i, h):
        h = a_sc[pl.ds(i, 1), :] * h + u_sc[pl.ds(i, 1), :]
        hs_sc[pl.ds(i, 1), :] = h
        return h

    h = lax.fori_loop(0, tl, step, hcar[...], unroll=8)
    hcar[...] = h
    last_ref[...] = h
    y_ref[...] = (hs_sc[...] * jax.nn.gelu(gate_ref[...])).astype(y_ref.dtype)


def lru_core(n, l, gate, xb, conv_buf8, h0, p, tl, out_dtype, conv_done):
    wd = gate.shape[1]
    nt = l // tl
    bd = wd // LRU_BLOCKS
    vec = lambda a: a.reshape(1, wd)
    cst2 = lambda b, t: (0, 0)
    in_specs = [pl.BlockSpec((tl, wd), lambda b, t: (b * nt + t, 0)),
                pl.BlockSpec((tl, wd), lambda b, t: (b * nt + t, 0)),
                pl.BlockSpec((None, 8, wd), lambda b, t: (b, 0, 0)),
                pl.BlockSpec((4, wd), cst2), pl.BlockSpec((1, wd), cst2),
                pl.BlockSpec((LRU_BLOCKS, bd, bd), lambda b, t: (0, 0, 0)),
                pl.BlockSpec((LRU_BLOCKS, bd, bd), lambda b, t: (0, 0, 0)),
                pl.BlockSpec((1, wd), cst2), pl.BlockSpec((1, wd), cst2), pl.BlockSpec((1, wd), cst2)]
    args = [gate, xb, conv_buf8, p["lru_conv_w"][0], vec(p["lru_conv_b"]), p["lru_w_r"][0], p["lru_w_i"][0],
            vec(p["lru_b_r"]), vec(p["lru_b_i"]), vec(p["lru_lam"])]
    if h0 is not None:
        in_specs.append(pl.BlockSpec((None, 1, wd), lambda b, t: (b, 0, 0)))
        args.append(h0.reshape(n, 1, wd))
    y, last = pl.pallas_call(
        functools.partial(_lru_kernel, tl=tl, has_state=h0 is not None, conv_done=conv_done),
        grid=(n, nt),
        in_specs=in_specs,
        out_specs=[pl.BlockSpec((tl, wd), lambda b, t: (b * nt + t, 0)),
                   pl.BlockSpec((None, 1, wd), lambda b, t: (b, 0, 0))],
        out_shape=[jax.ShapeDtypeStruct((n * l, wd), out_dtype), jax.ShapeDtypeStruct((n, 1, wd), F32)],
        scratch_shapes=[pltpu.VMEM((8 + tl, wd), F32), pltpu.VMEM((8, wd), F32),
                        pltpu.VMEM((tl, wd), F32), pltpu.VMEM((tl, wd), F32), pltpu.VMEM((tl, wd), F32),
                        pltpu.VMEM((1, wd), F32),
                        pltpu.VMEM((LRU_BLOCKS, bd, bd), BF16), pltpu.VMEM((LRU_BLOCKS, bd, bd), BF16)],
        compiler_params=_cparams("arbitrary", "arbitrary"),
        name="lru_core",
    )(*args)
    return y, last.reshape(n, wd)


def _pad_buf8(buf):
    return jnp.pad(buf, ((0, 0), (5, 0), (0, 0)))


def _trunk(seq, x, mod, pos, cache, p):
    n, l, m = seq.n, seq.l, seq.m
    prompt = seq.prompt
    d = x.shape[1]
    act = BF16 if prompt else F32
    new = {}
    reps = 1 if prompt else n

    h = norm_mod(seq, x, p["g_mix"], 0, mod, 0)
    rope64 = rope_tables(pos, SWA_HD, reps)
    nq = SWA_QH * SWA_HD
    q = matmul(seq, h, p["swa_w_qkv"], 0, col0=0, ncols=nq, tn=1024, out_dtype=act, epi="rope", rope=rope64,
               n_rope=nq // 1024, hd=SWA_HD, name="swa_q")
    kv = matmul(seq, h, p["swa_w_qkv"], 0, col0=nq, tn=256, out_dtype=F32, epi="rope", rope=rope64,
                n_rope=1, hd=SWA_HD, name="swa_kv")
    akw = dict(d=1, hd=SWA_HD, kvh=SWA_KVH, grp=SWA_QH // SWA_KVH, q_blk=0, k_blk=0, v_blk=1,
               sinks=p["swa_sinks"][0], want_lse=False)
    if prompt:
        o, _ = attn_prompt(n, l, q, kv, out_dtype=BF16, **akw)
        keep = min(SWA_WINDOW, l)
        new["swa_kv"] = kv.reshape(n, l, -1)[:, l - keep:].reshape(n, keep, 2, SWA_KVH, SWA_HD)
    else:
        c = cache["swa_kv"][0]
        o, _ = attn_sample(n, l, q, kv, c.reshape(n, c.shape[1], 2 * SWA_KVH, SWA_HD), **akw)
        new["swa_kv"] = kv.reshape(n, l, 2, SWA_KVH, SWA_HD)
    x = matmul(seq, o, p["swa_w_o"], 0, tn=1024, out_dtype=F32, epi="resid", res=x, mod=mod, mod_layer=0, mod_k=2,
               name="swa_o")
    x = ffn(seq, norm_mod(seq, x, p["g_ffn"], 0, mod, 3), p["w_ff1"], p["w_ff2"], 0, x, mod)

    h = norm_mod(seq, x, p["g_mix"], 1, mod, 0)
    nzx = SSM_DI + SSM_DI + 2 * SSM_G * SSM_S
    w_dt = p["ssd_w_in"][:, :, nzx:]
    dt = matmul(seq, h, w_dt, 0, tn=SSM_HEADS, out_dtype=F32, bias=p["ssd_dt_bias"], epi="softplus", name="ssd_dt")
    zx = matmul(seq, h, p["ssd_w_in"], 0, col0=0, ncols=nzx, tn=1024, out_dtype=F32, name="ssd_in")
    zx3 = zx.reshape(n, l, nzx)
    if prompt:
        buf8 = jnp.zeros((n, 8, nzx - SSM_DI), F32)
        h0 = None
        new["ssd_conv"] = zx3[:, l - 3:, SSM_DI:]
    else:
        buf8 = _pad_buf8(cache["ssd_conv"][0])
        h0 = cache["ssd"][0]
        new["ssd_conv"] = jnp.concatenate([cache["ssd_conv"][0], zx3[:, :, SSM_DI:]], axis=1)[:, -3:]
    y, new["ssd"] = ssd_core(n, l, zx, dt, buf8, h0, p, act, ng=4 if prompt else SSM_G)
    x = matmul(seq, y, p["ssd_w_out"], 0, tn=512, out_dtype=F32, epi="resid", res=x, mod=mod,
               mod_layer=1, mod_k=2, name="ssd_out")
    x = ffn(seq, norm_mod(seq, x, p["g_ffn"], 1, mod, 3), p["w_ff1"], p["w_ff2"], 1, x, mod)

    h = norm_mod(seq, x, p["g_mix"], 2, mod, 0)
    rope128 = rope_tables(pos, DIL_HD, reps)
    ng = len(DIL_PATTERN)
    nq = ng * DIL_QH * DIL_HD
    q = matmul(seq, h, p["dil_w_qkv"], 0, col0=0, ncols=nq, tn=1024, out_dtype=F32, epi="rope", rope=rope128,
               n_rope=nq // 1024, hd=DIL_HD, name="dil_q")
    ck = DIL_KVH * DIL_HD
    grp = DIL_QH // DIL_KVH
    outs, lses = [], []
    for g, ((w, dil), key) in enumerate(zip(DIL_PATTERN, ("dil_kv_w128", "dil_kv_w512", "dil_kv_w2048"))):
        kv = matmul(seq, h, p["dil_w_qkv"], 0, col0=nq + g * ck, ncols=2 * ck, cstride=ng, tn=ck, out_dtype=F32,
                    epi="rope", rope=rope128, n_rope=1, hd=DIL_HD, name="dil_kv")
        akw = dict(d=dil, hd=DIL_HD, kvh=DIL_KVH, grp=grp, q_blk=g, k_blk=0, v_blk=1)
        if not prompt:
            c = cache[key][0]
            o, lse = attn_sample(n, l, q, kv, c.reshape(n, c.shape[1], 2 * DIL_KVH, DIL_HD), **akw)
            new[key] = kv.reshape(n, l, 2, DIL_KVH, DIL_HD)
        else:
            if dil == 1:
                o, lse = attn_prompt(n, l, q, kv, **akw)
            else:
                o, lse = attn_dil_prompt(n, l, q, kv, d=dil, kvh=DIL_KVH, grp=grp, q_head0=g * DIL_QH,
                                         k_head0=0, v_head0=DIL_KVH)
            keep = min(w, l)
            new[key] = kv.reshape(n, l, -1)[:, l - keep:].reshape(n, keep, 2, DIL_KVH, DIL_HD)
        outs.append(o)
        lses.append(lse)
    o = dil_merge(outs, lses, seq.tm if not prompt else 512, act)
    x = matmul(seq, o, p["dil_w_o"], 0, tn=1024, out_dtype=F32, epi="resid", res=x, mod=mod, mod_layer=2, mod_k=2,
               name="dil_o")
    x = ffn(seq, norm_mod(seq, x, p["g_ffn"], 2, mod, 3), p["w_ff1"], p["w_ff2"], 2, x, mod)

    h = norm_mod(seq, x, p["g_mix"], 3, mod, 0)
    wd = p["lru_w_in"].shape[2] // 2
    if prompt:
        gate = matmul(seq, h, p["lru_w_in"], 0, col0=0, ncols=wd, tn=1024, out_dtype=F32, bias=p["lru_b_in"],
                      name="lru_in_gate")
        xb, tails = matmul_conv(seq, h, p["lru_w_in"], 0, col0=wd, ncols=wd, tn=1024, bias=p["lru_b_in"],
                                conv_w=p["lru_conv_w"][0], conv_b=p["lru_conv_b"].reshape(1, wd), conv_col0=0,
                                silu=False, name="lru_in_xb")
        buf8 = tails
        h0 = None
        new["lru_conv"] = tails[:, 5:, :]
    else:
        gx = matmul(seq, h, p["lru_w_in"], 0, tn=1024, out_dtype=F32, bias=p["lru_b_in"], name="lru_in")
        gate, xb = gx[:, :wd], gx[:, wd:]
        buf8 = _pad_buf8(cache["lru_conv"][0])
        h0 = cache["lru"][0]
        new["lru_conv"] = jnp.concatenate([cache["lru_conv"][0], xb.reshape(n, l, wd)], axis=1)[:, -3:]
    y, new["lru"] = lru_core(n, l, gate, xb, buf8, h0, p, 256 if prompt else l, act, conv_done=prompt)
    x = matmul(seq, y, p["lru_w_out"], 0, tn=1024, out_dtype=F32, epi="resid", res=x, mod=mod, mod_layer=3, mod_k=2,
               name="lru_out")
    x = ffn(seq, norm_mod(seq, x, p["g_ffn"], 3, mod, 3), p["w_ff1"], p["w_ff2"], 3, x, mod)

    y = final_norm(x, p["g_final"], seq.tm)
    return y.reshape(n, l, d), {k: v[None] for k, v in new.items()}


def kernel(x_prompt, x_sample, cache_swa_kv, state_ssd_conv, state_ssd, cache_dil_kv_w128, cache_dil_kv_w512,
           cache_dil_kv_w2048, state_lru_conv, state_lru, c_prompt, c_sample, w_ada, b_ada, g_mix, g_ffn,
           w_ff1, w_ff2, g_final, swa_w_qkv, swa_sinks, swa_w_o, ssd_w_in, ssd_conv_w, ssd_conv_b,
           ssd_dt_bias, ssd_a_log, ssd_d, ssd_norm, ssd_w_out, dil_w_qkv, dil_w_o, lru_w_in, lru_b_in,
           lru_conv_w, lru_conv_b, lru_w_r, lru_b_r, lru_w_i, lru_b_i, lru_lam, lru_w_out):
    p = dict(g_mix=g_mix, g_ffn=g_ffn, w_ff1=w_ff1.astype(BF16), w_ff2=w_ff2.astype(BF16), g_final=g_final,
             swa_w_qkv=swa_w_qkv, swa_sinks=swa_sinks, swa_w_o=swa_w_o,
             ssd_w_in=ssd_w_in, ssd_conv_w=ssd_conv_w, ssd_conv_b=ssd_conv_b, ssd_dt_bias=ssd_dt_bias,
             ssd_a_log=ssd_a_log, ssd_d=ssd_d, ssd_norm=ssd_norm, ssd_w_out=ssd_w_out,
             dil_w_qkv=dil_w_qkv, dil_w_o=dil_w_o,
             lru_w_in=lru_w_in, lru_b_in=lru_b_in, lru_conv_w=lru_conv_w, lru_conv_b=lru_conv_b,
             lru_w_r=lru_w_r, lru_b_r=lru_b_r, lru_w_i=lru_w_i, lru_b_i=lru_b_i, lru_lam=lru_lam,
             lru_w_out=lru_w_out)
    cache = dict(swa_kv=cache_swa_kv, ssd_conv=state_ssd_conv, ssd=state_ssd, dil_kv_w128=cache_dil_kv_w128,
                 dil_kv_w512=cache_dil_kv_w512, dil_kv_w2048=cache_dil_kv_w2048, lru_conv=state_lru_conv,
                 lru=state_lru)
    nb, l, d = x_prompt.shape
    ns, ls, _ = x_sample.shape
    depth = w_ada.shape[0]
    rows = -(-(nb + ns) // 16) * 16
    c_all = jnp.concatenate([c_prompt, c_sample, jnp.zeros((rows - nb - ns, d), F32)], axis=0)
    mod = ada_mod(c_all, w_ada, b_ada)
    mod_p = mod.reshape(depth, rows, 1, 6 * d)
    mod_s = jnp.repeat(mod[:, nb:nb + ns], ls, axis=1).reshape(depth, 1, ns * ls, 6 * d)

    seq_p = Seq(nb, l, 1024, True)
    seq_s = Seq(ns, ls, ns * ls, False)
    y_p, sp = _trunk(seq_p, x_prompt.reshape(nb * l, d), mod_p, jnp.arange(l, dtype=jnp.int32), None, p)
    y_s, ss = _trunk(seq_s, x_sample.reshape(ns * ls, d), mod_s, PAST_LEN + jnp.arange(ls, dtype=jnp.int32), cache, p)
    return (y_p, y_s,
            sp["swa_kv"], ss["swa_kv"],
            sp["ssd_conv"], ss["ssd_conv"],
            sp["ssd"], ss["ssd"],
            sp["dil_kv_w128"], ss["dil_kv_w128"],
            sp["dil_kv_w512"], ss["dil_kv_w512"],
            sp["dil_kv_w2048"], ss["dil_kv_w2048"],
            sp["lru_conv"], ss["lru_conv"],
            sp["lru"], ss["lru"])
```

```python
import functools
import math

import jax
import jax.numpy as jnp
from jax import lax
from jax.experimental import pallas as pl
from jax.experimental.pallas import tpu as pltpu

F32 = jnp.float32
BF16 = jnp.bfloat16
HI = lax.Precision.HIGHEST

RMS_EPS = 1e-6
ROPE_THETA = 10000.0
NEG_INF = -1e30
LANES = 128
MXU_COLS = 256
VMEM_LIMIT = 56 * 1024 * 1024

PAST_LEN = 16384
ATTN_BLOCK = 128
SWA_WINDOW, SWA_HD, SWA_QH, SWA_KVH = 128, 64, 32, 4
DIL_PATTERN = ((128, 1), (512, 4), (2048, 16))
DIL_HD, DIL_QH, DIL_KVH = 128, 16, 4
SSM_HEADS, SSM_P, SSM_S, SSM_G, SSM_HPG, SSM_CHUNK = 64, 64, 128, 8, 8, 128
SSM_DI = SSM_HEADS * SSM_P
SSM_GW = SSM_HPG * SSM_P
LRU_BLOCKS, LRU_C = 8, 8.0


def _cparams(*sem):
    return pltpu.CompilerParams(dimension_semantics=sem, vmem_limit_bytes=VMEM_LIMIT)


def _nt_dot(a, b):
    return lax.dot_general(a, b, (((1,), (1,)), ((), ())), preferred_element_type=F32)


def _sigmoid(x):
    return 0.5 * jnp.tanh(0.5 * x) + 0.5


def _tn_dot(a, b):
    return lax.dot_general(a, b, (((0,), (0,)), ((), ())), preferred_element_type=F32)


class Seq:
    def __init__(self, n, l, tm, prompt):
        self.n, self.l, self.m, self.tm, self.prompt = n, l, n * l, tm, prompt

    def mod_spec(self, layer, k, d, row_axis):
        tm, l = self.tm, self.l
        if self.prompt:
            return pl.BlockSpec((None, None, 1, d), lambda *g: (layer, (g[row_axis] * tm) // l, 0, k))
        return pl.BlockSpec((None, None, tm, d), lambda *g: (layer, 0, g[row_axis], k))


def _ada_kernel(c_ref, w_ref, b_ref, o_ref):
    c = c_ref[...]
    cond = (c * jax.nn.sigmoid(c)).astype(BF16)
    o_ref[...] = jnp.dot(cond, w_ref[...].astype(BF16), preferred_element_type=F32) + b_ref[...]


def ada_mod(c_all, w_ada, b_ada, tn=1024):
    depth, d, n6 = w_ada.shape
    r = c_all.shape[0]
    return pl.pallas_call(
        _ada_kernel,
        grid=(depth, n6 // tn),
        in_specs=[pl.BlockSpec((r, d), lambda a, j: (0, 0)),
                  pl.BlockSpec((None, d, tn), lambda a, j: (a, 0, j)),
                  pl.BlockSpec((None, 1, tn), lambda a, j: (a, 0, j))],
        out_specs=pl.BlockSpec((None, r, tn), lambda a, j: (a, 0, j)),
        out_shape=jax.ShapeDtypeStruct((depth, r, n6), F32),
        compiler_params=_cparams("arbitrary", "arbitrary"),
        name="ada_mod",
    )(c_all, w_ada, b_ada.reshape(depth, 1, n6))


def _norm_mod_kernel(x_ref, g_ref, sh_ref, sc_ref, o_ref):
    x = x_ref[...]
    y = x * lax.rsqrt(jnp.mean(x * x, axis=-1, keepdims=True) + RMS_EPS) * g_ref[...]
    o_ref[...] = (y * (1.0 + sc_ref[...]) + sh_ref[...]).astype(o_ref.dtype)


def _norm_kernel(x_ref, g_ref, o_ref):
    x = x_ref[...]
    o_ref[...] = x * lax.rsqrt(jnp.mean(x * x, axis=-1, keepdims=True) + RMS_EPS) * g_ref[...]


def norm_mod(seq, x, g, layer, mod, k_shift):
    m, d = x.shape
    tm = seq.tm
    sub = Seq(seq.n, seq.l, tm, seq.prompt)
    return pl.pallas_call(
        _norm_mod_kernel,
        grid=(m // tm,),
        in_specs=[pl.BlockSpec((tm, d), lambda i: (i, 0)),
                  pl.BlockSpec((None, 1, d), lambda i: (layer, 0, 0)),
                  sub.mod_spec(layer, k_shift, d, 0),
                  sub.mod_spec(layer, k_shift + 1, d, 0)],
        out_specs=pl.BlockSpec((tm, d), lambda i: (i, 0)),
        out_shape=jax.ShapeDtypeStruct((m, d), BF16),
        compiler_params=_cparams("arbitrary"),
        name="norm_mod",
    )(x, g.reshape(g.shape[0], 1, d), mod, mod)


def final_norm(x, g, tm):
    m, d = x.shape
    return pl.pallas_call(
        _norm_kernel,
        grid=(m // tm,),
        in_specs=[pl.BlockSpec((tm, d), lambda i: (i, 0)), pl.BlockSpec((1, d), lambda i: (0, 0))],
        out_specs=pl.BlockSpec((tm, d), lambda i: (i, 0)),
        out_shape=jax.ShapeDtypeStruct((m, d), F32),
        compiler_params=_cparams("arbitrary"),
        name="final_norm",
    )(x, g.reshape(1, d))


def _rope_cols(y, cos, sin, hd):
    outs = []
    for c in range(y.shape[1] // LANES):
        yc = y[:, c * LANES:(c + 1) * LANES]
        if hd == LANES:
            partner = pltpu.roll(yc, LANES // 2, axis=1)
        else:
            lane = lax.broadcasted_iota(jnp.int32, yc.shape, 1)
            partner = jnp.where(lane % hd < hd // 2, pltpu.roll(yc, LANES - hd // 2, axis=1),
                                pltpu.roll(yc, hd // 2, axis=1))
        outs.append(yc * cos + partner * sin)
    return outs[0] if len(outs) == 1 else jnp.concatenate(outs, axis=1)


def _mm_kernel(*refs, epi, n_rope, all_rope, hd, has_bias):
    x_ref, w_ref = refs[0], refs[1]
    o_ref, wb_ref = refs[-2], refs[-1]
    extra = refs[2:-2]

    @pl.when(pl.program_id(1) == 0)
    def _():
        wb_ref[...] = w_ref[...].astype(BF16)

    if has_bias:
        bias_ref, extra = extra[0], extra[1:]
    xb = x_ref[...].astype(BF16)
    tn = o_ref.shape[1]
    sub = min(tn, MXU_COLS)
    for c in range(tn // sub):
        cs = slice(c * sub, (c + 1) * sub)
        y = jnp.dot(xb, wb_ref[:, cs], preferred_element_type=F32)
        if has_bias:
            y = y + bias_ref[:, cs]
        if epi == "none":
            out = y
        elif epi == "softplus":
            out = jax.nn.softplus(y)
        elif epi == "resid":
            res_ref, gate_ref = extra
            out = res_ref[:, cs] + gate_ref[:, cs] * y
        elif epi == "rope":
            cos_ref, sin_ref = extra
            out = _rope_cols(y, cos_ref[...], sin_ref[...], hd)
            if not all_rope:
                out = jnp.where(pl.program_id(0) < n_rope, out, y)
        else:
            raise ValueError(epi)
        o_ref[:, cs] = out.astype(o_ref.dtype)


def matmul(seq, x, w, layer, *, col0=0, ncols=None, tn, out_dtype, tm=None, bias=None, epi="none",
           rope=None, n_rope=0, hd=LANES, res=None, mod=None, mod_layer=0, mod_k=0, cstride=1, name="mm"):
    m, k = x.shape
    ntot = w.shape[2]
    ncols = ntot - col0 if ncols is None else ncols
    tm = seq.tm if tm is None else tm
    cb = col0 // tn
    assert col0 % tn == 0 and ncols % tn == 0 and m % tm == 0
    in_specs = [pl.BlockSpec((tm, k), lambda j, i: (i, 0)),
                pl.BlockSpec((None, k, tn), lambda j, i: (layer, 0, cb + j * cstride))]
    args = [x, w]
    if bias is not None:
        in_specs.append(pl.BlockSpec((None, 1, tn), lambda j, i: (0, 0, cb + j * cstride)))
        args.append(bias.reshape(1, 1, -1))
    if epi == "rope":
        cos, sin = rope
        nrt = cos.shape[0] // tm
        for t in (cos, sin):
            in_specs.append(pl.BlockSpec((tm, LANES), lambda j, i: (i % nrt, 0)))
            args.append(t)
    if epi == "resid":
        sub = Seq(seq.n, seq.l, tm, seq.prompt)
        base = sub.mod_spec(mod_layer, mod_k, tn, 1)
        nk = w.shape[2] // tn
        gate_spec = pl.BlockSpec(base.block_shape,
                                 lambda j, i, f=base.index_map: f(j, i)[:3] + (f(j, i)[3] * nk + j,))
        in_specs += [pl.BlockSpec((tm, tn), lambda j, i: (i, j)), gate_spec]
        args += [res, mod]
    return pl.pallas_call(
        functools.partial(_mm_kernel, epi=epi, n_rope=n_rope, all_rope=n_rope >= ncols // tn, hd=hd,
                          has_bias=bias is not None),
        grid=(ncols // tn, m // tm),
        in_specs=in_specs,
        out_specs=pl.BlockSpec((tm, tn), lambda j, i: (i, j)),
        out_shape=jax.ShapeDtypeStruct((m, ncols), out_dtype),
        scratch_shapes=[pltpu.VMEM((k, tn), BF16)],
        compiler_params=_cparams("arbitrary", "arbitrary"),
        name=name,
    )(*args)


def _mm_conv_kernel(*refs, tiles_per_seq, silu, has_bias):
    x_ref, w_ref = refs[0], refs[1]
    k = 3 if has_bias else 2
    cw_ref, cb_ref = refs[k], refs[k + 1]
    o_ref, tails_ref, wb_ref, xp_ref, tail_ref = refs[k + 2:]
    i = pl.program_id(1)

    @pl.when(i == 0)
    def _():
        wb_ref[...] = w_ref[...].astype(BF16)

    @pl.when(i % tiles_per_seq == 0)
    def _():
        tail_ref[...] = jnp.zeros(tail_ref.shape, F32)

    xb = x_ref[...].astype(BF16)
    tm, tn = o_ref.shape
    sub = min(tn, MXU_COLS)
    for c in range(tn // sub):
        cs = slice(c * sub, (c + 1) * sub)
        y = jnp.dot(xb, wb_ref[:, cs], preferred_element_type=F32)
        if has_bias:
            y = y + refs[2][:, cs]
        xp_ref[c, 0:8, :] = tail_ref[:, cs]
        xp_ref[c, 8:8 + tm, :] = y
        acc = cb_ref[:, cs]
        for t in range(4):
            acc = acc + xp_ref[c, 5 + t:5 + t + tm, :] * cw_ref[t:t + 1, cs]
        last = y[tm - 8:tm, :]
        tail_ref[:, cs] = last
        tails_ref[:, cs] = last
        o_ref[:, cs] = (acc * _sigmoid(acc) if silu else acc).astype(o_ref.dtype)


def matmul_conv(seq, x, w, layer, *, col0, ncols, tn, conv_w, conv_b, conv_col0, silu, bias=None, name="mm_conv"):
    m, k = x.shape
    tm = seq.tm
    assert seq.prompt and seq.l % tm == 0 and col0 % tn == 0 and ncols % tn == 0 and conv_col0 % tn == 0
    cb, ccb, tps = col0 // tn, conv_col0 // tn, seq.l // tm
    sub = min(tn, MXU_COLS)
    in_specs = [pl.BlockSpec((tm, k), lambda j, i: (i, 0)),
                pl.BlockSpec((None, k, tn), lambda j, i: (layer, 0, cb + j))]
    args = [x, w]
    if bias is not None:
        in_specs.append(pl.BlockSpec((None, 1, tn), lambda j, i: (0, 0, cb + j)))
        args.append(bias.reshape(1, 1, -1))
    in_specs += [pl.BlockSpec((4, tn), lambda j, i: (0, ccb + j)), pl.BlockSpec((1, tn), lambda j, i: (0, ccb + j))]
    args += [conv_w, conv_b]
    return pl.pallas_call(
        functools.partial(_mm_conv_kernel, tiles_per_seq=tps, silu=silu, has_bias=bias is not None),
        grid=(ncols // tn, m // tm),
        in_specs=in_specs,
        out_specs=[pl.BlockSpec((tm, tn), lambda j, i: (i, j)),
                   pl.BlockSpec((None, 8, tn), lambda j, i: (i // tps, 0, j))],
        out_shape=[jax.ShapeDtypeStruct((m, ncols), F32), jax.ShapeDtypeStruct((seq.n, 8, ncols), F32)],
        scratch_shapes=[pltpu.VMEM((k, tn), BF16), pltpu.VMEM((tn // sub, 8 + tm, sub), F32),
                        pltpu.VMEM((8, tn), F32)],
        compiler_params=_cparams("arbitrary", "arbitrary"),
        name=name,
    )(*args)


def _ffn_kernel(x_ref, w1_ref, w2_ref, res_ref, gate_ref, o_ref, h_ref, *, na, tf):
    s = pl.program_id(1)

    @pl.when(s < na)
    def _():
        h = jnp.dot(x_ref[...], w1_ref[...], preferred_element_type=F32)
        col = pl.multiple_of(s * tf, tf)
        h_ref[:, pl.ds(col, tf)] = jnp.square(jnp.maximum(h, 0.0)).astype(BF16)

    @pl.when(s >= na)
    def _():
        y = jnp.dot(h_ref[...], w2_ref[...], preferred_element_type=F32)
        o_ref[...] = res_ref[...] + gate_ref[...] * y


def _ffn_tiles(tm, d, f):
    for tf, tn in ((2048, 512), (1024, 512), (1024, 256), (512, 256)):
        windows = 2 * 2 * (tm * d + d * tf + f * tn) + 2 * 2 * 4 * tm * tn
        if windows + 2 * tm * f + 6 * tm * tf <= 0.9 * VMEM_LIMIT:
            return tf, tn
    raise ValueError("FFN row tile too large for VMEM")


def ffn(seq, h, w1, w2, layer, res, mod):
    m, d = h.shape
    f = w1.shape[2]
    tm = seq.tm
    tf, tn = _ffn_tiles(tm, d, f)
    na, nb = f // tf, d // tn
    sub = Seq(seq.n, seq.l, tm, seq.prompt)
    base = sub.mod_spec(layer, 5, tn, 0)
    ocol = lambda s: jnp.maximum(s - na, 0)
    gate_spec = pl.BlockSpec(base.block_shape,
                             lambda i, s, fm=base.index_map: fm(i, s)[:3] + (5 * nb + ocol(s),))
    return pl.pallas_call(
        functools.partial(_ffn_kernel, na=na, tf=tf),
        grid=(m // tm, na + nb),
        in_specs=[pl.BlockSpec((tm, d), lambda i, s: (i, 0)),
                  pl.BlockSpec((None, d, tf), lambda i, s: (layer, 0, jnp.minimum(s, na - 1))),
                  pl.BlockSpec((None, f, tn), lambda i, s: (layer, 0, ocol(s))),
                  pl.BlockSpec((tm, tn), lambda i, s: (i, ocol(s))),
                  gate_spec],
        out_specs=pl.BlockSpec((tm, tn), lambda i, s: (i, ocol(s))),
        out_shape=jax.ShapeDtypeStruct((m, d), F32),
        scratch_shapes=[pltpu.VMEM((tm, f), BF16)],
        compiler_params=_cparams("arbitrary", "arbitrary"),
        name="ffn",
    )(h, w1, w2, res, mod)


def rope_tables(pos, hd, reps):
    half = hd // 2
    inv = ROPE_THETA ** (-jnp.arange(half, dtype=F32) / half)
    ang = pos.astype(F32)[:, None] * inv[None, :]
    cos = jnp.concatenate([jnp.cos(ang), jnp.cos(ang)], axis=-1)
    sin = jnp.concatenate([-jnp.sin(ang), jnp.sin(ang)], axis=-1)
    lane_reps = LANES // hd
    return jnp.tile(cos, (reps, lane_reps)), jnp.tile(sin, (reps, lane_reps))


def _attn_prompt_kernel(*refs, hd, kvh, grp, scale, has_sinks, want_lse):
    if has_sinks:
        sink_ref, refs = refs[0], refs[1:]
    q_ref, kp_ref, kc_ref, vp_ref, vc_ref = refs[:5]
    o_ref = refs[5]
    lse_ref = refs[6] if want_lse else None
    ub = pl.program_id(2)
    bq = q_ref.shape[0]
    rows = grp * bq
    iq = lax.broadcasted_iota(jnp.int32, (rows, 2 * bq), 0) % bq
    jk = lax.broadcasted_iota(jnp.int32, (rows, 2 * bq), 1)
    mask = (jk >= iq) & (jk <= iq + bq) & ((jk >= bq) | (ub > 0))
    rcol = lax.broadcasted_iota(jnp.int32, (rows, 1), 0)
    lane = lax.broadcasted_iota(jnp.int32, (bq, LANES), 1)
    lse_tile = jnp.zeros((bq, LANES), F32)
    for kh in range(kvh):
        ksl = slice(kh * hd, (kh + 1) * hd)
        kk = jnp.concatenate([kp_ref[:, ksl], kc_ref[:, ksl]], axis=0).astype(BF16)
        vv = jnp.concatenate([vp_ref[:, ksl], vc_ref[:, ksl]], axis=0).astype(BF16)
        qs = jnp.concatenate([q_ref[:, (kh * grp + g) * hd:(kh * grp + g + 1) * hd] for g in range(grp)], axis=0)
        s = jnp.where(mask, _nt_dot(qs.astype(BF16), kk) * scale, NEG_INF)
        mx = jnp.max(s, axis=-1, keepdims=True)
        if has_sinks:
            sk = jnp.zeros((rows, 1), F32)
            for g in range(grp):
                sk = jnp.where(rcol // bq == g, sink_ref[kh * grp + g], sk)
            mx = jnp.maximum(mx, sk)
        p = jnp.exp(s - mx)
        den = jnp.sum(p, axis=-1, keepdims=True)
        if has_sinks:
            den = den + jnp.exp(sk - mx)
        o = jnp.dot(p.astype(BF16), vv, preferred_element_type=F32) / den
        lse = mx + jnp.log(den)
        for g in range(grp):
            h = kh * grp + g
            o_ref[:, h * hd:(h + 1) * hd] = o[g * bq:(g + 1) * bq, :].astype(o_ref.dtype)
            if want_lse:
                lse_tile = jnp.where(lane == h, lse[g * bq:(g + 1) * bq, :], lse_tile)
    if want_lse:
        lse_ref[...] = lse_tile


def attn_prompt(n, l, q, kv, *, d, hd, kvh, grp, q_blk, k_blk, v_blk, sinks=None, want_lse=True, out_dtype=F32):
    bq = ATTN_BLOCK
    cq, ck = grp * kvh * hd, kvh * hd
    nq, nk = q.shape[1] // cq, kv.shape[1] // ck
    lu = l // d
    qv = q.reshape(n, lu, d * q.shape[1])
    kvv = kv.reshape(n, lu, d * kv.shape[1])
    prev = lambda u: jnp.maximum(u - 1, 0)
    in_specs = [pl.BlockSpec((None, bq, cq), lambda b, r, u: (b, u, r * nq + q_blk)),
                pl.BlockSpec((None, bq, ck), lambda b, r, u: (b, prev(u), r * nk + k_blk)),
                pl.BlockSpec((None, bq, ck), lambda b, r, u: (b, u, r * nk + k_blk)),
                pl.BlockSpec((None, bq, ck), lambda b, r, u: (b, prev(u), r * nk + v_blk)),
                pl.BlockSpec((None, bq, ck), lambda b, r, u: (b, u, r * nk + v_blk))]
    args = [qv, kvv, kvv, kvv, kvv]
    if sinks is not None:
        in_specs.insert(0, pl.BlockSpec(memory_space=pltpu.SMEM))
        args.insert(0, sinks)
    out_shape = [jax.ShapeDtypeStruct((n, lu, d * cq), out_dtype)]
    out_specs = [pl.BlockSpec((None, bq, cq), lambda b, r, u: (b, u, r))]
    if want_lse:
        out_shape.append(jax.ShapeDtypeStruct((n, lu, d * LANES), F32))
        out_specs.append(pl.BlockSpec((None, bq, LANES), lambda b, r, u: (b, u, r)))
    outs = pl.pallas_call(
        functools.partial(_attn_prompt_kernel, hd=hd, kvh=kvh, grp=grp, scale=hd ** -0.5,
                          has_sinks=sinks is not None, want_lse=want_lse),
        grid=(n, d, lu // bq),
        in_specs=in_specs,
        out_specs=out_specs,
        out_shape=out_shape,
        compiler_params=_cparams("arbitrary", "arbitrary", "arbitrary"),
        name=f"attn_prompt_d{d}",
    )(*args)
    o = outs[0].reshape(n * l, cq)
    return (o, outs[1].reshape(n * l, LANES)) if want_lse else (o, None)


def _attn_dil_kernel(*refs, d, grp, scale, has_prev):
    q_refs = refs[:grp]
    kp_ref, kc_ref, vp_ref, vc_ref = refs[grp:grp + 4]
    o_refs = refs[grp + 4:2 * grp + 4]
    lse_ref = refs[2 * grp + 4]
    ub, kh = pl.program_id(1), pl.program_id(2)
    bq = ATTN_BLOCK
    nk = 2 * bq if has_prev else bq
    iq = lax.broadcasted_iota(jnp.int32, (grp * bq, nk), 0) % bq
    jk = lax.broadcasted_iota(jnp.int32, (grp * bq, nk), 1)
    if has_prev:
        mask = (jk >= iq) & (jk <= iq + bq) & ((jk >= bq) | (ub > 0))
    else:
        mask = jk <= iq
    lane = lax.broadcasted_iota(jnp.int32, (bq, LANES), 1)

    @pl.when(kh == 0)
    def _():
        lse_ref[...] = jnp.zeros(lse_ref.shape, F32)

    for r in range(d):
        rows = pl.ds(r, bq, stride=d)
        qs = jnp.concatenate([qr[rows, :] for qr in q_refs], axis=0).astype(BF16)
        if has_prev:
            kk = jnp.concatenate([kp_ref[rows, :], kc_ref[rows, :]], axis=0).astype(BF16)
            vv = jnp.concatenate([vp_ref[rows, :], vc_ref[rows, :]], axis=0).astype(BF16)
        else:
            kk = kc_ref[rows, :].astype(BF16)
            vv = vc_ref[rows, :].astype(BF16)
        s = jnp.where(mask, _nt_dot(qs, kk) * scale, NEG_INF)
        mx = jnp.max(s, axis=-1, keepdims=True)
        p = jnp.exp(s - mx)
        den = jnp.sum(p, axis=-1, keepdims=True)
        o = jnp.dot(p.astype(BF16), vv, preferred_element_type=F32) / den
        lse = mx + jnp.log(den)
        tile = lse_ref[r * bq:(r + 1) * bq, :]
        for g in range(grp):
            o_refs[g][rows, :] = o[g * bq:(g + 1) * bq, :]
            tile = jnp.where(lane == kh * grp + g, lse[g * bq:(g + 1) * bq, :], tile)
        lse_ref[r * bq:(r + 1) * bq, :] = tile


def attn_dil_prompt(n, l, q, kv, *, d, kvh, grp, q_head0, k_head0, v_head0):
    hd = LANES
    rt = d * ATTN_BLOCK
    nub = l // rt
    has_prev = nub > 1
    prev = lambda u: jnp.maximum(u - 1, 0)
    cur = lambda u: u
    blk = lambda col, rowf: pl.BlockSpec((rt, hd), lambda b, u, kh: (b * nub + rowf(u), col(kh)))
    q_specs = [blk(lambda kh, g=g: q_head0 + kh * grp + g, cur) for g in range(grp)]
    outs = pl.pallas_call(
        functools.partial(_attn_dil_kernel, d=d, grp=grp, scale=hd ** -0.5, has_prev=has_prev),
        grid=(n, nub, kvh),
        in_specs=q_specs + [blk(lambda kh: k_head0 + kh, prev), blk(lambda kh: k_head0 + kh, cur),
                            blk(lambda kh: v_head0 + kh, prev), blk(lambda kh: v_head0 + kh, cur)],
        out_specs=[blk(lambda kh: kh, cur)] * grp + [blk(lambda kh: 0, cur)],
        out_shape=[jax.ShapeDtypeStruct((n * l, kvh * hd), F32)] * grp + [jax.ShapeDtypeStruct((n * l, LANES), F32)],
        compiler_params=_cparams("arbitrary", "arbitrary", "arbitrary"),
        name=f"attn_dil_d{d}",
    )(*([q] * grp), kv, kv, kv, kv)
    lse = outs[grp].reshape(n * nub, d, ATTN_BLOCK, LANES).transpose(0, 2, 1, 3).reshape(n * l, LANES)
    return list(outs[:grp]), lse


def _attn_sample_kernel(*refs, hd, kvh, grp, scale, d, has_sinks, want_lse):
    if has_sinks:
        sink_ref, refs = refs[0], refs[1:]
    q_ref, kn_ref, vn_ref, c_ref = refs[:4]
    o_ref = refs[4]
    lse_ref = refs[5] if want_lse else None
    nb = c_ref.shape[0]
    lq = q_ref.shape[0] // nb
    flat = len(c_ref.shape) == 3
    w = c_ref.shape[1] // (2 * kvh) if flat else c_ref.shape[1]

    def cache_head(bi, idx):
        return c_ref[bi, pl.ds(idx, w, stride=2 * kvh), :] if flat else c_ref[bi, :, idx, :]

    rows = grp * lq
    row = lax.broadcasted_iota(jnp.int32, (rows, w + lq), 0)
    jk = lax.broadcasted_iota(jnp.int32, (rows, w + lq), 1)
    dist = row % lq + w - jk
    mask = (dist >= 0) & (dist <= w) & (dist % d == 0)
    rcol = lax.broadcasted_iota(jnp.int32, (rows, 1), 0)
    lane = lax.broadcasted_iota(jnp.int32, (lq, LANES), 1)
    for bi in range(nb):
        rsl = slice(bi * lq, (bi + 1) * lq)
        lse_tile = jnp.zeros((lq, LANES), F32)
        for kh in range(kvh):
            ksl = slice(kh * hd, (kh + 1) * hd)
            kk = jnp.concatenate([cache_head(bi, kh), kn_ref[rsl, ksl]], axis=0).astype(BF16)
            vv = jnp.concatenate([cache_head(bi, kvh + kh), vn_ref[rsl, ksl]], axis=0).astype(BF16)
            qs = jnp.concatenate([q_ref[rsl, (kh * grp + g) * hd:(kh * grp + g + 1) * hd] for g in range(grp)],
                                 axis=0)
            s = jnp.where(mask, _nt_dot(qs.astype(BF16), kk) * scale, NEG_INF)
            mx = jnp.max(s, axis=-1, keepdims=True)
            if has_sinks:
                sk = jnp.zeros((rows, 1), F32)
                for g in range(grp):
                    sk = jnp.where(rcol // lq == g, sink_ref[kh * grp + g], sk)
                mx = jnp.maximum(mx, sk)
            p = jnp.exp(s - mx)
            den = jnp.sum(p, axis=-1, keepdims=True)
            if has_sinks:
                den = den + jnp.exp(sk - mx)
            o = jnp.dot(p.astype(BF16), vv, preferred_element_type=F32) / den
            lse = mx + jnp.log(den)
            for g in range(grp):
                h = kh * grp + g
                o_ref[rsl, h * hd:(h + 1) * hd] = o[g * lq:(g + 1) * lq, :]
                if want_lse:
                    lse_tile = jnp.where(lane == h, lse[g * lq:(g + 1) * lq, :], lse_tile)
        if want_lse:
            lse_ref[rsl, :] = lse_tile


def attn_sample(n, l, q, kv, cache, *, d, hd, kvh, grp, q_blk, k_blk, v_blk, sinks=None, want_lse=True):
    cq, ck = grp * kvh * hd, kvh * hd
    w = cache.shape[1]
    nb = 4 if (w <= 512 and n % 4 == 0) else 1
    if hd == LANES:
        cache = cache.reshape(n, w * 2 * kvh, hd)
        cache_spec = pl.BlockSpec((nb, w * 2 * kvh, hd), lambda b: (b, 0, 0))
    else:
        cache_spec = pl.BlockSpec((nb, w, 2 * kvh, hd), lambda b: (b, 0, 0, 0))
    l, n = nb * l, n // nb
    in_specs = [pl.BlockSpec((l, cq), lambda b: (b, q_blk)),
                pl.BlockSpec((l, ck), lambda b: (b, k_blk)),
                pl.BlockSpec((l, ck), lambda b: (b, v_blk)),
                cache_spec]
    args = [q, kv, kv, cache]
    if sinks is not None:
        in_specs.insert(0, pl.BlockSpec(memory_space=pltpu.SMEM))
        args.insert(0, sinks)
    out_shape = [jax.ShapeDtypeStruct((n * l, cq), F32)]
    out_specs = [pl.BlockSpec((l, cq), lambda b: (b, 0))]
    if want_lse:
        out_shape.append(jax.ShapeDtypeStruct((n * l, LANES), F32))
        out_specs.append(pl.BlockSpec((l, LANES), lambda b: (b, 0)))
    outs = pl.pallas_call(
        functools.partial(_attn_sample_kernel, hd=hd, kvh=kvh, grp=grp, scale=hd ** -0.5, d=d,
                          has_sinks=sinks is not None, want_lse=want_lse),
        grid=(n,),
        in_specs=in_specs,
        out_specs=out_specs,
        out_shape=out_shape,
        compiler_params=_cparams("arbitrary"),
        name=f"attn_sample_d{d}",
    )(*args)
    return (outs[0], outs[1]) if want_lse else (outs[0], None)


def _dil_merge_kernel(*refs, counts, qh, hd):
    no = sum(counts)
    o_refs, l_refs, out_ref = refs[:no], refs[no:no + len(counts)], refs[-1]
    ls = [r[...] for r in l_refs]
    mx = functools.reduce(jnp.maximum, ls)
    es = [jnp.exp(v - mx) for v in ls]
    tot = functools.reduce(lambda a, b: a + b, es)
    wts = [e / tot for e in es]
    for h in range(qh):
        acc, first = None, 0
        for cnt, wt in zip(counts, wts):
            if cnt == 1:
                piece = o_refs[first][:, h * hd:(h + 1) * hd]
            else:
                piece = o_refs[first + h % cnt][:, (h // cnt) * hd:(h // cnt + 1) * hd]
            first += cnt
            term = wt[:, h:h + 1] * piece
            acc = term if acc is None else acc + term
        out_ref[:, h * hd:(h + 1) * hd] = acc.astype(out_ref.dtype)


def dil_merge(outs, lses, tm, out_dtype):
    groups = [o if isinstance(o, (list, tuple)) else [o] for o in outs]
    flat = [a for grp_arrays in groups for a in grp_arrays]
    m = flat[0].shape[0]
    c = DIL_QH * DIL_HD
    return pl.pallas_call(
        functools.partial(_dil_merge_kernel, counts=tuple(len(g) for g in groups), qh=DIL_QH, hd=DIL_HD),
        grid=(m // tm,),
        in_specs=[pl.BlockSpec((tm, a.shape[1]), lambda i: (i, 0)) for a in flat]
        + [pl.BlockSpec((tm, LANES), lambda i: (i, 0))] * len(lses),
        out_specs=pl.BlockSpec((tm, c), lambda i: (i, 0)),
        out_shape=jax.ShapeDtypeStruct((m, c), out_dtype),
        compiler_params=_cparams("arbitrary"),
        name="dil_merge",
    )(*flat, *lses)


def _conv_silu(xp_ref, tail_ref, x_ref, w_ref, b_ref, q):
    xp_ref[0:8, :] = tail_ref[...]
    xp_ref[8:8 + q, :] = x_ref[...]
    y = b_ref[...]
    for i in range(4):
        y = y + xp_ref[5 + i:5 + i + q, :] * w_ref[i:i + 1, :]
    tail_ref[...] = xp_ref[q:q + 8, :]
    return y * _sigmoid(y)


def _ssd_kernel(*refs, q, ng, has_state):
    (z_ref, x_ref, b_ref, c_ref, bufx_ref, bufb_ref, bufc_ref, dtc_ref, dtr_ref, alr_ref, alc_ref, dpar_ref,
     cwx_ref, cwb_ref, cwc_ref, cbx_ref, cbb_ref, cbc_ref, nw_ref) = refs[:19]
    refs = refs[19:]
    if has_state:
        h0_ref, refs = refs[0], refs[1:]
    y_ref, st_ref, xpx, xpb, xpc, tlx, tlb, tlc, state = refs
    c = pl.program_id(2)
    gw, s = SSM_GW, SSM_S

    @pl.when(c == 0)
    def _():
        tlx[...] = bufx_ref[...]
        tlb[...] = bufb_ref[...]
        tlc[...] = bufc_ref[...]
        state[...] = h0_ref[...] if has_state else jnp.zeros(state.shape, F32)

    xs_all = _conv_silu(xpx, tlx, x_ref, cwx_ref, cbx_ref, q)
    bm_all = _conv_silu(xpb, tlb, b_ref, cwb_ref, cbb_ref, q)
    cm_all = _conv_silu(xpc, tlc, c_ref, cwc_ref, cbc_ref, q)

    li = lax.broadcasted_iota(jnp.int32, (q, q), 0)
    mi = lax.broadcasted_iota(jnp.int32, (q, q), 1)
    causal = li >= mi
    tri_c = causal.astype(F32)
    tri_r = (li <= mi).astype(F32)
    low = lax.broadcasted_iota(jnp.int32, (q, LANES), 1) < SSM_P
    on_mxu = q % LANES == 0

    def onehot(width, block):
        r = lax.broadcasted_iota(jnp.int32, (SSM_HPG, width), 0)
        c = lax.broadcasted_iota(jnp.int32, (SSM_HPG, width), 1)
        return (c // block == r).astype(BF16)

    e_chan = onehot(gw, SSM_P)
    e_time = onehot(SSM_HPG * q, q) if on_mxu else None

    def expand(v, e):
        hi = v.astype(BF16)
        r1 = v - hi.astype(F32)
        mid = r1.astype(BF16)
        lo = (r1 - mid.astype(F32)).astype(BF16)
        return (jnp.dot(hi, e, preferred_element_type=F32) + jnp.dot(mid, e, preferred_element_type=F32)
                + jnp.dot(lo, e, preferred_element_type=F32))

    def widen(v):
        if on_mxu:
            return expand(v, e_chan)
        return jnp.concatenate([jnp.where(low, v[:, 2 * j:2 * j + 1], v[:, 2 * j + 1:2 * j + 2])
                                for j in range(gw // LANES)], axis=1)

    for k in range(ng):
        xs = xs_all[:, k * gw:(k + 1) * gw]
        bmb = bm_all[:, k * s:(k + 1) * s].astype(BF16)
        cmb = cm_all[:, k * s:(k + 1) * s].astype(BF16)
        dt_c = dtc_ref[k]
        dt_r = dtr_ref[k]
        a_r = -jnp.exp(alr_ref[k])
        a_c = -jnp.exp(alc_ref[k])
        cs_c = jnp.dot(tri_c, dt_c * a_r, precision=HI, preferred_element_type=F32)
        cs_r = jnp.dot(dt_r * a_c, tri_r, precision=HI, preferred_element_type=F32)
        cs_last = cs_c[q - 1:q, :]

        tend, ecs = jnp.exp(cs_last - cs_c), jnp.exp(cs_c)
        if on_mxu:
            wide = widen(jnp.concatenate([dt_c, tend, ecs], axis=0))
            dtw, tendw, ecsw = wide[:q], wide[q:2 * q], wide[2 * q:]
            segc = expand(cs_c, e_time)
        else:
            dtw, tendw, ecsw = widen(dt_c), widen(tend), widen(ecs)
        xdt = xs * dtw
        xe = (xdt * tendw).astype(BF16)
        xdtb = xdt.astype(BF16)
        cb = _nt_dot(cmb, bmb)
        st = state[k]
        y_off = _nt_dot(cmb, st.astype(BF16)) * ecsw
        y_diag = []
        for j in range(gw // LANES):
            pair = []
            for h in (2 * j, 2 * j + 1):
                col = segc[:, h * q:(h + 1) * q] if on_mxu else cs_c[:, h:h + 1]
                seg = col - cs_r[h:h + 1, :]
                gm = (cb * jnp.exp(jnp.where(causal, seg, NEG_INF))).astype(BF16)
                pair.append(jnp.dot(gm, xdtb[:, j * LANES:(j + 1) * LANES], preferred_element_type=F32))
            y_diag.append(jnp.where(low, pair[0], pair[1]))
        y = jnp.concatenate(y_diag, axis=1) + y_off + xs * dpar_ref[:, k * gw:(k + 1) * gw]
        new_st = _tn_dot(xe, bmb)
        dec_last = jnp.exp(cs_r[:, q - 1:q])
        st = jnp.concatenate([st[h * SSM_P:(h + 1) * SSM_P, :] * dec_last[h:h + 1, :] for h in range(SSM_HPG)],
                             axis=0) + new_st
        state[k] = st
        st_ref[k] = st

        z = z_ref[:, k * gw:(k + 1) * gw]
        y = y * (z * _sigmoid(z))
        y = y * lax.rsqrt(jnp.mean(y * y, axis=-1, keepdims=True) + RMS_EPS) * nw_ref[:, k * gw:(k + 1) * gw]
        y_ref[:, k * gw:(k + 1) * gw] = y.astype(y_ref.dtype)


def ssd_core(n, l, zx, dt, conv_buf8, h0, p, out_dtype, ng):
    q = SSM_CHUNK if l % SSM_CHUNK == 0 else l
    nc = l // q
    g = SSM_G
    gw, s = ng * SSM_GW, ng * SSM_S
    xb0 = SSM_DI // gw
    bb0 = 2 * SSM_DI // s
    cb0 = bb0 + g // ng
    kb0 = SSM_DI // s
    kc0 = kb0 + g // ng
    dt4 = dt.reshape(n, l, g, SSM_HPG)
    dt_c = jnp.transpose(dt4, (0, 2, 1, 3))
    dt_r = jnp.transpose(dt4, (0, 2, 3, 1))
    alog = p["ssd_a_log"].reshape(g, SSM_HPG)
    row = lambda b, gi, c: (b * nc + c)
    in_specs = [
        pl.BlockSpec((q, gw), lambda b, gi, c: (row(b, gi, c), gi)),
        pl.BlockSpec((q, gw), lambda b, gi, c: (row(b, gi, c), xb0 + gi)),
        pl.BlockSpec((q, s), lambda b, gi, c: (row(b, gi, c), bb0 + gi)),
        pl.BlockSpec((q, s), lambda b, gi, c: (row(b, gi, c), cb0 + gi)),
        pl.BlockSpec((None, 8, gw), lambda b, gi, c: (b, 0, gi)),
        pl.BlockSpec((None, 8, s), lambda b, gi, c: (b, 0, kb0 + gi)),
        pl.BlockSpec((None, 8, s), lambda b, gi, c: (b, 0, kc0 + gi)),
        pl.BlockSpec((None, ng, q, SSM_HPG), lambda b, gi, c: (b, gi, c, 0)),
        pl.BlockSpec((None, ng, SSM_HPG, q), lambda b, gi, c: (b, gi, 0, c)),
        pl.BlockSpec((ng, 1, SSM_HPG), lambda b, gi, c: (gi, 0, 0)),
        pl.BlockSpec((ng, SSM_HPG, 1), lambda b, gi, c: (gi, 0, 0)),
        pl.BlockSpec((1, gw), lambda b, gi, c: (0, gi)),
        pl.BlockSpec((4, gw), lambda b, gi, c: (0, gi)),
        pl.BlockSpec((4, s), lambda b, gi, c: (0, kb0 + gi)),
        pl.BlockSpec((4, s), lambda b, gi, c: (0, kc0 + gi)),
        pl.BlockSpec((1, gw), lambda b, gi, c: (0, gi)),
        pl.BlockSpec((1, s), lambda b, gi, c: (0, kb0 + gi)),
        pl.BlockSpec((1, s), lambda b, gi, c: (0, kc0 + gi)),
        pl.BlockSpec((1, gw), lambda b, gi, c: (0, gi)),
    ]
    cw = p["ssd_conv_w"][0]
    cbias = p["ssd_conv_b"]
    args = [zx, zx, zx, zx, conv_buf8, conv_buf8, conv_buf8, dt_c, dt_r,
            alog.reshape(g, 1, SSM_HPG), alog.reshape(g, SSM_HPG, 1),
            jnp.repeat(p["ssd_d"].reshape(-1), SSM_P).reshape(1, SSM_DI),
            cw, cw, cw, cbias, cbias, cbias, p["ssd_norm"]]
    st_spec = pl.BlockSpec((None, ng, SSM_GW, SSM_S), lambda b, gi, c: (b, gi, 0, 0))
    if h0 is not None:
        in_specs.append(st_spec)
        args.append(h0.reshape(n, g, SSM_GW, SSM_S))
    y, st = pl.pallas_call(
        functools.partial(_ssd_kernel, q=q, ng=ng, has_state=h0 is not None),
        grid=(n, g // ng, nc),
        in_specs=in_specs,
        out_specs=[pl.BlockSpec((q, gw), lambda b, gi, c: (row(b, gi, c), gi)), st_spec],
        out_shape=[jax.ShapeDtypeStruct((n * l, SSM_DI), out_dtype),
                   jax.ShapeDtypeStruct((n, g, SSM_GW, SSM_S), F32)],
        scratch_shapes=[pltpu.VMEM((8 + q, gw), F32), pltpu.VMEM((8 + q, s), F32), pltpu.VMEM((8 + q, s), F32),
                        pltpu.VMEM((8, gw), F32), pltpu.VMEM((8, s), F32), pltpu.VMEM((8, s), F32),
                        pltpu.VMEM((ng, SSM_GW, SSM_S), F32)],
        compiler_params=_cparams("arbitrary", "arbitrary", "arbitrary"),
        name="ssd_core",
    )(*args)
    return y, st.reshape(n, SSM_HEADS, SSM_P, SSM_S)


def _lru_kernel(*refs, tl, has_state, conv_done):
    (gate_ref, xb_ref, buf_ref, cw_ref, cb_ref, wr_ref, wi_ref, br_ref, bi_ref, lam_ref) = refs[:10]
    refs = refs[10:]
    if has_state:
        h0_ref, refs = refs[0], refs[1:]
    y_ref, last_ref, xp, tail, a_sc, u_sc, hs_sc, hcar, wrb, wib = refs
    t = pl.program_id(1)

    @pl.when((pl.program_id(0) == 0) & (t == 0))
    def _():
        wrb[...] = wr_ref[...].astype(BF16)
        wib[...] = wi_ref[...].astype(BF16)

    @pl.when(t == 0)
    def _():
        if not conv_done:
            tail[...] = buf_ref[...]
        hcar[...] = h0_ref[...] if has_state else jnp.zeros(hcar.shape, F32)

    if conv_done:
        xc = xb_ref[...]
    else:
        xp[0:8, :] = tail[...]
        xp[8:8 + tl, :] = xb_ref[...]
        xc = cb_ref[...]
        for i in range(4):
            xc = xc + xp[5 + i:5 + i + tl, :] * cw_ref[i:i + 1, :]
        tail[...] = xp[tl:tl + 8, :]

    xcb = xc.astype(BF16)
    bd = xc.shape[1] // LRU_BLOCKS
    rs, is_ = [], []
    for b in range(LRU_BLOCKS):
        xblk = xcb[:, b * bd:(b + 1) * bd]
        rs.append(jnp.dot(xblk, wrb[b], preferred_element_type=F32))
        is_.append(jnp.dot(xblk, wib[b], preferred_element_type=F32))
    r = _sigmoid(jnp.concatenate(rs, axis=1) + br_ref[...])
    ig = _sigmoid(jnp.concatenate(is_, axis=1) + bi_ref[...])
    log_a = -LRU_C * r * jax.nn.softplus(-lam_ref[...])
    a = jnp.exp(log_a)
    a_sc[...] = a
    u_sc[...] = jnp.sqrt(-jnp.tanh(log_a) * (a * a + 1.0)) * (ig * xc)

    def step(i, h):
        h = a_sc[pl.ds(i, 1), :] * h + u_sc[pl.ds(i, 1), :]
        hs_sc[pl.ds(i, 1), :] = h
        return h

    h = lax.fori_loop(0, tl, step, hcar[...], unroll=8)
    hcar[...] = h
    last_ref[...] = h
    y_ref[...] = (hs_sc[...] * jax.nn.gelu(gate_ref[...])).astype(y_ref.dtype)


def lru_core(n, l, gate, xb, conv_buf8, h0, p, tl, out_dtype, conv_done):
    wd = gate.shape[1]
    nt = l // tl
    bd = wd // LRU_BLOCKS
    vec = lambda a: a.reshape(1, wd)
    cst2 = lambda b, t: (0, 0)
    in_specs = [pl.BlockSpec((tl, wd), lambda b, t: (b * nt + t, 0)),
                pl.BlockSpec((tl, wd), lambda b, t: (b * nt + t, 0)),
                pl.BlockSpec((None, 8, wd), lambda b, t: (b, 0, 0)),
                pl.BlockSpec((4, wd), cst2), pl.BlockSpec((1, wd), cst2),
                pl.BlockSpec((LRU_BLOCKS, bd, bd), lambda b, t: (0, 0, 0)),
                pl.BlockSpec((LRU_BLOCKS, bd, bd), lambda b, t: (0, 0, 0)),
                pl.BlockSpec((1, wd), cst2), pl.BlockSpec((1, wd), cst2), pl.BlockSpec((1, wd), cst2)]
    args = [gate, xb, conv_buf8, p["lru_conv_w"][0], vec(p["lru_conv_b"]), p["lru_w_r"][0], p["lru_w_i"][0],
            vec(p["lru_b_r"]), vec(p["lru_b_i"]), vec(p["lru_lam"])]
    if h0 is not None:
        in_specs.append(pl.BlockSpec((None, 1, wd), lambda b, t: (b, 0, 0)))
        args.append(h0.reshape(n, 1, wd))
    y, last = pl.pallas_call(
        functools.partial(_lru_kernel, tl=tl, has_state=h0 is not None, conv_done=conv_done),
        grid=(n, nt),
        in_specs=in_specs,
        out_specs=[pl.BlockSpec((tl, wd), lambda b, t: (b * nt + t, 0)),
                   pl.BlockSpec((None, 1, wd), lambda b, t: (b, 0, 0))],
        out_shape=[jax.ShapeDtypeStruct((n * l, wd), out_dtype), jax.ShapeDtypeStruct((n, 1, wd), F32)],
        scratch_shapes=[pltpu.VMEM((8 + tl, wd), F32), pltpu.VMEM((8, wd), F32),
                        pltpu.VMEM((tl, wd), F32), pltpu.VMEM((tl, wd), F32), pltpu.VMEM((tl, wd), F32),
                        pltpu.VMEM((1, wd), F32),
                        pltpu.VMEM((LRU_BLOCKS, bd, bd), BF16), pltpu.VMEM((LRU_BLOCKS, bd, bd), BF16)],
        compiler_params=_cparams("arbitrary", "arbitrary"),
        name="lru_core",
    )(*args)
    return y, last.reshape(n, wd)


def _pad_buf8(buf):
    return jnp.pad(buf, ((0, 0), (5, 0), (0, 0)))


def _trunk(seq, x, mod, pos, cache, p):
    n, l, m = seq.n, seq.l, seq.m
    prompt = seq.prompt
    d = x.shape[1]
    act = BF16 if prompt else F32
    new = {}
    reps = 1 if prompt else n

    h = norm_mod(seq, x, p["g_mix"], 0, mod, 0)
    rope64 = rope_tables(pos, SWA_HD, reps)
    nq = SWA_QH * SWA_HD
    q = matmul(seq, h, p["swa_w_qkv"], 0, col0=0, ncols=nq, tn=1024, out_dtype=act, epi="rope", rope=rope64,
               n_rope=nq // 1024, hd=SWA_HD, name="swa_q")
    kv = matmul(seq, h, p["swa_w_qkv"], 0, col0=nq, tn=256, out_dtype=F32, epi="rope", rope=rope64,
                n_rope=1, hd=SWA_HD, name="swa_kv")
    akw = dict(d=1, hd=SWA_HD, kvh=SWA_KVH, grp=SWA_QH // SWA_KVH, q_blk=0, k_blk=0, v_blk=1,
               sinks=p["swa_sinks"][0], want_lse=False)
    if prompt:
        o, _ = attn_prompt(n, l, q, kv, out_dtype=BF16, **akw)
        keep = min(SWA_WINDOW, l)
        new["swa_kv"] = kv.reshape(n, l, -1)[:, l - keep:].reshape(n, keep, 2, SWA_KVH, SWA_HD)
    else:
        c = cache["swa_kv"][0]
        o, _ = attn_sample(n, l, q, kv, c.reshape(n, c.shape[1], 2 * SWA_KVH, SWA_HD), **akw)
        new["swa_kv"] = kv.reshape(n, l, 2, SWA_KVH, SWA_HD)
    x = matmul(seq, o, p["swa_w_o"], 0, tn=1024, out_dtype=F32, epi="resid", res=x, mod=mod, mod_layer=0, mod_k=2,
               name="swa_o")
    x = ffn(seq, norm_mod(seq, x, p["g_ffn"], 0, mod, 3), p["w_ff1"], p["w_ff2"], 0, x, mod)

    h = norm_mod(seq, x, p["g_mix"], 1, mod, 0)
    nzx = SSM_DI + SSM_DI + 2 * SSM_G * SSM_S
    w_dt = p["ssd_w_in"][:, :, nzx:]
    dt = matmul(seq, h, w_dt, 0, tn=SSM_HEADS, out_dtype=F32, bias=p["ssd_dt_bias"], epi="softplus", name="ssd_dt")
    zx = matmul(seq, h, p["ssd_w_in"], 0, col0=0, ncols=nzx, tn=1024, out_dtype=F32, name="ssd_in")
    zx3 = zx.reshape(n, l, nzx)
    if prompt:
        buf8 = jnp.zeros((n, 8, nzx - SSM_DI), F32)
        h0 = None
        new["ssd_conv"] = zx3[:, l - 3:, SSM_DI:]
    else:
        buf8 = _pad_buf8(cache["ssd_conv"][0])
        h0 = cache["ssd"][0]
        new["ssd_conv"] = jnp.concatenate([cache["ssd_conv"][0], zx3[:, :, SSM_DI:]], axis=1)[:, -3:]
    y, new["ssd"] = ssd_core(n, l, zx, dt, buf8, h0, p, act, ng=SSM_G)
    x = matmul(seq, y, p["ssd_w_out"], 0, tn=512, out_dtype=F32, epi="resid", res=x, mod=mod,
               mod_layer=1, mod_k=2, name="ssd_out")
    x = ffn(seq, norm_mod(seq, x, p["g_ffn"], 1, mod, 3), p["w_ff1"], p["w_ff2"], 1, x, mod)

    h = norm_mod(seq, x, p["g_mix"], 2, mod, 0)
    rope128 = rope_tables(pos, DIL_HD, reps)
    ng = len(DIL_PATTERN)
    nq = ng * DIL_QH * DIL_HD
    q = matmul(seq, h, p["dil_w_qkv"], 0, col0=0, ncols=nq, tn=1024, out_dtype=F32, epi="rope", rope=rope128,
               n_rope=nq // 1024, hd=DIL_HD, name="dil_q")
    ck = DIL_KVH * DIL_HD
    grp = DIL_QH // DIL_KVH
    outs, lses = [], []
    for g, ((w, dil), key) in enumerate(zip(DIL_PATTERN, ("dil_kv_w128", "dil_kv_w512", "dil_kv_w2048"))):
        kv = matmul(seq, h, p["dil_w_qkv"], 0, col0=nq + g * ck, ncols=2 * ck, cstride=ng, tn=ck, out_dtype=F32,
                    epi="rope", rope=rope128, n_rope=1, hd=DIL_HD, name="dil_kv")
        akw = dict(d=dil, hd=DIL_HD, kvh=DIL_KVH, grp=grp, q_blk=g, k_blk=0, v_blk=1)
        if not prompt:
            c = cache[key][0]
            o, lse = attn_sample(n, l, q, kv, c.reshape(n, c.shape[1], 2 * DIL_KVH, DIL_HD), **akw)
            new[key] = kv.reshape(n, l, 2, DIL_KVH, DIL_HD)
        else:
            if dil == 1:
                o, lse = attn_prompt(n, l, q, kv, **akw)
            else:
                o, lse = attn_dil_prompt(n, l, q, kv, d=dil, kvh=DIL_KVH, grp=grp, q_head0=g * DIL_QH,
                                         k_head0=0, v_head0=DIL_KVH)
            keep = min(w, l)
            new[key] = kv.reshape(n, l, -1)[:, l - keep:].reshape(n, keep, 2, DIL_KVH, DIL_HD)
        outs.append(o)
        lses.append(lse)
    o = dil_merge(outs, lses, seq.tm if not prompt else 512, act)
    x = matmul(seq, o, p["dil_w_o"], 0, tn=1024, out_dtype=F32, epi="resid", res=x, mod=mod, mod_layer=2, mod_k=2,
               name="dil_o")
    x = ffn(seq, norm_mod(seq, x, p["g_ffn"], 2, mod, 3), p["w_ff1"], p["w_ff2"], 2, x, mod)

    h = norm_mod(seq, x, p["g_mix"], 3, mod, 0)
    wd = p["lru_w_in"].shape[2] // 2
    if prompt:
        gate = matmul(seq, h, p["lru_w_in"], 0, col0=0, ncols=wd, tn=1024, out_dtype=F32, bias=p["lru_b_in"],
                      name="lru_in_gate")
        xb, tails = matmul_conv(seq, h, p["lru_w_in"], 0, col0=wd, ncols=wd, tn=1024, bias=p["lru_b_in"],
                                conv_w=p["lru_conv_w"][0], conv_b=p["lru_conv_b"].reshape(1, wd), conv_col0=0,
                                silu=False, name="lru_in_xb")
        buf8 = tails
        h0 = None
        new["lru_conv"] = tails[:, 5:, :]
    else:
        gx = matmul(seq, h, p["lru_w_in"], 0, tn=1024, out_dtype=F32, bias=p["lru_b_in"], name="lru_in")
        gate, xb = gx[:, :wd], gx[:, wd:]
        buf8 = _pad_buf8(cache["lru_conv"][0])
        h0 = cache["lru"][0]
        new["lru_conv"] = jnp.concatenate([cache["lru_conv"][0], xb.reshape(n, l, wd)], axis=1)[:, -3:]
    y, new["lru"] = lru_core(n, l, gate, xb, buf8, h0, p, 256 if prompt else l, act, conv_done=prompt)
    x = matmul(seq, y, p["lru_w_out"], 0, tn=1024, out_dtype=F32, epi="resid", res=x, mod=mod, mod_layer=3, mod_k=2,
               name="lru_out")
    x = ffn(seq, norm_mod(seq, x, p["g_ffn"], 3, mod, 3), p["w_ff1"], p["w_ff2"], 3, x, mod)

    y = final_norm(x, p["g_final"], seq.tm)
    return y.reshape(n, l, d), {k: v[None] for k, v in new.items()}


def kernel(x_prompt, x_sample, cache_swa_kv, state_ssd_conv, state_ssd, cache_dil_kv_w128, cache_dil_kv_w512,
           cache_dil_kv_w2048, state_lru_conv, state_lru, c_prompt, c_sample, w_ada, b_ada, g_mix, g_ffn,
           w_ff1, w_ff2, g_final, swa_w_qkv, swa_sinks, swa_w_o, ssd_w_in, ssd_conv_w, ssd_conv_b,
           ssd_dt_bias, ssd_a_log, ssd_d, ssd_norm, ssd_w_out, dil_w_qkv, dil_w_o, lru_w_in, lru_b_in,
           lru_conv_w, lru_conv_b, lru_w_r, lru_b_r, lru_w_i, lru_b_i, lru_lam, lru_w_out):
    p = dict(g_mix=g_mix, g_ffn=g_ffn, w_ff1=w_ff1.astype(BF16), w_ff2=w_ff2.astype(BF16), g_final=g_final,
             swa_w_qkv=swa_w_qkv, swa_sinks=swa_sinks, swa_w_o=swa_w_o,
             ssd_w_in=ssd_w_in, ssd_conv_w=ssd_conv_w, ssd_conv_b=ssd_conv_b, ssd_dt_bias=ssd_dt_bias,
             ssd_a_log=ssd_a_log, ssd_d=ssd_d, ssd_norm=ssd_norm, ssd_w_out=ssd_w_out,
             dil_w_qkv=dil_w_qkv, dil_w_o=dil_w_o,
             lru_w_in=lru_w_in, lru_b_in=lru_b_in, lru_conv_w=lru_conv_w, lru_conv_b=lru_conv_b,
             lru_w_r=lru_w_r, lru_b_r=lru_b_r, lru_w_i=lru_w_i, lru_b_i=lru_b_i, lru_lam=lru_lam,
             lru_w_out=lru_w_out)
    cache = dict(swa_kv=cache_swa_kv, ssd_conv=state_ssd_conv, ssd=state_ssd, dil_kv_w128=cache_dil_kv_w128,
                 dil_kv_w512=cache_dil_kv_w512, dil_kv_w2048=cache_dil_kv_w2048, lru_conv=state_lru_conv,
                 lru=state_lru)
    nb, l, d = x_prompt.shape
    ns, ls, _ = x_sample.shape
    depth = w_ada.shape[0]
    rows = -(-(nb + ns) // 16) * 16
    c_all = jnp.concatenate([c_prompt, c_sample, jnp.zeros((rows - nb - ns, d), F32)], axis=0)
    mod = ada_mod(c_all, w_ada, b_ada)
    mod_p = mod.reshape(depth, rows, 1, 6 * d)
    mod_s = jnp.repeat(mod[:, nb:nb + ns], ls, axis=1).reshape(depth, 1, ns * ls, 6 * d)

    seq_p = Seq(nb, l, 1024, True)
    seq_s = Seq(ns, ls, ns * ls, False)
    y_p, sp = _trunk(seq_p, x_prompt.reshape(nb * l, d), mod_p, jnp.arange(l, dtype=jnp.int32), None, p)
    y_s, ss = _trunk(seq_s, x_sample.reshape(ns * ls, d), mod_s, PAST_LEN + jnp.arange(ls, dtype=jnp.int32), cache, p)
    return (y_p, y_s,
            sp["swa_kv"], ss["swa_kv"],
            sp["ssd_conv"], ss["ssd_conv"],
            sp["ssd"], ss["ssd"],
            sp["dil_kv_w128"], ss["dil_kv_w128"],
            sp["dil_kv_w512"], ss["dil_kv_w512"],
            sp["dil_kv_w2048"], ss["dil_kv_w2048"],
            sp["lru_conv"], ss["lru_conv"],
            sp["lru"], ss["lru"])
```

```python
import functools

import jax
import jax.numpy as jnp
from jax import lax
from jax.experimental import pallas as pl
from jax.experimental.pallas import tpu as pltpu

F32 = jnp.float32
BF16 = jnp.bfloat16
HI = lax.Precision.HIGHEST

RMS_EPS = 1e-6
ROPE_THETA = 10000.0
NEG_INF = -1e30
LANES = 128
MXU_COLS = 256
VMEM_LIMIT = 56 * 1024 * 1024

PAST_LEN = 16384
ATTN_BLOCK = 128
SWA_WINDOW, SWA_HD, SWA_QH, SWA_KVH = 128, 64, 32, 4
DIL_PATTERN = ((128, 1), (512, 4), (2048, 16))
DIL_HD, DIL_QH, DIL_KVH = 128, 16, 4
SSM_HEADS, SSM_P, SSM_S, SSM_G, SSM_HPG, SSM_CHUNK = 64, 64, 128, 8, 8, 128
SSM_DI = SSM_HEADS * SSM_P
SSM_GW = SSM_HPG * SSM_P
LRU_BLOCKS, LRU_C = 8, 8.0


def _cparams(*sem):
    return pltpu.CompilerParams(dimension_semantics=sem, vmem_limit_bytes=VMEM_LIMIT)


def _nt_dot(a, b):
    return lax.dot_general(a, b, (((1,), (1,)), ((), ())), preferred_element_type=F32)


def _sigmoid(x):
    return 0.5 * jnp.tanh(0.5 * x) + 0.5


def _tn_dot(a, b):
    return lax.dot_general(a, b, (((0,), (0,)), ((), ())), preferred_element_type=F32)


class Seq:
    def __init__(self, n, l, tm, prompt):
        self.n, self.l, self.m, self.tm, self.prompt = n, l, n * l, tm, prompt

    def mod_spec(self, layer, k, d, row_axis):
        tm, l = self.tm, self.l
        if self.prompt:
            return pl.BlockSpec((None, None, 1, d), lambda *g: (layer, (g[row_axis] * tm) // l, 0, k))
        return pl.BlockSpec((None, None, tm, d), lambda *g: (layer, 0, g[row_axis], k))


def _ada_kernel(c_ref, w_ref, b_ref, o_ref):
    c = c_ref[...]
    cond = (c * jax.nn.sigmoid(c)).astype(BF16)
    o_ref[...] = jnp.dot(cond, w_ref[...].astype(BF16), preferred_element_type=F32) + b_ref[...]


def ada_mod(c_all, w_ada, b_ada, tn=1024):
    depth, d, n6 = w_ada.shape
    r = c_all.shape[0]
    return pl.pallas_call(
        _ada_kernel,
        grid=(depth, n6 // tn),
        in_specs=[pl.BlockSpec((r, d), lambda a, j: (0, 0)),
                  pl.BlockSpec((None, d, tn), lambda a, j: (a, 0, j)),
                  pl.BlockSpec((None, 1, tn), lambda a, j: (a, 0, j))],
        out_specs=pl.BlockSpec((None, r, tn), lambda a, j: (a, 0, j)),
        out_shape=jax.ShapeDtypeStruct((depth, r, n6), F32),
        compiler_params=_cparams("arbitrary", "arbitrary"),
        name="ada_mod",
    )(c_all, w_ada, b_ada.reshape(depth, 1, n6))


def _norm_mod_kernel(x_ref, g_ref, sh_ref, sc_ref, o_ref):
    x = x_ref[...]
    y = x * lax.rsqrt(jnp.mean(x * x, axis=-1, keepdims=True) + RMS_EPS) * g_ref[...]
    o_ref[...] = (y * (1.0 + sc_ref[...]) + sh_ref[...]).astype(o_ref.dtype)


def _norm_kernel(x_ref, g_ref, o_ref):
    x = x_ref[...]
    o_ref[...] = x * lax.rsqrt(jnp.mean(x * x, axis=-1, keepdims=True) + RMS_EPS) * g_ref[...]


def norm_mod(seq, x, g, layer, mod, k_shift):
    m, d = x.shape
    tm = seq.tm
    sub = Seq(seq.n, seq.l, tm, seq.prompt)
    return pl.pallas_call(
        _norm_mod_kernel,
        grid=(m // tm,),
        in_specs=[pl.BlockSpec((tm, d), lambda i: (i, 0)),
                  pl.BlockSpec((None, 1, d), lambda i: (layer, 0, 0)),
                  sub.mod_spec(layer, k_shift, d, 0),
                  sub.mod_spec(layer, k_shift + 1, d, 0)],
        out_specs=pl.BlockSpec((tm, d), lambda i: (i, 0)),
        out_shape=jax.ShapeDtypeStruct((m, d), BF16),
        compiler_params=_cparams("arbitrary"),
        name="norm_mod",
    )(x, g.reshape(g.shape[0], 1, d), mod, mod)


def final_norm(x, g, tm):
    m, d = x.shape
    return pl.pallas_call(
        _norm_kernel,
        grid=(m // tm,),
        in_specs=[pl.BlockSpec((tm, d), lambda i: (i, 0)), pl.BlockSpec((1, d), lambda i: (0, 0))],
        out_specs=pl.BlockSpec((tm, d), lambda i: (i, 0)),
        out_shape=jax.ShapeDtypeStruct((m, d), F32),
        compiler_params=_cparams("arbitrary"),
        name="final_norm",
    )(x, g.reshape(1, d))


def _rope_cols(y, cos, sin, hd):
    outs = []
    for c in range(y.shape[1] // LANES):
        yc = y[:, c * LANES:(c + 1) * LANES]
        if hd == LANES:
            partner = pltpu.roll(yc, LANES // 2, axis=1)
        else:
            lane = lax.broadcasted_iota(jnp.int32, yc.shape, 1)
            partner = jnp.where(lane % hd < hd // 2, pltpu.roll(yc, LANES - hd // 2, axis=1),
                                pltpu.roll(yc, hd // 2, axis=1))
        outs.append(yc * cos + partner * sin)
    return outs[0] if len(outs) == 1 else jnp.concatenate(outs, axis=1)


def _mm_kernel(*refs, epi, n_rope, all_rope, hd, has_bias):
    x_ref, w_ref = refs[0], refs[1]
    o_ref, wb_ref = refs[-2], refs[-1]
    extra = refs[2:-2]

    @pl.when(pl.program_id(1) == 0)
    def _():
        wb_ref[...] = w_ref[...].astype(BF16)

    if has_bias:
        bias_ref, extra = extra[0], extra[1:]
    xb = x_ref[...].astype(BF16)
    tn = o_ref.shape[1]
    sub = min(tn, MXU_COLS)
    for c in range(tn // sub):
        cs = slice(c * sub, (c + 1) * sub)
        y = jnp.dot(xb, wb_ref[:, cs], preferred_element_type=F32)
        if has_bias:
            y = y + bias_ref[:, cs]
        if epi == "none":
            out = y
        elif epi == "softplus":
            out = jax.nn.softplus(y)
        elif epi == "resid":
            res_ref, gate_ref = extra
            out = res_ref[:, cs] + gate_ref[:, cs] * y
        elif epi == "rope":
            cos_ref, sin_ref = extra
            out = _rope_cols(y, cos_ref[...], sin_ref[...], hd)
            if not all_rope:
                out = jnp.where(pl.program_id(0) < n_rope, out, y)
        else:
            raise ValueError(epi)
        o_ref[:, cs] = out.astype(o_ref.dtype)


def matmul(seq, x, w, layer, *, col0=0, ncols=None, tn, out_dtype, tm=None, bias=None, epi="none",
           rope=None, n_rope=0, hd=LANES, res=None, mod=None, mod_layer=0, mod_k=0, cstride=1, name="mm"):
    m, k = x.shape
    ntot = w.shape[2]
    ncols = ntot - col0 if ncols is None else ncols
    tm = seq.tm if tm is None else tm
    cb = col0 // tn
    assert col0 % tn == 0 and ncols % tn == 0 and m % tm == 0
    in_specs = [pl.BlockSpec((tm, k), lambda j, i: (i, 0)),
                pl.BlockSpec((None, k, tn), lambda j, i: (layer, 0, cb + j * cstride))]
    args = [x, w]
    if bias is not None:
        in_specs.append(pl.BlockSpec((None, 1, tn), lambda j, i: (0, 0, cb + j * cstride)))
        args.append(bias.reshape(1, 1, -1))
    if epi == "rope":
        cos, sin = rope
        nrt = cos.shape[0] // tm
        for t in (cos, sin):
            in_specs.append(pl.BlockSpec((tm, LANES), lambda j, i: (i % nrt, 0)))
            args.append(t)
    if epi == "resid":
        sub = Seq(seq.n, seq.l, tm, seq.prompt)
        base = sub.mod_spec(mod_layer, mod_k, tn, 1)
        nk = w.shape[2] // tn
        gate_spec = pl.BlockSpec(base.block_shape,
                                 lambda j, i, f=base.index_map: f(j, i)[:3] + (f(j, i)[3] * nk + j,))
        in_specs += [pl.BlockSpec((tm, tn), lambda j, i: (i, j)), gate_spec]
        args += [res, mod]
    return pl.pallas_call(
        functools.partial(_mm_kernel, epi=epi, n_rope=n_rope, all_rope=n_rope >= ncols // tn, hd=hd,
                          has_bias=bias is not None),
        grid=(ncols // tn, m // tm),
        in_specs=in_specs,
        out_specs=pl.BlockSpec((tm, tn), lambda j, i: (i, j)),
        out_shape=jax.ShapeDtypeStruct((m, ncols), out_dtype),
        scratch_shapes=[pltpu.VMEM((k, tn), BF16)],
        compiler_params=_cparams("arbitrary", "arbitrary"),
        name=name,
    )(*args)


def _mm_conv_kernel(*refs, tiles_per_seq, silu, has_bias):
    x_ref, w_ref = refs[0], refs[1]
    k = 3 if has_bias else 2
    cw_ref, cb_ref = refs[k], refs[k + 1]
    o_ref, tails_ref, wb_ref, xp_ref, tail_ref = refs[k + 2:]
    i = pl.program_id(1)

    @pl.when(i == 0)
    def _():
        wb_ref[...] = w_ref[...].astype(BF16)

    @pl.when(i % tiles_per_seq == 0)
    def _():
        tail_ref[...] = jnp.zeros(tail_ref.shape, F32)

    xb = x_ref[...].astype(BF16)
    tm, tn = o_ref.shape
    sub = min(tn, MXU_COLS)
    for c in range(tn // sub):
        cs = slice(c * sub, (c + 1) * sub)
        y = jnp.dot(xb, wb_ref[:, cs], preferred_element_type=F32)
        if has_bias:
            y = y + refs[2][:, cs]
        xp_ref[c, 0:8, :] = tail_ref[:, cs]
        xp_ref[c, 8:8 + tm, :] = y
        acc = cb_ref[:, cs]
        for t in range(4):
            acc = acc + xp_ref[c, 5 + t:5 + t + tm, :] * cw_ref[t:t + 1, cs]
        last = y[tm - 8:tm, :]
        tail_ref[:, cs] = last
        tails_ref[:, cs] = last
        o_ref[:, cs] = (acc * _sigmoid(acc) if silu else acc).astype(o_ref.dtype)


def matmul_conv(seq, x, w, layer, *, col0, ncols, tn, conv_w, conv_b, conv_col0, silu, bias=None, name="mm_conv"):
    m, k = x.shape
    tm = seq.tm
    assert seq.prompt and seq.l % tm == 0 and col0 % tn == 0 and ncols % tn == 0 and conv_col0 % tn == 0
    cb, ccb, tps = col0 // tn, conv_col0 // tn, seq.l // tm
    sub = min(tn, MXU_COLS)
    in_specs = [pl.BlockSpec((tm, k), lambda j, i: (i, 0)),
                pl.BlockSpec((None, k, tn), lambda j, i: (layer, 0, cb + j))]
    args = [x, w]
    if bias is not None:
        in_specs.append(pl.BlockSpec((None, 1, tn), lambda j, i: (0, 0, cb + j)))
        args.append(bias.reshape(1, 1, -1))
    in_specs += [pl.BlockSpec((4, tn), lambda j, i: (0, ccb + j)), pl.BlockSpec((1, tn), lambda j, i: (0, ccb + j))]
    args += [conv_w, conv_b]
    return pl.pallas_call(
        functools.partial(_mm_conv_kernel, tiles_per_seq=tps, silu=silu, has_bias=bias is not None),
        grid=(ncols // tn, m // tm),
        in_specs=in_specs,
        out_specs=[pl.BlockSpec((tm, tn), lambda j, i: (i, j)),
                   pl.BlockSpec((None, 8, tn), lambda j, i: (i // tps, 0, j))],
        out_shape=[jax.ShapeDtypeStruct((m, ncols), F32), jax.ShapeDtypeStruct((seq.n, 8, ncols), F32)],
        scratch_shapes=[pltpu.VMEM((k, tn), BF16), pltpu.VMEM((tn // sub, 8 + tm, sub), F32),
                        pltpu.VMEM((8, tn), F32)],
        compiler_params=_cparams("arbitrary", "arbitrary"),
        name=name,
    )(*args)


def _ffn_kernel(x_ref, w1_ref, w2_ref, res_ref, gate_ref, o_ref, h_ref, *, na, tf):
    s = pl.program_id(1)

    @pl.when(s < na)
    def _():
        h = jnp.dot(x_ref[...], w1_ref[...], preferred_element_type=F32)
        col = pl.multiple_of(s * tf, tf)
        h_ref[:, pl.ds(col, tf)] = jnp.square(jnp.maximum(h, 0.0)).astype(BF16)

    @pl.when(s >= na)
    def _():
        y = jnp.dot(h_ref[...], w2_ref[...], preferred_element_type=F32)
        o_ref[...] = res_ref[...] + gate_ref[...] * y


def _ffn_tiles(tm, d, f):
    for tf, tn in ((2048, 512), (1024, 512), (1024, 256), (512, 256)):
        windows = 2 * 2 * (tm * d + d * tf + f * tn) + 2 * 2 * 4 * tm * tn
        if windows + 2 * tm * f + 6 * tm * tf <= 0.9 * VMEM_LIMIT:
            return tf, tn
    raise ValueError("FFN row tile too large for VMEM")


def ffn(seq, h, w1, w2, layer, res, mod):
    m, d = h.shape
    f = w1.shape[2]
    tm = seq.tm
    tf, tn = _ffn_tiles(tm, d, f)
    na, nb = f // tf, d // tn
    sub = Seq(seq.n, seq.l, tm, seq.prompt)
    base = sub.mod_spec(layer, 5, tn, 0)
    ocol = lambda s: jnp.maximum(s - na, 0)
    gate_spec = pl.BlockSpec(base.block_shape,
                             lambda i, s, fm=base.index_map: fm(i, s)[:3] + (5 * nb + ocol(s),))
    return pl.pallas_call(
        functools.partial(_ffn_kernel, na=na, tf=tf),
        grid=(m // tm, na + nb),
        in_specs=[pl.BlockSpec((tm, d), lambda i, s: (i, 0)),
                  pl.BlockSpec((None, d, tf), lambda i, s: (layer, 0, jnp.minimum(s, na - 1))),
                  pl.BlockSpec((None, f, tn), lambda i, s: (layer, 0, ocol(s))),
                  pl.BlockSpec((tm, tn), lambda i, s: (i, ocol(s))),
                  gate_spec],
        out_specs=pl.BlockSpec((tm, tn), lambda i, s: (i, ocol(s))),
        out_shape=jax.ShapeDtypeStruct((m, d), F32),
        scratch_shapes=[pltpu.VMEM((tm, f), BF16)],
        compiler_params=_cparams("arbitrary", "arbitrary"),
        name="ffn",
    )(h, w1, w2, res, mod)


def rope_tables(pos, hd, reps):
    half = hd // 2
    inv = ROPE_THETA ** (-jnp.arange(half, dtype=F32) / half)
    ang = pos.astype(F32)[:, None] * inv[None, :]
    cos = jnp.concatenate([jnp.cos(ang), jnp.cos(ang)], axis=-1)
    sin = jnp.concatenate([-jnp.sin(ang), jnp.sin(ang)], axis=-1)
    lane_reps = LANES // hd
    return jnp.tile(cos, (reps, lane_reps)), jnp.tile(sin, (reps, lane_reps))


def _attn_prompt_kernel(*refs, hd, kvh, grp, scale, has_sinks, want_lse):
    if has_sinks:
        sink_ref, refs = refs[0], refs[1:]
    q_ref, kp_ref, kc_ref, vp_ref, vc_ref = refs[:5]
    o_ref = refs[5]
    lse_ref = refs[6] if want_lse else None
    ub = pl.program_id(2)
    bq = q_ref.shape[0]
    rows = grp * bq
    iq = lax.broadcasted_iota(jnp.int32, (rows, 2 * bq), 0) % bq
    jk = lax.broadcasted_iota(jnp.int32, (rows, 2 * bq), 1)
    mask = (jk >= iq) & (jk <= iq + bq) & ((jk >= bq) | (ub > 0))
    rcol = lax.broadcasted_iota(jnp.int32, (rows, 1), 0)
    lane = lax.broadcasted_iota(jnp.int32, (bq, LANES), 1)
    lse_tile = jnp.zeros((bq, LANES), F32)
    for kh in range(kvh):
        ksl = slice(kh * hd, (kh + 1) * hd)
        kk = jnp.concatenate([kp_ref[:, ksl], kc_ref[:, ksl]], axis=0).astype(BF16)
        vv = jnp.concatenate([vp_ref[:, ksl], vc_ref[:, ksl]], axis=0).astype(BF16)
        qs = jnp.concatenate([q_ref[:, (kh * grp + g) * hd:(kh * grp + g + 1) * hd] for g in range(grp)], axis=0)
        s = jnp.where(mask, _nt_dot(qs.astype(BF16), kk) * scale, NEG_INF)
        mx = jnp.max(s, axis=-1, keepdims=True)
        if has_sinks:
            sk = jnp.zeros((rows, 1), F32)
            for g in range(grp):
                sk = jnp.where(rcol // bq == g, sink_ref[kh * grp + g], sk)
            mx = jnp.maximum(mx, sk)
        p = jnp.exp(s - mx)
        den = jnp.sum(p, axis=-1, keepdims=True)
        if has_sinks:
            den = den + jnp.exp(sk - mx)
        o = jnp.dot(p.astype(BF16), vv, preferred_element_type=F32) / den
        lse = mx + jnp.log(den)
        for g in range(grp):
            h = kh * grp + g
            o_ref[:, h * hd:(h + 1) * hd] = o[g * bq:(g + 1) * bq, :].astype(o_ref.dtype)
            if want_lse:
                lse_tile = jnp.where(lane == h, lse[g * bq:(g + 1) * bq, :], lse_tile)
    if want_lse:
        lse_ref[...] = lse_tile


def attn_prompt(n, l, q, kv, *, d, hd, kvh, grp, q_blk, k_blk, v_blk, sinks=None, want_lse=True, out_dtype=F32):
    bq = ATTN_BLOCK
    cq, ck = grp * kvh * hd, kvh * hd
    nq, nk = q.shape[1] // cq, kv.shape[1] // ck
    lu = l // d
    qv = q.reshape(n, lu, d * q.shape[1])
    kvv = kv.reshape(n, lu, d * kv.shape[1])
    prev = lambda u: jnp.maximum(u - 1, 0)
    in_specs = [pl.BlockSpec((None, bq, cq), lambda b, r, u: (b, u, r * nq + q_blk)),
                pl.BlockSpec((None, bq, ck), lambda b, r, u: (b, prev(u), r * nk + k_blk)),
                pl.BlockSpec((None, bq, ck), lambda b, r, u: (b, u, r * nk + k_blk)),
                pl.BlockSpec((None, bq, ck), lambda b, r, u: (b, prev(u), r * nk + v_blk)),
                pl.BlockSpec((None, bq, ck), lambda b, r, u: (b, u, r * nk + v_blk))]
    args = [qv, kvv, kvv, kvv, kvv]
    if sinks is not None:
        in_specs.insert(0, pl.BlockSpec(memory_space=pltpu.SMEM))
        args.insert(0, sinks)
    out_shape = [jax.ShapeDtypeStruct((n, lu, d * cq), out_dtype)]
    out_specs = [pl.BlockSpec((None, bq, cq), lambda b, r, u: (b, u, r))]
    if want_lse:
        out_shape.append(jax.ShapeDtypeStruct((n, lu, d * LANES), F32))
        out_specs.append(pl.BlockSpec((None, bq, LANES), lambda b, r, u: (b, u, r)))
    outs = pl.pallas_call(
        functools.partial(_attn_prompt_kernel, hd=hd, kvh=kvh, grp=grp, scale=hd ** -0.5,
                          has_sinks=sinks is not None, want_lse=want_lse),
        grid=(n, d, lu // bq),
        in_specs=in_specs,
        out_specs=out_specs,
        out_shape=out_shape,
        compiler_params=_cparams("arbitrary", "arbitrary", "arbitrary"),
        name=f"attn_prompt_d{d}",
    )(*args)
    o = outs[0].reshape(n * l, cq)
    return (o, outs[1].reshape(n * l, LANES)) if want_lse else (o, None)


def _attn_dil_kernel(*refs, d, grp, scale, has_prev):
    q_refs = refs[:grp]
    kp_ref, kc_ref, vp_ref, vc_ref = refs[grp:grp + 4]
    o_refs = refs[grp + 4:2 * grp + 4]
    lse_ref = refs[2 * grp + 4]
    ub, kh = pl.program_id(1), pl.program_id(2)
    bq = ATTN_BLOCK
    nk = 2 * bq if has_prev else bq
    iq = lax.broadcasted_iota(jnp.int32, (grp * bq, nk), 0) % bq
    jk = lax.broadcasted_iota(jnp.int32, (grp * bq, nk), 1)
    if has_prev:
        mask = (jk >= iq) & (jk <= iq + bq) & ((jk >= bq) | (ub > 0))
    else:
        mask = jk <= iq
    lane = lax.broadcasted_iota(jnp.int32, (bq, LANES), 1)

    @pl.when(kh == 0)
    def _():
        lse_ref[...] = jnp.zeros(lse_ref.shape, F32)

    for r in range(d):
        rows = pl.ds(r, bq, stride=d)
        qs = jnp.concatenate([qr[rows, :] for qr in q_refs], axis=0).astype(BF16)
        if has_prev:
            kk = jnp.concatenate([kp_ref[rows, :], kc_ref[rows, :]], axis=0).astype(BF16)
            vv = jnp.concatenate([vp_ref[rows, :], vc_ref[rows, :]], axis=0).astype(BF16)
        else:
            kk = kc_ref[rows, :].astype(BF16)
            vv = vc_ref[rows, :].astype(BF16)
        s = jnp.where(mask, _nt_dot(qs, kk) * scale, NEG_INF)
        mx = jnp.max(s, axis=-1, keepdims=True)
        p = jnp.exp(s - mx)
        den = jnp.sum(p, axis=-1, keepdims=True)
        o = jnp.dot(p.astype(BF16), vv, preferred_element_type=F32) / den
        lse = mx + jnp.log(den)
        tile = lse_ref[r * bq:(r + 1) * bq, :]
        for g in range(grp):
            o_refs[g][rows, :] = o[g * bq:(g + 1) * bq, :]
            tile = jnp.where(lane == kh * grp + g, lse[g * bq:(g + 1) * bq, :], tile)
        lse_ref[r * bq:(r + 1) * bq, :] = tile


def attn_dil_prompt(n, l, q, kv, *, d, kvh, grp, q_head0, k_head0, v_head0):
    hd = LANES
    rt = d * ATTN_BLOCK
    nub = l // rt
    has_prev = nub > 1
    prev = lambda u: jnp.maximum(u - 1, 0)
    cur = lambda u: u
    blk = lambda col, rowf: pl.BlockSpec((rt, hd), lambda b, u, kh: (b * nub + rowf(u), col(kh)))
    q_specs = [blk(lambda kh, g=g: q_head0 + kh * grp + g, cur) for g in range(grp)]
    outs = pl.pallas_call(
        functools.partial(_attn_dil_kernel, d=d, grp=grp, scale=hd ** -0.5, has_prev=has_prev),
        grid=(n, nub, kvh),
        in_specs=q_specs + [blk(lambda kh: k_head0 + kh, prev), blk(lambda kh: k_head0 + kh, cur),
                            blk(lambda kh: v_head0 + kh, prev), blk(lambda kh: v_head0 + kh, cur)],
        out_specs=[blk(lambda kh: kh, cur)] * grp + [blk(lambda kh: 0, cur)],
        out_shape=[jax.ShapeDtypeStruct((n * l, kvh * hd), F32)] * grp + [jax.ShapeDtypeStruct((n * l, LANES), F32)],
        compiler_params=_cparams("arbitrary", "arbitrary", "arbitrary"),
        name=f"attn_dil_d{d}",
    )(*([q] * grp), kv, kv, kv, kv)
    lse = outs[grp].reshape(n * nub, d, ATTN_BLOCK, LANES).transpose(0, 2, 1, 3).reshape(n * l, LANES)
    return list(outs[:grp]), lse


def _attn_sample_kernel(*refs, hd, kvh, grp, scale, d, has_sinks, want_lse):
    if has_sinks:
        sink_ref, refs = refs[0], refs[1:]
    q_ref, kn_ref, vn_ref, c_ref = refs[:4]
    o_ref = refs[4]
    lse_ref = refs[5] if want_lse else None
    nb = c_ref.shape[0]
    lq = q_ref.shape[0] // nb
    flat = len(c_ref.shape) == 3
    w = c_ref.shape[1] // (2 * kvh) if flat else c_ref.shape[1]

    def cache_head(bi, idx):
        return c_ref[bi, pl.ds(idx, w, stride=2 * kvh), :] if flat else c_ref[bi, :, idx, :]

    rows = grp * lq
    row = lax.broadcasted_iota(jnp.int32, (rows, w + lq), 0)
    jk = lax.broadcasted_iota(jnp.int32, (rows, w + lq), 1)
    dist = row % lq + w - jk
    mask = (dist >= 0) & (dist <= w) & (dist % d == 0)
    rcol = lax.broadcasted_iota(jnp.int32, (rows, 1), 0)
    lane = lax.broadcasted_iota(jnp.int32, (lq, LANES), 1)
    for bi in range(nb):
        rsl = slice(bi * lq, (bi + 1) * lq)
        lse_tile = jnp.zeros((lq, LANES), F32)
        for kh in range(kvh):
            ksl = slice(kh * hd, (kh + 1) * hd)
            kk = jnp.concatenate([cache_head(bi, kh), kn_ref[rsl, ksl]], axis=0).astype(BF16)
            vv = jnp.concatenate([cache_head(bi, kvh + kh), vn_ref[rsl, ksl]], axis=0).astype(BF16)
            qs = jnp.concatenate([q_ref[rsl, (kh * grp + g) * hd:(kh * grp + g + 1) * hd] for g in range(grp)],
                                 axis=0)
            s = jnp.where(mask, _nt_dot(qs.astype(BF16), kk) * scale, NEG_INF)
            mx = jnp.max(s, axis=-1, keepdims=True)
            if has_sinks:
                sk = jnp.zeros((rows, 1), F32)
                for g in range(grp):
                    sk = jnp.where(rcol // lq == g, sink_ref[kh * grp + g], sk)
                mx = jnp.maximum(mx, sk)
            p = jnp.exp(s - mx)
            den = jnp.sum(p, axis=-1, keepdims=True)
            if has_sinks:
                den = den + jnp.exp(sk - mx)
            o = jnp.dot(p.astype(BF16), vv, preferred_element_type=F32) / den
            lse = mx + jnp.log(den)
            for g in range(grp):
                h = kh * grp + g
                o_ref[rsl, h * hd:(h + 1) * hd] = o[g * lq:(g + 1) * lq, :]
                if want_lse:
                    lse_tile = jnp.where(lane == h, lse[g * lq:(g + 1) * lq, :], lse_tile)
        if want_lse:
            lse_ref[rsl, :] = lse_tile


def attn_sample(n, l, q, kv, cache, *, d, hd, kvh, grp, q_blk, k_blk, v_blk, sinks=None, want_lse=True):
    cq, ck = grp * kvh * hd, kvh * hd
    w = cache.shape[1]
    nb = max(b for b in (4, 2, 1) if n % b == 0 and b * w <= 4096)
    if hd == LANES:
        cache = cache.reshape(n, w * 2 * kvh, hd)
        cache_spec = pl.BlockSpec((nb, w * 2 * kvh, hd), lambda b: (b, 0, 0))
    else:
        cache_spec = pl.BlockSpec((nb, w, 2 * kvh, hd), lambda b: (b, 0, 0, 0))
    l, n = nb * l, n // nb
    in_specs = [pl.BlockSpec((l, cq), lambda b: (b, q_blk)),
                pl.BlockSpec((l, ck), lambda b: (b, k_blk)),
                pl.BlockSpec((l, ck), lambda b: (b, v_blk)),
                cache_spec]
    args = [q, kv, kv, cache]
    if sinks is not None:
        in_specs.insert(0, pl.BlockSpec(memory_space=pltpu.SMEM))
        args.insert(0, sinks)
    out_shape = [jax.ShapeDtypeStruct((n * l, cq), F32)]
    out_specs = [pl.BlockSpec((l, cq), lambda b: (b, 0))]
    if want_lse:
        out_shape.append(jax.ShapeDtypeStruct((n * l, LANES), F32))
        out_specs.append(pl.BlockSpec((l, LANES), lambda b: (b, 0)))
    outs = pl.pallas_call(
        functools.partial(_attn_sample_kernel, hd=hd, kvh=kvh, grp=grp, scale=hd ** -0.5, d=d,
                          has_sinks=sinks is not None, want_lse=want_lse),
        grid=(n,),
        in_specs=in_specs,
        out_specs=out_specs,
        out_shape=out_shape,
        compiler_params=_cparams("arbitrary"),
        name=f"attn_sample_d{d}",
    )(*args)
    return (outs[0], outs[1]) if want_lse else (outs[0], None)


def _dil_merge_kernel(*refs, counts, qh, hd):
    no = sum(counts)
    o_refs, l_refs, out_ref = refs[:no], refs[no:no + len(counts)], refs[-1]
    ls = [r[...] for r in l_refs]
    mx = functools.reduce(jnp.maximum, ls)
    es = [jnp.exp(v - mx) for v in ls]
    tot = functools.reduce(lambda a, b: a + b, es)
    wts = [e / tot for e in es]
    for h in range(qh):
        acc, first = None, 0
        for cnt, wt in zip(counts, wts):
            if cnt == 1:
                piece = o_refs[first][:, h * hd:(h + 1) * hd]
            else:
                piece = o_refs[first + h % cnt][:, (h // cnt) * hd:(h // cnt + 1) * hd]
            first += cnt
            term = wt[:, h:h + 1] * piece
            acc = term if acc is None else acc + term
        out_ref[:, h * hd:(h + 1) * hd] = acc.astype(out_ref.dtype)


def dil_merge(outs, lses, tm, out_dtype):
    groups = [o if isinstance(o, (list, tuple)) else [o] for o in outs]
    flat = [a for grp_arrays in groups for a in grp_arrays]
    m = flat[0].shape[0]
    c = DIL_QH * DIL_HD
    return pl.pallas_call(
        functools.partial(_dil_merge_kernel, counts=tuple(len(g) for g in groups), qh=DIL_QH, hd=DIL_HD),
        grid=(m // tm,),
        in_specs=[pl.BlockSpec((tm, a.shape[1]), lambda i: (i, 0)) for a in flat]
        + [pl.BlockSpec((tm, LANES), lambda i: (i, 0))] * len(lses),
        out_specs=pl.BlockSpec((tm, c), lambda i: (i, 0)),
        out_shape=jax.ShapeDtypeStruct((m, c), out_dtype),
        compiler_params=_cparams("arbitrary"),
        name="dil_merge",
    )(*flat, *lses)


def _conv_silu(xp_ref, tail_ref, x_ref, w_ref, b_ref, q):
    xp_ref[0:8, :] = tail_ref[...]
    xp_ref[8:8 + q, :] = x_ref[...]
    y = b_ref[...]
    for i in range(4):
        y = y + xp_ref[5 + i:5 + i + q, :] * w_ref[i:i + 1, :]
    tail_ref[...] = xp_ref[q:q + 8, :]
    return y * _sigmoid(y)


def _ssd_kernel(*refs, q, ng, has_state):
    (z_ref, x_ref, b_ref, c_ref, bufx_ref, bufb_ref, bufc_ref, dtc_ref, dtr_ref, alr_ref, alc_ref, dpar_ref,
     cwx_ref, cwb_ref, cwc_ref, cbx_ref, cbb_ref, cbc_ref, nw_ref) = refs[:19]
    refs = refs[19:]
    if has_state:
        h0_ref, refs = refs[0], refs[1:]
    y_ref, st_ref, xpx, xpb, xpc, tlx, tlb, tlc, state = refs
    c = pl.program_id(2)
    gw, s = SSM_GW, SSM_S

    @pl.when(c == 0)
    def _():
        tlx[...] = bufx_ref[...]
        tlb[...] = bufb_ref[...]
        tlc[...] = bufc_ref[...]
        state[...] = h0_ref[...] if has_state else jnp.zeros(state.shape, F32)

    xs_all = _conv_silu(xpx, tlx, x_ref, cwx_ref, cbx_ref, q)
    bm_all = _conv_silu(xpb, tlb, b_ref, cwb_ref, cbb_ref, q)
    cm_all = _conv_silu(xpc, tlc, c_ref, cwc_ref, cbc_ref, q)

    li = lax.broadcasted_iota(jnp.int32, (q, q), 0)
    mi = lax.broadcasted_iota(jnp.int32, (q, q), 1)
    causal = li >= mi
    tri_c = causal.astype(F32)
    tri_r = (li <= mi).astype(F32)
    low = lax.broadcasted_iota(jnp.int32, (q, LANES), 1) < SSM_P
    on_mxu = q % LANES == 0

    def onehot(width, block):
        r = lax.broadcasted_iota(jnp.int32, (SSM_HPG, width), 0)
        c = lax.broadcasted_iota(jnp.int32, (SSM_HPG, width), 1)
        return (c // block == r).astype(BF16)

    e_chan = onehot(gw, SSM_P)
    e_time = onehot(SSM_HPG * q, q) if on_mxu else None

    def expand(v, e):
        hi = v.astype(BF16)
        r1 = v - hi.astype(F32)
        mid = r1.astype(BF16)
        lo = (r1 - mid.astype(F32)).astype(BF16)
        return (jnp.dot(hi, e, preferred_element_type=F32) + jnp.dot(mid, e, preferred_element_type=F32)
                + jnp.dot(lo, e, preferred_element_type=F32))

    def widen(v):
        if on_mxu:
            return expand(v, e_chan)
        return jnp.concatenate([jnp.where(low, v[:, 2 * j:2 * j + 1], v[:, 2 * j + 1:2 * j + 2])
                                for j in range(gw // LANES)], axis=1)

    for k in range(ng):
        xs = xs_all[:, k * gw:(k + 1) * gw]
        bmb = bm_all[:, k * s:(k + 1) * s].astype(BF16)
        cmb = cm_all[:, k * s:(k + 1) * s].astype(BF16)
        dt_c = dtc_ref[k]
        dt_r = dtr_ref[k]
        a_r = -jnp.exp(alr_ref[k])
        a_c = -jnp.exp(alc_ref[k])
        cs_c = jnp.dot(tri_c, dt_c * a_r, precision=HI, preferred_element_type=F32)
        cs_r = jnp.dot(dt_r * a_c, tri_r, precision=HI, preferred_element_type=F32)
        cs_last = cs_c[q - 1:q, :]

        tend, ecs = jnp.exp(cs_last - cs_c), jnp.exp(cs_c)
        if on_mxu:
            wide = widen(jnp.concatenate([dt_c, tend, ecs], axis=0))
            dtw, tendw, ecsw = wide[:q], wide[q:2 * q], wide[2 * q:]
            segc = expand(cs_c, e_time)
        else:
            dtw, tendw, ecsw = widen(dt_c), widen(tend), widen(ecs)
        xdt = xs * dtw
        xe = (xdt * tendw).astype(BF16)
        xdtb = xdt.astype(BF16)
        cb = _nt_dot(cmb, bmb)
        st = state[k]
        y_off = _nt_dot(cmb, st.astype(BF16)) * ecsw
        y_diag = []
        for j in range(gw // LANES):
            pair = []
            for h in (2 * j, 2 * j + 1):
                col = segc[:, h * q:(h + 1) * q] if on_mxu else cs_c[:, h:h + 1]
                seg = col - cs_r[h:h + 1, :]
                gm = (cb * jnp.exp(jnp.where(causal, seg, NEG_INF))).astype(BF16)
                pair.append(jnp.dot(gm, xdtb[:, j * LANES:(j + 1) * LANES], preferred_element_type=F32))
            y_diag.append(jnp.where(low, pair[0], pair[1]))
        y = jnp.concatenate(y_diag, axis=1) + y_off + xs * dpar_ref[:, k * gw:(k + 1) * gw]
        new_st = _tn_dot(xe, bmb)
        dec_last = jnp.exp(cs_r[:, q - 1:q])
        st = jnp.concatenate([st[h * SSM_P:(h + 1) * SSM_P, :] * dec_last[h:h + 1, :] for h in range(SSM_HPG)],
                             axis=0) + new_st
        state[k] = st
        st_ref[k] = st

        z = z_ref[:, k * gw:(k + 1) * gw]
        y = y * (z * _sigmoid(z))
        y = y * lax.rsqrt(jnp.mean(y * y, axis=-1, keepdims=True) + RMS_EPS) * nw_ref[:, k * gw:(k + 1) * gw]
        y_ref[:, k * gw:(k + 1) * gw] = y.astype(y_ref.dtype)


def ssd_core(n, l, zx, dt, conv_buf8, h0, p, out_dtype, ng):
    q = SSM_CHUNK if l % SSM_CHUNK == 0 else l
    nc = l // q
    g = SSM_G
    gw, s = ng * SSM_GW, ng * SSM_S
    xb0 = SSM_DI // gw
    bb0 = 2 * SSM_DI // s
    cb0 = bb0 + g // ng
    kb0 = SSM_DI // s
    kc0 = kb0 + g // ng
    dt4 = dt.reshape(n, l, g, SSM_HPG)
    dt_c = jnp.transpose(dt4, (0, 2, 1, 3))
    dt_r = jnp.transpose(dt4, (0, 2, 3, 1))
    alog = p["ssd_a_log"].reshape(g, SSM_HPG)
    row = lambda b, gi, c: (b * nc + c)
    in_specs = [
        pl.BlockSpec((q, gw), lambda b, gi, c: (row(b, gi, c), gi)),
        pl.BlockSpec((q, gw), lambda b, gi, c: (row(b, gi, c), xb0 + gi)),
        pl.BlockSpec((q, s), lambda b, gi, c: (row(b, gi, c), bb0 + gi)),
        pl.BlockSpec((q, s), lambda b, gi, c: (row(b, gi, c), cb0 + gi)),
        pl.BlockSpec((None, 8, gw), lambda b, gi, c: (b, 0, gi)),
        pl.BlockSpec((None, 8, s), lambda b, gi, c: (b, 0, kb0 + gi)),
        pl.BlockSpec((None, 8, s), lambda b, gi, c: (b, 0, kc0 + gi)),
        pl.BlockSpec((None, ng, q, SSM_HPG), lambda b, gi, c: (b, gi, c, 0)),
        pl.BlockSpec((None, ng, SSM_HPG, q), lambda b, gi, c: (b, gi, 0, c)),
        pl.BlockSpec((ng, 1, SSM_HPG), lambda b, gi, c: (gi, 0, 0)),
        pl.BlockSpec((ng, SSM_HPG, 1), lambda b, gi, c: (gi, 0, 0)),
        pl.BlockSpec((1, gw), lambda b, gi, c: (0, gi)),
        pl.BlockSpec((4, gw), lambda b, gi, c: (0, gi)),
        pl.BlockSpec((4, s), lambda b, gi, c: (0, kb0 + gi)),
        pl.BlockSpec((4, s), lambda b, gi, c: (0, kc0 + gi)),
        pl.BlockSpec((1, gw), lambda b, gi, c: (0, gi)),
        pl.BlockSpec((1, s), lambda b, gi, c: (0, kb0 + gi)),
        pl.BlockSpec((1, s), lambda b, gi, c: (0, kc0 + gi)),
        pl.BlockSpec((1, gw), lambda b, gi, c: (0, gi)),
    ]
    cw = p["ssd_conv_w"][0]
    cbias = p["ssd_conv_b"]
    args = [zx, zx, zx, zx, conv_buf8, conv_buf8, conv_buf8, dt_c, dt_r,
            alog.reshape(g, 1, SSM_HPG), alog.reshape(g, SSM_HPG, 1),
            jnp.repeat(p["ssd_d"].reshape(-1), SSM_P).reshape(1, SSM_DI),
            cw, cw, cw, cbias, cbias, cbias, p["ssd_norm"]]
    st_spec = pl.BlockSpec((None, ng, SSM_GW, SSM_S), lambda b, gi, c: (b, gi, 0, 0))
    if h0 is not None:
        in_specs.append(st_spec)
        args.append(h0.reshape(n, g, SSM_GW, SSM_S))
    y, st = pl.pallas_call(
        functools.partial(_ssd_kernel, q=q, ng=ng, has_state=h0 is not None),
        grid=(n, g // ng, nc),
        in_specs=in_specs,
        out_specs=[pl.BlockSpec((q, gw), lambda b, gi, c: (row(b, gi, c), gi)), st_spec],
        out_shape=[jax.ShapeDtypeStruct((n * l, SSM_DI), out_dtype),
                   jax.ShapeDtypeStruct((n, g, SSM_GW, SSM_S), F32)],
        scratch_shapes=[pltpu.VMEM((8 + q, gw), F32), pltpu.VMEM((8 + q, s), F32), pltpu.VMEM((8 + q, s), F32),
                        pltpu.VMEM((8, gw), F32), pltpu.VMEM((8, s), F32), pltpu.VMEM((8, s), F32),
                        pltpu.VMEM((ng, SSM_GW, SSM_S), F32)],
        compiler_params=_cparams("arbitrary", "arbitrary", "arbitrary"),
        name="ssd_core",
    )(*args)
    return y, st.reshape(n, SSM_HEADS, SSM_P, SSM_S)


def _lru_kernel(*refs, tl, has_state, conv_done):
    (gate_ref, xb_ref, buf_ref, cw_ref, cb_ref, wr_ref, wi_ref, br_ref, bi_ref, lam_ref) = refs[:10]
    refs = refs[10:]
    if has_state:
        h0_ref, refs = refs[0], refs[1:]
    y_ref, last_ref, xp, tail, a_sc, u_sc, hs_sc, hcar, wrb, wib = refs
    t = pl.program_id(1)

    @pl.when((pl.program_id(0) == 0) & (t == 0))
    def _():
        wrb[...] = wr_ref[...].astype(BF16)
        wib[...] = wi_ref[...].astype(BF16)

    @pl.when(t == 0)
    def _():
        if not conv_done:
            tail[...] = buf_ref[...]
        hcar[...] = h0_ref[...] if has_state else jnp.zeros(hcar.shape, F32)

    if conv_done:
        xc = xb_ref[...]
    else:
        xp[0:8, :] = tail[...]
        xp[8:8 + tl, :] = xb_ref[...]
        xc = cb_ref[...]
        for i in range(4):
            xc = xc + xp[5 + i:5 + i + tl, :] * cw_ref[i:i + 1, :]
        tail[...] = xp[tl:tl + 8, :]

    xcb = xc.astype(BF16)
    bd = xc.shape[1] // LRU_BLOCKS
    rs, is_ = [], []
    for b in range(LRU_BLOCKS):
        xblk = xcb[:, b * bd:(b + 1) * bd]
        rs.append(jnp.dot(xblk, wrb[b], preferred_element_type=F32))
        is_.append(jnp.dot(xblk, wib[b], preferred_element_type=F32))
    r = _sigmoid(jnp.concatenate(rs, axis=1) + br_ref[...])
    ig = _sigmoid(jnp.concatenate(is_, axis=1) + bi_ref[...])
    log_a = -LRU_C * r * jax.nn.softplus(-lam_ref[...])
    a = jnp.exp(log_a)
    a_sc[...] = a
    u_sc[...] = jnp.sqrt(-jnp.tanh(log_a) * (a * a + 1.0)) * (ig * xc)

    def step(i, h):
        h = a_sc[pl.ds(i, 1), :] * h + u_sc[pl.ds(i, 1), :]
        hs_sc[pl.ds(i, 1), :] = h
        return h

    h = lax.fori_loop(0, tl, step, hcar[...], unroll=8)
    hcar[...] = h
    last_ref[...] = h
    y_ref[...] = (hs_sc[...] * jax.nn.gelu(gate_ref[...])).astype(y_ref.dtype)


def lru_core(n, l, gate, xb, conv_buf8, h0, p, tl, out_dtype, conv_done):
    wd = gate.shape[1]
    nt = l // tl
    bd = wd // LRU_BLOCKS
    vec = lambda a: a.reshape(1, wd)
    cst2 = lambda b, t: (0, 0)
    in_specs = [pl.BlockSpec((tl, wd), lambda b, t: (b * nt + t, 0)),
                pl.BlockSpec((tl, wd), lambda b, t: (b * nt + t, 0)),
                pl.BlockSpec((None, 8, wd), lambda b, t: (b, 0, 0)),
                pl.BlockSpec((4, wd), cst2), pl.BlockSpec((1, wd), cst2),
                pl.BlockSpec((LRU_BLOCKS, bd, bd), lambda b, t: (0, 0, 0)),
                pl.BlockSpec((LRU_BLOCKS, bd, bd), lambda b, t: (0, 0, 0)),
                pl.BlockSpec((1, wd), cst2), pl.BlockSpec((1, wd), cst2), pl.BlockSpec((1, wd), cst2)]
    args = [gate, xb, conv_buf8, p["lru_conv_w"][0], vec(p["lru_conv_b"]), p["lru_w_r"][0], p["lru_w_i"][0],
            vec(p["lru_b_r"]), vec(p["lru_b_i"]), vec(p["lru_lam"])]
    if h0 is not None:
        in_specs.append(pl.BlockSpec((None, 1, wd), lambda b, t: (b, 0, 0)))
        args.append(h0.reshape(n, 1, wd))
    y, last = pl.pallas_call(
        functools.partial(_lru_kernel, tl=tl, has_state=h0 is not None, conv_done=conv_done),
        grid=(n, nt),
        in_specs=in_specs,
        out_specs=[pl.BlockSpec((tl, wd), lambda b, t: (b * nt + t, 0)),
                   pl.BlockSpec((None, 1, wd), lambda b, t: (b, 0, 0))],
        out_shape=[jax.ShapeDtypeStruct((n * l, wd), out_dtype), jax.ShapeDtypeStruct((n, 1, wd), F32)],
        scratch_shapes=[pltpu.VMEM((8 + tl, wd), F32), pltpu.VMEM((8, wd), F32),
                        pltpu.VMEM((tl, wd), F32), pltpu.VMEM((tl, wd), F32), pltpu.VMEM((tl, wd), F32),
                        pltpu.VMEM((1, wd), F32),
                        pltpu.VMEM((LRU_BLOCKS, bd, bd), BF16), pltpu.VMEM((LRU_BLOCKS, bd, bd), BF16)],
        compiler_params=_cparams("arbitrary", "arbitrary"),
        name="lru_core",
    )(*args)
    return y, last.reshape(n, wd)


def _pad_buf8(buf):
    return jnp.pad(buf, ((0, 0), (5, 0), (0, 0)))


def _trunk(seq, x, mod, pos, cache, p):
    n, l, m = seq.n, seq.l, seq.m
    prompt = seq.prompt
    d = x.shape[1]
    act = BF16 if prompt else F32
    new = {}
    reps = 1 if prompt else n

    h = norm_mod(seq, x, p["g_mix"], 0, mod, 0)
    rope64 = rope_tables(pos, SWA_HD, reps)
    nq = SWA_QH * SWA_HD
    q = matmul(seq, h, p["swa_w_qkv"], 0, col0=0, ncols=nq, tn=1024, out_dtype=act, epi="rope", rope=rope64,
               n_rope=nq // 1024, hd=SWA_HD, name="swa_q")
    kv = matmul(seq, h, p["swa_w_qkv"], 0, col0=nq, tn=256, out_dtype=F32, epi="rope", rope=rope64,
                n_rope=1, hd=SWA_HD, name="swa_kv")
    akw = dict(d=1, hd=SWA_HD, kvh=SWA_KVH, grp=SWA_QH // SWA_KVH, q_blk=0, k_blk=0, v_blk=1,
               sinks=p["swa_sinks"][0], want_lse=False)
    if prompt:
        o, _ = attn_prompt(n, l, q, kv, out_dtype=BF16, **akw)
        keep = min(SWA_WINDOW, l)
        new["swa_kv"] = kv.reshape(n, l, -1)[:, l - keep:].reshape(n, keep, 2, SWA_KVH, SWA_HD)
    else:
        c = cache["swa_kv"][0]
        o, _ = attn_sample(n, l, q, kv, c.reshape(n, c.shape[1], 2 * SWA_KVH, SWA_HD), **akw)
        new["swa_kv"] = kv.reshape(n, l, 2, SWA_KVH, SWA_HD)
    x = matmul(seq, o, p["swa_w_o"], 0, tn=1024, out_dtype=F32, epi="resid", res=x, mod=mod, mod_layer=0, mod_k=2,
               name="swa_o")
    x = ffn(seq, norm_mod(seq, x, p["g_ffn"], 0, mod, 3), p["w_ff1"], p["w_ff2"], 0, x, mod)

    h = norm_mod(seq, x, p["g_mix"], 1, mod, 0)
    nzx = SSM_DI + SSM_DI + 2 * SSM_G * SSM_S
    w_dt = p["ssd_w_in"][:, :, nzx:]
    dt = matmul(seq, h, w_dt, 0, tn=SSM_HEADS, out_dtype=F32, bias=p["ssd_dt_bias"], epi="softplus", name="ssd_dt")
    zx = matmul(seq, h, p["ssd_w_in"], 0, col0=0, ncols=nzx, tn=1024, out_dtype=F32, name="ssd_in")
    zx3 = zx.reshape(n, l, nzx)
    if prompt:
        buf8 = jnp.zeros((n, 8, nzx - SSM_DI), F32)
        h0 = None
        new["ssd_conv"] = zx3[:, l - 3:, SSM_DI:]
    else:
        buf8 = _pad_buf8(cache["ssd_conv"][0])
        h0 = cache["ssd"][0]
        new["ssd_conv"] = jnp.concatenate([cache["ssd_conv"][0], zx3[:, :, SSM_DI:]], axis=1)[:, -3:]
    y, new["ssd"] = ssd_core(n, l, zx, dt, buf8, h0, p, act, ng=SSM_G)
    x = matmul(seq, y, p["ssd_w_out"], 0, tn=512, out_dtype=F32, epi="resid", res=x, mod=mod,
               mod_layer=1, mod_k=2, name="ssd_out")
    x = ffn(seq, norm_mod(seq, x, p["g_ffn"], 1, mod, 3), p["w_ff1"], p["w_ff2"], 1, x, mod)

    h = norm_mod(seq, x, p["g_mix"], 2, mod, 0)
    rope128 = rope_tables(pos, DIL_HD, reps)
    ng = len(DIL_PATTERN)
    nq = ng * DIL_QH * DIL_HD
    q = matmul(seq, h, p["dil_w_qkv"], 0, col0=0, ncols=nq, tn=1024, out_dtype=F32, epi="rope", rope=rope128,
               n_rope=nq // 1024, hd=DIL_HD, name="dil_q")
    ck = DIL_KVH * DIL_HD
    grp = DIL_QH // DIL_KVH
    outs, lses = [], []
    for g, ((w, dil), key) in enumerate(zip(DIL_PATTERN, ("dil_kv_w128", "dil_kv_w512", "dil_kv_w2048"))):
        kv = matmul(seq, h, p["dil_w_qkv"], 0, col0=nq + g * ck, ncols=2 * ck, cstride=ng, tn=ck, out_dtype=F32,
                    epi="rope", rope=rope128, n_rope=1, hd=DIL_HD, name="dil_kv")
        akw = dict(d=dil, hd=DIL_HD, kvh=DIL_KVH, grp=grp, q_blk=g, k_blk=0, v_blk=1)
        if not prompt:
            c = cache[key][0]
            o, lse = attn_sample(n, l, q, kv, c.reshape(n, c.shape[1], 2 * DIL_KVH, DIL_HD), **akw)
            new[key] = kv.reshape(n, l, 2, DIL_KVH, DIL_HD)
        else:
            if dil == 1:
                o, lse = attn_prompt(n, l, q, kv, **akw)
            else:
                o, lse = attn_dil_prompt(n, l, q, kv, d=dil, kvh=DIL_KVH, grp=grp, q_head0=g * DIL_QH,
                                         k_head0=0, v_head0=DIL_KVH)
            keep = min(w, l)
            new[key] = kv.reshape(n, l, -1)[:, l - keep:].reshape(n, keep, 2, DIL_KVH, DIL_HD)
        outs.append(o)
        lses.append(lse)
    o = dil_merge(outs, lses, seq.tm if not prompt else 512, act)
    x = matmul(seq, o, p["dil_w_o"], 0, tn=1024, out_dtype=F32, epi="resid", res=x, mod=mod, mod_layer=2, mod_k=2,
               name="dil_o")
    x = ffn(seq, norm_mod(seq, x, p["g_ffn"], 2, mod, 3), p["w_ff1"], p["w_ff2"], 2, x, mod)

    h = norm_mod(seq, x, p["g_mix"], 3, mod, 0)
    wd = p["lru_w_in"].shape[2] // 2
    if prompt:
        gate = matmul(seq, h, p["lru_w_in"], 0, col0=0, ncols=wd, tn=1024, out_dtype=F32, bias=p["lru_b_in"],
                      name="lru_in_gate")
        xb, tails = matmul_conv(seq, h, p["lru_w_in"], 0, col0=wd, ncols=wd, tn=1024, bias=p["lru_b_in"],
                                conv_w=p["lru_conv_w"][0], conv_b=p["lru_conv_b"].reshape(1, wd), conv_col0=0,
                                silu=False, name="lru_in_xb")
        buf8 = tails
        h0 = None
        new["lru_conv"] = tails[:, 5:, :]
    else:
        gx = matmul(seq, h, p["lru_w_in"], 0, tn=1024, out_dtype=F32, bias=p["lru_b_in"], name="lru_in")
        gate, xb = gx[:, :wd], gx[:, wd:]
        buf8 = _pad_buf8(cache["lru_conv"][0])
        h0 = cache["lru"][0]
        new["lru_conv"] = jnp.concatenate([cache["lru_conv"][0], xb.reshape(n, l, wd)], axis=1)[:, -3:]
    y, new["lru"] = lru_core(n, l, gate, xb, buf8, h0, p, 512 if prompt else l, act, conv_done=prompt)
    x = matmul(seq, y, p["lru_w_out"], 0, tn=1024, out_dtype=F32, epi="resid", res=x, mod=mod, mod_layer=3, mod_k=2,
               name="lru_out")
    x = ffn(seq, norm_mod(seq, x, p["g_ffn"], 3, mod, 3), p["w_ff1"], p["w_ff2"], 3, x, mod)

    y = final_norm(x, p["g_final"], seq.tm)
    return y.reshape(n, l, d), {k: v[None] for k, v in new.items()}


def kernel(x_prompt, x_sample, cache_swa_kv, state_ssd_conv, state_ssd, cache_dil_kv_w128, cache_dil_kv_w512,
           cache_dil_kv_w2048, state_lru_conv, state_lru, c_prompt, c_sample, w_ada, b_ada, g_mix, g_ffn,
           w_ff1, w_ff2, g_final, swa_w_qkv, swa_sinks, swa_w_o, ssd_w_in, ssd_conv_w, ssd_conv_b,
           ssd_dt_bias, ssd_a_log, ssd_d, ssd_norm, ssd_w_out, dil_w_qkv, dil_w_o, lru_w_in, lru_b_in,
           lru_conv_w, lru_conv_b, lru_w_r, lru_b_r, lru_w_i, lru_b_i, lru_lam, lru_w_out):
    p = dict(g_mix=g_mix, g_ffn=g_ffn, w_ff1=w_ff1.astype(BF16), w_ff2=w_ff2.astype(BF16), g_final=g_final,
             swa_w_qkv=swa_w_qkv, swa_sinks=swa_sinks, swa_w_o=swa_w_o,
             ssd_w_in=ssd_w_in, ssd_conv_w=ssd_conv_w, ssd_conv_b=ssd_conv_b, ssd_dt_bias=ssd_dt_bias,
             ssd_a_log=ssd_a_log, ssd_d=ssd_d, ssd_norm=ssd_norm, ssd_w_out=ssd_w_out,
             dil_w_qkv=dil_w_qkv, dil_w_o=dil_w_o,
             lru_w_in=lru_w_in, lru_b_in=lru_b_in, lru_conv_w=lru_conv_w, lru_conv_b=lru_conv_b,
             lru_w_r=lru_w_r, lru_b_r=lru_b_r, lru_w_i=lru_w_i, lru_b_i=lru_b_i, lru_lam=lru_lam,
             lru_w_out=lru_w_out)
    cache = dict(swa_kv=cache_swa_kv, ssd_conv=state_ssd_conv, ssd=state_ssd, dil_kv_w128=cache_dil_kv_w128,
                 dil_kv_w512=cache_dil_kv_w512, dil_kv_w2048=cache_dil_kv_w2048, lru_conv=state_lru_conv,
                 lru=state_lru)
    nb, l, d = x_prompt.shape
    ns, ls, _ = x_sample.shape
    depth = w_ada.shape[0]
    rows = -(-(nb + ns) // 16) * 16
    c_all = jnp.concatenate([c_prompt, c_sample, jnp.zeros((rows - nb - ns, d), F32)], axis=0)
    mod = ada_mod(c_all, w_ada, b_ada)
    mod_p = mod.reshape(depth, rows, 1, 6 * d)
    mod_s = jnp.repeat(mod[:, nb:nb + ns], ls, axis=1).reshape(depth, 1, ns * ls, 6 * d)

    seq_p = Seq(nb, l, 1024, True)
    seq_s = Seq(ns, ls, ns * ls, False)
    y_p, sp = _trunk(seq_p, x_prompt.reshape(nb * l, d), mod_p, jnp.arange(l, dtype=jnp.int32), None, p)
    y_s, ss = _trunk(seq_s, x_sample.reshape(ns * ls, d), mod_s, PAST_LEN + jnp.arange(ls, dtype=jnp.int32), cache, p)
    return (y_p, y_s,
            sp["swa_kv"], ss["swa_kv"],
            sp["ssd_conv"], ss["ssd_conv"],
            sp["ssd"], ss["ssd"],
            sp["dil_kv_w128"], ss["dil_kv_w128"],
            sp["dil_kv_w512"], ss["dil_kv_w512"],
            sp["dil_kv_w2048"], ss["dil_kv_w2048"],
            sp["lru_conv"], ss["lru_conv"],
            sp["lru"], ss["lru"])
```

```python
import functools

import jax
import jax.numpy as jnp
from jax import lax
from jax.experimental import pallas as pl
from jax.experimental.pallas import tpu as pltpu

F32 = jnp.float32
BF16 = jnp.bfloat16
HI = lax.Precision.HIGHEST

RMS_EPS = 1e-6
ROPE_THETA = 10000.0
NEG_INF = -1e30
LANES = 128
MXU_COLS = 256
VMEM_LIMIT = 56 * 1024 * 1024

PAST_LEN = 16384
ATTN_BLOCK = 128
SWA_WINDOW, SWA_HD, SWA_QH, SWA_KVH = 128, 64, 32, 4
DIL_PATTERN = ((128, 1), (512, 4), (2048, 16))
DIL_HD, DIL_QH, DIL_KVH = 128, 16, 4
SSM_HEADS, SSM_P, SSM_S, SSM_G, SSM_HPG, SSM_CHUNK = 64, 64, 128, 8, 8, 128
SSM_DI = SSM_HEADS * SSM_P
SSM_GW = SSM_HPG * SSM_P
LRU_BLOCKS, LRU_C = 8, 8.0


def _cparams(*sem):
    return pltpu.CompilerParams(dimension_semantics=sem, vmem_limit_bytes=VMEM_LIMIT)


def _nt_dot(a, b):
    return lax.dot_general(a, b, (((1,), (1,)), ((), ())), preferred_element_type=F32)


def _sigmoid(x):
    return 0.5 * jnp.tanh(0.5 * x) + 0.5


def _tn_dot(a, b):
    return lax.dot_general(a, b, (((0,), (0,)), ((), ())), preferred_element_type=F32)


class Seq:
    def __init__(self, n, l, tm, prompt):
        self.n, self.l, self.m, self.tm, self.prompt = n, l, n * l, tm, prompt

    def mod_spec(self, layer, k, d, row_axis):
        tm, l = self.tm, self.l
        if self.prompt:
            return pl.BlockSpec((None, None, 1, d), lambda *g: (layer, (g[row_axis] * tm) // l, 0, k))
        return pl.BlockSpec((None, None, tm, d), lambda *g: (layer, 0, g[row_axis], k))


def _ada_kernel(c_ref, w_ref, b_ref, o_ref):
    c = c_ref[...]
    cond = (c * jax.nn.sigmoid(c)).astype(BF16)
    o_ref[...] = jnp.dot(cond, w_ref[...].astype(BF16), preferred_element_type=F32) + b_ref[...]


def ada_mod(c_all, w_ada, b_ada, tn=1024):
    depth, d, n6 = w_ada.shape
    r = c_all.shape[0]
    return pl.pallas_call(
        _ada_kernel,
        grid=(depth, n6 // tn),
        in_specs=[pl.BlockSpec((r, d), lambda a, j: (0, 0)),
                  pl.BlockSpec((None, d, tn), lambda a, j: (a, 0, j)),
                  pl.BlockSpec((None, 1, tn), lambda a, j: (a, 0, j))],
        out_specs=pl.BlockSpec((None, r, tn), lambda a, j: (a, 0, j)),
        out_shape=jax.ShapeDtypeStruct((depth, r, n6), F32),
        compiler_params=_cparams("arbitrary", "arbitrary"),
        name="ada_mod",
    )(c_all, w_ada, b_ada.reshape(depth, 1, n6))


def _norm_mod_kernel(x_ref, g_ref, sh_ref, sc_ref, o_ref):
    x = x_ref[...]
    y = x * lax.rsqrt(jnp.mean(x * x, axis=-1, keepdims=True) + RMS_EPS) * g_ref[...]
    o_ref[...] = (y * (1.0 + sc_ref[...]) + sh_ref[...]).astype(o_ref.dtype)


def _norm_kernel(x_ref, g_ref, o_ref):
    x = x_ref[...]
    o_ref[...] = x * lax.rsqrt(jnp.mean(x * x, axis=-1, keepdims=True) + RMS_EPS) * g_ref[...]


def norm_mod(seq, x, g, layer, mod, k_shift):
    m, d = x.shape
    tm = seq.tm
    sub = Seq(seq.n, seq.l, tm, seq.prompt)
    return pl.pallas_call(
        _norm_mod_kernel,
        grid=(m // tm,),
        in_specs=[pl.BlockSpec((tm, d), lambda i: (i, 0)),
                  pl.BlockSpec((None, 1, d), lambda i: (layer, 0, 0)),
                  sub.mod_spec(layer, k_shift, d, 0),
                  sub.mod_spec(layer, k_shift + 1, d, 0)],
        out_specs=pl.BlockSpec((tm, d), lambda i: (i, 0)),
        out_shape=jax.ShapeDtypeStruct((m, d), BF16),
        compiler_params=_cparams("arbitrary"),
        name="norm_mod",
    )(x, g.reshape(g.shape[0], 1, d), mod, mod)


def final_norm(x, g, tm):
    m, d = x.shape
    return pl.pallas_call(
        _norm_kernel,
        grid=(m // tm,),
        in_specs=[pl.BlockSpec((tm, d), lambda i: (i, 0)), pl.BlockSpec((1, d), lambda i: (0, 0))],
        out_specs=pl.BlockSpec((tm, d), lambda i: (i, 0)),
        out_shape=jax.ShapeDtypeStruct((m, d), F32),
        compiler_params=_cparams("arbitrary"),
        name="final_norm",
    )(x, g.reshape(1, d))


def _rope_cols(y, cos, sin, hd):
    outs = []
    for c in range(y.shape[1] // LANES):
        yc = y[:, c * LANES:(c + 1) * LANES]
        if hd == LANES:
            partner = pltpu.roll(yc, LANES // 2, axis=1)
        else:
            lane = lax.broadcasted_iota(jnp.int32, yc.shape, 1)
            partner = jnp.where(lane % hd < hd // 2, pltpu.roll(yc, LANES - hd // 2, axis=1),
                                pltpu.roll(yc, hd // 2, axis=1))
        outs.append(yc * cos + partner * sin)
    return outs[0] if len(outs) == 1 else jnp.concatenate(outs, axis=1)


def _mm_kernel(*refs, epi, n_rope, all_rope, hd, has_bias):
    x_ref, w_ref = refs[0], refs[1]
    o_ref, wb_ref = refs[-2], refs[-1]
    extra = refs[2:-2]

    @pl.when(pl.program_id(1) == 0)
    def _():
        wb_ref[...] = w_ref[...].astype(BF16)

    if has_bias:
        bias_ref, extra = extra[0], extra[1:]
    xb = x_ref[...].astype(BF16)
    tn = o_ref.shape[1]
    sub = min(tn, MXU_COLS)
    for c in range(tn // sub):
        cs = slice(c * sub, (c + 1) * sub)
        y = jnp.dot(xb, wb_ref[:, cs], preferred_element_type=F32)
        if has_bias:
            y = y + bias_ref[:, cs]
        if epi == "none":
            out = y
        elif epi == "softplus":
            out = jax.nn.softplus(y)
        elif epi == "resid":
            res_ref, gate_ref = extra
            out = res_ref[:, cs] + gate_ref[:, cs] * y
        elif epi == "rope":
            cos_ref, sin_ref = extra
            out = _rope_cols(y, cos_ref[...], sin_ref[...], hd)
            if not all_rope:
                out = jnp.where(pl.program_id(0) < n_rope, out, y)
        else:
            raise ValueError(epi)
        o_ref[:, cs] = out.astype(o_ref.dtype)


def matmul(seq, x, w, layer, *, col0=0, ncols=None, tn, out_dtype, tm=None, bias=None, epi="none",
           rope=None, n_rope=0, hd=LANES, res=None, mod=None, mod_layer=0, mod_k=0, cstride=1, name="mm"):
    m, k = x.shape
    ntot = w.shape[2]
    ncols = ntot - col0 if ncols is None else ncols
    tm = seq.tm if tm is None else tm
    cb = col0 // tn
    assert col0 % tn == 0 and ncols % tn == 0 and m % tm == 0
    in_specs = [pl.BlockSpec((tm, k), lambda j, i: (i, 0)),
                pl.BlockSpec((None, k, tn), lambda j, i: (layer, 0, cb + j * cstride))]
    args = [x, w]
    if bias is not None:
        in_specs.append(pl.BlockSpec((None, 1, tn), lambda j, i: (0, 0, cb + j * cstride)))
        args.append(bias.reshape(1, 1, -1))
    if epi == "rope":
        cos, sin = rope
        nrt = cos.shape[0] // tm
        for t in (cos, sin):
            in_specs.append(pl.BlockSpec((tm, LANES), lambda j, i: (i % nrt, 0)))
            args.append(t)
    if epi == "resid":
        sub = Seq(seq.n, seq.l, tm, seq.prompt)
        base = sub.mod_spec(mod_layer, mod_k, tn, 1)
        nk = w.shape[2] // tn
        gate_spec = pl.BlockSpec(base.block_shape,
                                 lambda j, i, f=base.index_map: f(j, i)[:3] + (f(j, i)[3] * nk + j,))
        in_specs += [pl.BlockSpec((tm, tn), lambda j, i: (i, j)), gate_spec]
        args += [res, mod]
    return pl.pallas_call(
        functools.partial(_mm_kernel, epi=epi, n_rope=n_rope, all_rope=n_rope >= ncols // tn, hd=hd,
                          has_bias=bias is not None),
        grid=(ncols // tn, m // tm),
        in_specs=in_specs,
        out_specs=pl.BlockSpec((tm, tn), lambda j, i: (i, j)),
        out_shape=jax.ShapeDtypeStruct((m, ncols), out_dtype),
        scratch_shapes=[pltpu.VMEM((k, tn), BF16)],
        compiler_params=_cparams("arbitrary", "arbitrary"),
        name=name,
    )(*args)


def _mm_conv_kernel(*refs, tiles_per_seq, silu, has_bias):
    x_ref, w_ref = refs[0], refs[1]
    k = 3 if has_bias else 2
    cw_ref, cb_ref = refs[k], refs[k + 1]
    o_ref, tails_ref, wb_ref, xp_ref, tail_ref = refs[k + 2:]
    i = pl.program_id(1)

    @pl.when(i == 0)
    def _():
        wb_ref[...] = w_ref[...].astype(BF16)

    @pl.when(i % tiles_per_seq == 0)
    def _():
        tail_ref[...] = jnp.zeros(tail_ref.shape, F32)

    xb = x_ref[...].astype(BF16)
    tm, tn = o_ref.shape
    sub = min(tn, MXU_COLS)
    for c in range(tn // sub):
        cs = slice(c * sub, (c + 1) * sub)
        y = jnp.dot(xb, wb_ref[:, cs], preferred_element_type=F32)
        if has_bias:
            y = y + refs[2][:, cs]
        xp_ref[c, 0:8, :] = tail_ref[:, cs]
        xp_ref[c, 8:8 + tm, :] = y
        acc = cb_ref[:, cs]
        for t in range(4):
            acc = acc + xp_ref[c, 5 + t:5 + t + tm, :] * cw_ref[t:t + 1, cs]
        last = y[tm - 8:tm, :]
        tail_ref[:, cs] = last
        tails_ref[:, cs] = last
        o_ref[:, cs] = (acc * _sigmoid(acc) if silu else acc).astype(o_ref.dtype)


def matmul_conv(seq, x, w, layer, *, col0, ncols, tn, conv_w, conv_b, conv_col0, silu, bias=None, name="mm_conv"):
    m, k = x.shape
    tm = seq.tm
    assert seq.prompt and seq.l % tm == 0 and col0 % tn == 0 and ncols % tn == 0 and conv_col0 % tn == 0
    cb, ccb, tps = col0 // tn, conv_col0 // tn, seq.l // tm
    sub = min(tn, MXU_COLS)
    in_specs = [pl.BlockSpec((tm, k), lambda j, i: (i, 0)),
                pl.BlockSpec((None, k, tn), lambda j, i: (layer, 0, cb + j))]
    args = [x, w]
    if bias is not None:
        in_specs.append(pl.BlockSpec((None, 1, tn), lambda j, i: (0, 0, cb + j)))
        args.append(bias.reshape(1, 1, -1))
    in_specs += [pl.BlockSpec((4, tn), lambda j, i: (0, ccb + j)), pl.BlockSpec((1, tn), lambda j, i: (0, ccb + j))]
    args += [conv_w, conv_b]
    return pl.pallas_call(
        functools.partial(_mm_conv_kernel, tiles_per_seq=tps, silu=silu, has_bias=bias is not None),
        grid=(ncols // tn, m // tm),
        in_specs=in_specs,
        out_specs=[pl.BlockSpec((tm, tn), lambda j, i: (i, j)),
                   pl.BlockSpec((None, 8, tn), lambda j, i: (i // tps, 0, j))],
        out_shape=[jax.ShapeDtypeStruct((m, ncols), F32), jax.ShapeDtypeStruct((seq.n, 8, ncols), F32)],
        scratch_shapes=[pltpu.VMEM((k, tn), BF16), pltpu.VMEM((tn // sub, 8 + tm, sub), F32),
                        pltpu.VMEM((8, tn), F32)],
        compiler_params=_cparams("arbitrary", "arbitrary"),
        name=name,
    )(*args)


def _ffn_kernel(x_ref, w1_ref, w2_ref, res_ref, gate_ref, o_ref, h_ref, *, na, tf):
    s = pl.program_id(1)

    @pl.when(s < na)
    def _():
        h = jnp.dot(x_ref[...], w1_ref[...], preferred_element_type=F32)
        col = pl.multiple_of(s * tf, tf)
        h_ref[:, pl.ds(col, tf)] = jnp.square(jnp.maximum(h, 0.0)).astype(BF16)

    @pl.when(s >= na)
    def _():
        y = jnp.dot(h_ref[...], w2_ref[...], preferred_element_type=F32)
        o_ref[...] = res_ref[...] + gate_ref[...] * y


def _ffn_tiles(tm, d, f):
    for tf, tn in ((2048, 512), (1024, 512), (1024, 256), (512, 256)):
        windows = 2 * 2 * (tm * d + d * tf + f * tn) + 2 * 2 * 4 * tm * tn
        if windows + 2 * tm * f + 6 * tm * tf <= 0.9 * VMEM_LIMIT:
            return tf, tn
    raise ValueError("FFN row tile too large for VMEM")


def ffn(seq, h, w1, w2, layer, res, mod):
    m, d = h.shape
    f = w1.shape[2]
    tm = seq.tm
    tf, tn = _ffn_tiles(tm, d, f)
    na, nb = f // tf, d // tn
    sub = Seq(seq.n, seq.l, tm, seq.prompt)
    base = sub.mod_spec(layer, 5, tn, 0)
    ocol = lambda s: jnp.maximum(s - na, 0)
    gate_spec = pl.BlockSpec(base.block_shape,
                             lambda i, s, fm=base.index_map: fm(i, s)[:3] + (5 * nb + ocol(s),))
    return pl.pallas_call(
        functools.partial(_ffn_kernel, na=na, tf=tf),
        grid=(m // tm, na + nb),
        in_specs=[pl.BlockSpec((tm, d), lambda i, s: (i, 0)),
                  pl.BlockSpec((None, d, tf), lambda i, s: (layer, 0, jnp.minimum(s, na - 1))),
                  pl.BlockSpec((None, f, tn), lambda i, s: (layer, 0, ocol(s))),
                  pl.BlockSpec((tm, tn), lambda i, s: (i, ocol(s))),
                  gate_spec],
        out_specs=pl.BlockSpec((tm, tn), lambda i, s: (i, ocol(s))),
        out_shape=jax.ShapeDtypeStruct((m, d), F32),
        scratch_shapes=[pltpu.VMEM((tm, f), BF16)],
        compiler_params=_cparams("arbitrary", "arbitrary"),
        name="ffn",
    )(h, w1, w2, res, mod)


def rope_tables(pos, hd, reps):
    half = hd // 2
    inv = ROPE_THETA ** (-jnp.arange(half, dtype=F32) / half)
    ang = pos.astype(F32)[:, None] * inv[None, :]
    cos = jnp.concatenate([jnp.cos(ang), jnp.cos(ang)], axis=-1)
    sin = jnp.concatenate([-jnp.sin(ang), jnp.sin(ang)], axis=-1)
    lane_reps = LANES // hd
    return jnp.tile(cos, (reps, lane_reps)), jnp.tile(sin, (reps, lane_reps))


def _attn_prompt_kernel(*refs, hd, kvh, grp, scale, has_sinks, want_lse):
    if has_sinks:
        sink_ref, refs = refs[0], refs[1:]
    q_ref, kp_ref, kc_ref, vp_ref, vc_ref = refs[:5]
    o_ref = refs[5]
    lse_ref = refs[6] if want_lse else None
    ub = pl.program_id(2)
    bq = q_ref.shape[0]
    rows = grp * bq
    iq = lax.broadcasted_iota(jnp.int32, (rows, 2 * bq), 0) % bq
    jk = lax.broadcasted_iota(jnp.int32, (rows, 2 * bq), 1)
    mask = (jk >= iq) & (jk <= iq + bq) & ((jk >= bq) | (ub > 0))
    rcol = lax.broadcasted_iota(jnp.int32, (rows, 1), 0)
    lane = lax.broadcasted_iota(jnp.int32, (bq, LANES), 1)
    lse_tile = jnp.zeros((bq, LANES), F32)
    for kh in range(kvh):
        ksl = slice(kh * hd, (kh + 1) * hd)
        kk = jnp.concatenate([kp_ref[:, ksl], kc_ref[:, ksl]], axis=0).astype(BF16)
        vv = jnp.concatenate([vp_ref[:, ksl], vc_ref[:, ksl]], axis=0).astype(BF16)
        qs = jnp.concatenate([q_ref[:, (kh * grp + g) * hd:(kh * grp + g + 1) * hd] for g in range(grp)], axis=0)
        s = jnp.where(mask, _nt_dot(qs.astype(BF16), kk) * scale, NEG_INF)
        mx = jnp.max(s, axis=-1, keepdims=True)
        if has_sinks:
            sk = jnp.zeros((rows, 1), F32)
            for g in range(grp):
                sk = jnp.where(rcol // bq == g, sink_ref[kh * grp + g], sk)
            mx = jnp.maximum(mx, sk)
        p = jnp.exp(s - mx)
        den = jnp.sum(p, axis=-1, keepdims=True)
        if has_sinks:
            den = den + jnp.exp(sk - mx)
        o = jnp.dot(p.astype(BF16), vv, preferred_element_type=F32) / den
        lse = mx + jnp.log(den)
        for g in range(grp):
            h = kh * grp + g
            o_ref[:, h * hd:(h + 1) * hd] = o[g * bq:(g + 1) * bq, :].astype(o_ref.dtype)
            if want_lse:
                lse_tile = jnp.where(lane == h, lse[g * bq:(g + 1) * bq, :], lse_tile)
    if want_lse:
        lse_ref[...] = lse_tile


def attn_prompt(n, l, q, kv, *, d, hd, kvh, grp, q_blk, k_blk, v_blk, sinks=None, want_lse=True, out_dtype=F32):
    bq = ATTN_BLOCK
    cq, ck = grp * kvh * hd, kvh * hd
    nq, nk = q.shape[1] // cq, kv.shape[1] // ck
    lu = l // d
    qv = q.reshape(n, lu, d * q.shape[1])
    kvv = kv.reshape(n, lu, d * kv.shape[1])
    prev = lambda u: jnp.maximum(u - 1, 0)
    in_specs = [pl.BlockSpec((None, bq, cq), lambda b, r, u: (b, u, r * nq + q_blk)),
                pl.BlockSpec((None, bq, ck), lambda b, r, u: (b, prev(u), r * nk + k_blk)),
                pl.BlockSpec((None, bq, ck), lambda b, r, u: (b, u, r * nk + k_blk)),
                pl.BlockSpec((None, bq, ck), lambda b, r, u: (b, prev(u), r * nk + v_blk)),
                pl.BlockSpec((None, bq, ck), lambda b, r, u: (b, u, r * nk + v_blk))]
    args = [qv, kvv, kvv, kvv, kvv]
    if sinks is not None:
        in_specs.insert(0, pl.BlockSpec(memory_space=pltpu.SMEM))
        args.insert(0, sinks)
    out_shape = [jax.ShapeDtypeStruct((n, lu, d * cq), out_dtype)]
    out_specs = [pl.BlockSpec((None, bq, cq), lambda b, r, u: (b, u, r))]
    if want_lse:
        out_shape.append(jax.ShapeDtypeStruct((n, lu, d * LANES), F32))
        out_specs.append(pl.BlockSpec((None, bq, LANES), lambda b, r, u: (b, u, r)))
    outs = pl.pallas_call(
        functools.partial(_attn_prompt_kernel, hd=hd, kvh=kvh, grp=grp, scale=hd ** -0.5,
                          has_sinks=sinks is not None, want_lse=want_lse),
        grid=(n, d, lu // bq),
        in_specs=in_specs,
        out_specs=out_specs,
        out_shape=out_shape,
        compiler_params=_cparams("arbitrary", "arbitrary", "arbitrary"),
        name=f"attn_prompt_d{d}",
    )(*args)
    o = outs[0].reshape(n * l, cq)
    return (o, outs[1].reshape(n * l, LANES)) if want_lse else (o, None)


def _attn_dil_kernel(*refs, d, grp, scale, has_prev):
    q_refs = refs[:grp]
    kp_ref, kc_ref, vp_ref, vc_ref = refs[grp:grp + 4]
    o_refs = refs[grp + 4:2 * grp + 4]
    lse_ref = refs[2 * grp + 4]
    ub, kh = pl.program_id(1), pl.program_id(2)
    bq = ATTN_BLOCK
    nk = 2 * bq if has_prev else bq
    iq = lax.broadcasted_iota(jnp.int32, (grp * bq, nk), 0) % bq
    jk = lax.broadcasted_iota(jnp.int32, (grp * bq, nk), 1)
    if has_prev:
        mask = (jk >= iq) & (jk <= iq + bq) & ((jk >= bq) | (ub > 0))
    else:
        mask = jk <= iq
    lane = lax.broadcasted_iota(jnp.int32, (bq, LANES), 1)

    @pl.when(kh == 0)
    def _():
        lse_ref[...] = jnp.zeros(lse_ref.shape, F32)

    for r in range(d):
        rows = pl.ds(r, bq, stride=d)
        qs = jnp.concatenate([qr[rows, :] for qr in q_refs], axis=0).astype(BF16)
        if has_prev:
            kk = jnp.concatenate([kp_ref[rows, :], kc_ref[rows, :]], axis=0).astype(BF16)
            vv = jnp.concatenate([vp_ref[rows, :], vc_ref[rows, :]], axis=0).astype(BF16)
        else:
            kk = kc_ref[rows, :].astype(BF16)
            vv = vc_ref[rows, :].astype(BF16)
        s = jnp.where(mask, _nt_dot(qs, kk) * scale, NEG_INF)
        mx = jnp.max(s, axis=-1, keepdims=True)
        p = jnp.exp(s - mx)
        den = jnp.sum(p, axis=-1, keepdims=True)
        o = jnp.dot(p.astype(BF16), vv, preferred_element_type=F32) / den
        lse = mx + jnp.log(den)
        tile = lse_ref[r * bq:(r + 1) * bq, :]
        for g in range(grp):
            o_refs[g][rows, :] = o[g * bq:(g + 1) * bq, :]
            tile = jnp.where(lane == kh * grp + g, lse[g * bq:(g + 1) * bq, :], tile)
        lse_ref[r * bq:(r + 1) * bq, :] = tile


def attn_dil_prompt(n, l, q, kv, *, d, kvh, grp, q_head0, k_head0, v_head0):
    hd = LANES
    rt = d * ATTN_BLOCK
    nub = l // rt
    has_prev = nub > 1
    prev = lambda u: jnp.maximum(u - 1, 0)
    cur = lambda u: u
    blk = lambda col, rowf: pl.BlockSpec((rt, hd), lambda b, u, kh: (b * nub + rowf(u), col(kh)))
    q_specs = [blk(lambda kh, g=g: q_head0 + kh * grp + g, cur) for g in range(grp)]
    outs = pl.pallas_call(
        functools.partial(_attn_dil_kernel, d=d, grp=grp, scale=hd ** -0.5, has_prev=has_prev),
        grid=(n, nub, kvh),
        in_specs=q_specs + [blk(lambda kh: k_head0 + kh, prev), blk(lambda kh: k_head0 + kh, cur),
                            blk(lambda kh: v_head0 + kh, prev), blk(lambda kh: v_head0 + kh, cur)],
        out_specs=[blk(lambda kh: kh, cur)] * grp + [blk(lambda kh: 0, cur)],
        out_shape=[jax.ShapeDtypeStruct((n * l, kvh * hd), F32)] * grp + [jax.ShapeDtypeStruct((n * l, LANES), F32)],
        compiler_params=_cparams("arbitrary", "arbitrary", "arbitrary"),
        name=f"attn_dil_d{d}",
    )(*([q] * grp), kv, kv, kv, kv)
    lse = outs[grp].reshape(n * nub, d, ATTN_BLOCK, LANES).transpose(0, 2, 1, 3).reshape(n * l, LANES)
    return list(outs[:grp]), lse


def _attn_sample_kernel(*refs, hd, kvh, grp, scale, d, has_sinks, want_lse):
    if has_sinks:
        sink_ref, refs = refs[0], refs[1:]
    q_ref, kn_ref, vn_ref, c_ref = refs[:4]
    o_ref = refs[4]
    lse_ref = refs[5] if want_lse else None
    nb = c_ref.shape[0]
    lq = q_ref.shape[0] // nb
    flat = len(c_ref.shape) == 3
    w = c_ref.shape[1] // (2 * kvh) if flat else c_ref.shape[1]

    def cache_head(bi, idx):
        return c_ref[bi, pl.ds(idx, w, stride=2 * kvh), :] if flat else c_ref[bi, :, idx, :]

    rows = grp * lq
    row = lax.broadcasted_iota(jnp.int32, (rows, w + lq), 0)
    jk = lax.broadcasted_iota(jnp.int32, (rows, w + lq), 1)
    dist = row % lq + w - jk
    mask = (dist >= 0) & (dist <= w) & (dist % d == 0)
    rcol = lax.broadcasted_iota(jnp.int32, (rows, 1), 0)
    lane = lax.broadcasted_iota(jnp.int32, (lq, LANES), 1)
    for bi in range(nb):
        rsl = slice(bi * lq, (bi + 1) * lq)
        lse_tile = jnp.zeros((lq, LANES), F32)
        for kh in range(kvh):
            ksl = slice(kh * hd, (kh + 1) * hd)
            kk = jnp.concatenate([cache_head(bi, kh), kn_ref[rsl, ksl]], axis=0).astype(BF16)
            vv = jnp.concatenate([cache_head(bi, kvh + kh), vn_ref[rsl, ksl]], axis=0).astype(BF16)
            qs = jnp.concatenate([q_ref[rsl, (kh * grp + g) * hd:(kh * grp + g + 1) * hd] for g in range(grp)],
                                 axis=0)
            s = jnp.where(mask, _nt_dot(qs.astype(BF16), kk) * scale, NEG_INF)
            mx = jnp.max(s, axis=-1, keepdims=True)
            if has_sinks:
                sk = jnp.zeros((rows, 1), F32)
                for g in range(grp):
                    sk = jnp.where(rcol // lq == g, sink_ref[kh * grp + g], sk)
                mx = jnp.maximum(mx, sk)
            p = jnp.exp(s - mx)
            den = jnp.sum(p, axis=-1, keepdims=True)
            if has_sinks:
                den = den + jnp.exp(sk - mx)
            o = jnp.dot(p.astype(BF16), vv, preferred_element_type=F32) / den
            lse = mx + jnp.log(den)
            for g in range(grp):
                h = kh * grp + g
                o_ref[rsl, h * hd:(h + 1) * hd] = o[g * lq:(g + 1) * lq, :]
                if want_lse:
                    lse_tile = jnp.where(lane == h, lse[g * lq:(g + 1) * lq, :], lse_tile)
        if want_lse:
            lse_ref[rsl, :] = lse_tile


def attn_sample(n, l, q, kv, cache, *, d, hd, kvh, grp, q_blk, k_blk, v_blk, sinks=None, want_lse=True):
    cq, ck = grp * kvh * hd, kvh * hd
    w = cache.shape[1]
    nb = max(b for b in (4, 2, 1) if n % b == 0 and b * w <= 4096)
    if hd == LANES:
        cache = cache.reshape(n, w * 2 * kvh, hd)
        cache_spec = pl.BlockSpec((nb, w * 2 * kvh, hd), lambda b: (b, 0, 0))
    else:
        cache_spec = pl.BlockSpec((nb, w, 2 * kvh, hd), lambda b: (b, 0, 0, 0))
    l, n = nb * l, n // nb
    in_specs = [pl.BlockSpec((l, cq), lambda b: (b, q_blk)),
                pl.BlockSpec((l, ck), lambda b: (b, k_blk)),
                pl.BlockSpec((l, ck), lambda b: (b, v_blk)),
                cache_spec]
    args = [q, kv, kv, cache]
    if sinks is not None:
        in_specs.insert(0, pl.BlockSpec(memory_space=pltpu.SMEM))
        args.insert(0, sinks)
    out_shape = [jax.ShapeDtypeStruct((n * l, cq), F32)]
    out_specs = [pl.BlockSpec((l, cq), lambda b: (b, 0))]
    if want_lse:
        out_shape.append(jax.ShapeDtypeStruct((n * l, LANES), F32))
        out_specs.append(pl.BlockSpec((l, LANES), lambda b: (b, 0)))
    outs = pl.pallas_call(
        functools.partial(_attn_sample_kernel, hd=hd, kvh=kvh, grp=grp, scale=hd ** -0.5, d=d,
                          has_sinks=sinks is not None, want_lse=want_lse),
        grid=(n,),
        in_specs=in_specs,
        out_specs=out_specs,
        out_shape=out_shape,
        compiler_params=_cparams("arbitrary"),
        name=f"attn_sample_d{d}",
    )(*args)
    return (outs[0], outs[1]) if want_lse else (outs[0], None)


def _dil_merge_kernel(*refs, counts, qh, hd):
    no = sum(counts)
    o_refs, l_refs, out_ref = refs[:no], refs[no:no + len(counts)], refs[-1]
    ls = [r[...] for r in l_refs]
    mx = functools.reduce(jnp.maximum, ls)
    es = [jnp.exp(v - mx) for v in ls]
    tot = functools.reduce(lambda a, b: a + b, es)
    wts = [e / tot for e in es]
    for h in range(qh):
        acc, first = None, 0
        for cnt, wt in zip(counts, wts):
            if cnt == 1:
                piece = o_refs[first][:, h * hd:(h + 1) * hd]
            else:
                piece = o_refs[first + h % cnt][:, (h // cnt) * hd:(h // cnt + 1) * hd]
            first += cnt
            term = wt[:, h:h + 1] * piece
            acc = term if acc is None else acc + term
        out_ref[:, h * hd:(h + 1) * hd] = acc.astype(out_ref.dtype)


def dil_merge(outs, lses, tm, out_dtype):
    groups = [o if isinstance(o, (list, tuple)) else [o] for o in outs]
    flat = [a for grp_arrays in groups for a in grp_arrays]
    m = flat[0].shape[0]
    c = DIL_QH * DIL_HD
    return pl.pallas_call(
        functools.partial(_dil_merge_kernel, counts=tuple(len(g) for g in groups), qh=DIL_QH, hd=DIL_HD),
        grid=(m // tm,),
        in_specs=[pl.BlockSpec((tm, a.shape[1]), lambda i: (i, 0)) for a in flat]
        + [pl.BlockSpec((tm, LANES), lambda i: (i, 0))] * len(lses),
        out_specs=pl.BlockSpec((tm, c), lambda i: (i, 0)),
        out_shape=jax.ShapeDtypeStruct((m, c), out_dtype),
        compiler_params=_cparams("arbitrary"),
        name="dil_merge",
    )(*flat, *lses)


def _conv_silu(xp_ref, tail_ref, x_ref, w_ref, b_ref, q):
    xp_ref[0:8, :] = tail_ref[...]
    xp_ref[8:8 + q, :] = x_ref[...]
    y = b_ref[...]
    for i in range(4):
        y = y + xp_ref[5 + i:5 + i + q, :] * w_ref[i:i + 1, :]
    tail_ref[...] = xp_ref[q:q + 8, :]
    return y * _sigmoid(y)


def _ssd_kernel(*refs, q, ng, has_state):
    (z_ref, x_ref, b_ref, c_ref, bufx_ref, bufb_ref, bufc_ref, dtc_ref, dtr_ref, alr_ref, alc_ref, dpar_ref,
     cwx_ref, cwb_ref, cwc_ref, cbx_ref, cbb_ref, cbc_ref, nw_ref) = refs[:19]
    refs = refs[19:]
    if has_state:
        h0_ref, refs = refs[0], refs[1:]
    y_ref, st_ref, xpx, xpb, xpc, tlx, tlb, tlc, state = refs
    c = pl.program_id(2)
    gw, s = SSM_GW, SSM_S

    @pl.when(c == 0)
    def _():
        tlx[...] = bufx_ref[...]
        tlb[...] = bufb_ref[...]
        tlc[...] = bufc_ref[...]
        state[...] = h0_ref[...] if has_state else jnp.zeros(state.shape, F32)

    xs_all = _conv_silu(xpx, tlx, x_ref, cwx_ref, cbx_ref, q)
    bm_all = _conv_silu(xpb, tlb, b_ref, cwb_ref, cbb_ref, q)
    cm_all = _conv_silu(xpc, tlc, c_ref, cwc_ref, cbc_ref, q)

    li = lax.broadcasted_iota(jnp.int32, (q, q), 0)
    mi = lax.broadcasted_iota(jnp.int32, (q, q), 1)
    causal = li >= mi
    tri_c = causal.astype(F32)
    tri_r = (li <= mi).astype(F32)
    low = lax.broadcasted_iota(jnp.int32, (q, LANES), 1) < SSM_P
    on_mxu = q % LANES == 0

    def onehot(width, block):
        r = lax.broadcasted_iota(jnp.int32, (SSM_HPG, width), 0)
        c = lax.broadcasted_iota(jnp.int32, (SSM_HPG, width), 1)
        return (c // block == r).astype(BF16)

    e_chan = onehot(gw, SSM_P)
    e_time = onehot(SSM_HPG * q, q) if on_mxu else None

    def expand(v, e):
        hi = v.astype(BF16)
        r1 = v - hi.astype(F32)
        mid = r1.astype(BF16)
        lo = (r1 - mid.astype(F32)).astype(BF16)
        return (jnp.dot(hi, e, preferred_element_type=F32) + jnp.dot(mid, e, preferred_element_type=F32)
                + jnp.dot(lo, e, preferred_element_type=F32))

    def widen(v):
        if on_mxu:
            return expand(v, e_chan)
        return jnp.concatenate([jnp.where(low, v[:, 2 * j:2 * j + 1], v[:, 2 * j + 1:2 * j + 2])
                                for j in range(gw // LANES)], axis=1)

    for k in range(ng):
        xs = xs_all[:, k * gw:(k + 1) * gw]
        bmb = bm_all[:, k * s:(k + 1) * s].astype(BF16)
        cmb = cm_all[:, k * s:(k + 1) * s].astype(BF16)
        dt_c = dtc_ref[k]
        dt_r = dtr_ref[k]
        a_r = -jnp.exp(alr_ref[k])
        a_c = -jnp.exp(alc_ref[k])
        cs_c = jnp.dot(tri_c, dt_c * a_r, precision=HI, preferred_element_type=F32)
        cs_r = jnp.dot(dt_r * a_c, tri_r, precision=HI, preferred_element_type=F32)
        cs_last = cs_c[q - 1:q, :]

        tend, ecs = jnp.exp(cs_last - cs_c), jnp.exp(cs_c)
        if on_mxu:
            wide = widen(jnp.concatenate([dt_c, tend, ecs], axis=0))
            dtw, tendw, ecsw = wide[:q], wide[q:2 * q], wide[2 * q:]
            segc = expand(cs_c, e_time)
        else:
            dtw, tendw, ecsw = widen(dt_c), widen(tend), widen(ecs)
        xdt = xs * dtw
        xe = (xdt * tendw).astype(BF16)
        xdtb = xdt.astype(BF16)
        cb = _nt_dot(cmb, bmb)
        st = state[k]
        y_off = _nt_dot(cmb, st.astype(BF16)) * ecsw
        y_diag = []
        for j in range(gw // LANES):
            pair = []
            for h in (2 * j, 2 * j + 1):
                col = segc[:, h * q:(h + 1) * q] if on_mxu else cs_c[:, h:h + 1]
                seg = col - cs_r[h:h + 1, :]
                gm = (cb * jnp.exp(jnp.where(causal, seg, NEG_INF))).astype(BF16)
                pair.append(jnp.dot(gm, xdtb[:, j * LANES:(j + 1) * LANES], preferred_element_type=F32))
            y_diag.append(jnp.where(low, pair[0], pair[1]))
        y = jnp.concatenate(y_diag, axis=1) + y_off + xs * dpar_ref[:, k * gw:(k + 1) * gw]
        new_st = _tn_dot(xe, bmb)
        dec_last = jnp.exp(cs_r[:, q - 1:q])
        st = jnp.concatenate([st[h * SSM_P:(h + 1) * SSM_P, :] * dec_last[h:h + 1, :] for h in range(SSM_HPG)],
                             axis=0) + new_st
        state[k] = st
        st_ref[k] = st

        z = z_ref[:, k * gw:(k + 1) * gw]
        y = y * (z * _sigmoid(z))
        y = y * lax.rsqrt(jnp.mean(y * y, axis=-1, keepdims=True) + RMS_EPS) * nw_ref[:, k * gw:(k + 1) * gw]
        y_ref[:, k * gw:(k + 1) * gw] = y.astype(y_ref.dtype)


def ssd_core(n, l, zx, dt, conv_buf8, h0, p, out_dtype, ng):
    q = SSM_CHUNK if l % SSM_CHUNK == 0 else l
    nc = l // q
    g = SSM_G
    gw, s = ng * SSM_GW, ng * SSM_S
    xb0 = SSM_DI // gw
    bb0 = 2 * SSM_DI // s
    cb0 = bb0 + g // ng
    kb0 = SSM_DI // s
    kc0 = kb0 + g // ng
    dt4 = dt.reshape(n, l, g, SSM_HPG)
    dt_c = jnp.transpose(dt4, (0, 2, 1, 3))
    dt_r = jnp.transpose(dt4, (0, 2, 3, 1))
    alog = p["ssd_a_log"].reshape(g, SSM_HPG)
    row = lambda b, gi, c: (b * nc + c)
    in_specs = [
        pl.BlockSpec((q, gw), lambda b, gi, c: (row(b, gi, c), gi)),
        pl.BlockSpec((q, gw), lambda b, gi, c: (row(b, gi, c), xb0 + gi)),
        pl.BlockSpec((q, s), lambda b, gi, c: (row(b, gi, c), bb0 + gi)),
        pl.BlockSpec((q, s), lambda b, gi, c: (row(b, gi, c), cb0 + gi)),
        pl.BlockSpec((None, 8, gw), lambda b, gi, c: (b, 0, gi)),
        pl.BlockSpec((None, 8, s), lambda b, gi, c: (b, 0, kb0 + gi)),
        pl.BlockSpec((None, 8, s), lambda b, gi, c: (b, 0, kc0 + gi)),
        pl.BlockSpec((None, ng, q, SSM_HPG), lambda b, gi, c: (b, gi, c, 0)),
        pl.BlockSpec((None, ng, SSM_HPG, q), lambda b, gi, c: (b, gi, 0, c)),
        pl.BlockSpec((ng, 1, SSM_HPG), lambda b, gi, c: (gi, 0, 0)),
        pl.BlockSpec((ng, SSM_HPG, 1), lambda b, gi, c: (gi, 0, 0)),
        pl.BlockSpec((1, gw), lambda b, gi, c: (0, gi)),
        pl.BlockSpec((4, gw), lambda b, gi, c: (0, gi)),
        pl.BlockSpec((4, s), lambda b, gi, c: (0, kb0 + gi)),
        pl.BlockSpec((4, s), lambda b, gi, c: (0, kc0 + gi)),
        pl.BlockSpec((1, gw), lambda b, gi, c: (0, gi)),
        pl.BlockSpec((1, s), lambda b, gi, c: (0, kb0 + gi)),
        pl.BlockSpec((1, s), lambda b, gi, c: (0, kc0 + gi)),
        pl.BlockSpec((1, gw), lambda b, gi, c: (0, gi)),
    ]
    cw = p["ssd_conv_w"][0]
    cbias = p["ssd_conv_b"]
    args = [zx, zx, zx, zx, conv_buf8, conv_buf8, conv_buf8, dt_c, dt_r,
            alog.reshape(g, 1, SSM_HPG), alog.reshape(g, SSM_HPG, 1),
            jnp.repeat(p["ssd_d"].reshape(-1), SSM_P).reshape(1, SSM_DI),
            cw, cw, cw, cbias, cbias, cbias, p["ssd_norm"]]
    st_spec = pl.BlockSpec((None, ng, SSM_GW, SSM_S), lambda b, gi, c: (b, gi, 0, 0))
    if h0 is not None:
        in_specs.append(st_spec)
        args.append(h0.reshape(n, g, SSM_GW, SSM_S))
    y, st = pl.pallas_call(
        functools.partial(_ssd_kernel, q=q, ng=ng, has_state=h0 is not None),
        grid=(n, g // ng, nc),
        in_specs=in_specs,
        out_specs=[pl.BlockSpec((q, gw), lambda b, gi, c: (row(b, gi, c), gi)), st_spec],
        out_shape=[jax.ShapeDtypeStruct((n * l, SSM_DI), out_dtype),
                   jax.ShapeDtypeStruct((n, g, SSM_GW, SSM_S), F32)],
        scratch_shapes=[pltpu.VMEM((8 + q, gw), F32), pltpu.VMEM((8 + q, s), F32), pltpu.VMEM((8 + q, s), F32),
                        pltpu.VMEM((8, gw), F32), pltpu.VMEM((8, s), F32), pltpu.VMEM((8, s), F32),
                        pltpu.VMEM((ng, SSM_GW, SSM_S), F32)],
        compiler_params=_cparams("arbitrary", "arbitrary", "arbitrary"),
        name="ssd_core",
    )(*args)
    return y, st.reshape(n, SSM_HEADS, SSM_P, SSM_S)


def _lru_kernel(*refs, tl, has_state, conv_done):
    (gate_ref, xb_ref, buf_ref, cw_ref, cb_ref, wr_ref, wi_ref, br_ref, bi_ref, lam_ref) = refs[:10]
    refs = refs[10:]
    if has_state:
        h0_ref, refs = refs[0], refs[1:]
    y_ref, last_ref, xp, tail, a_sc, u_sc, hs_sc, hcar, wrb, wib = refs
    t = pl.program_id(1)

    @pl.when((pl.program_id(0) == 0) & (t == 0))
    def _():
        wrb[...] = wr_ref[...].astype(BF16)
        wib[...] = wi_ref[...].astype(BF16)

    @pl.when(t == 0)
    def _():
        if not conv_done:
            tail[...] = buf_ref[...]
        hcar[...] = h0_ref[...] if has_state else jnp.zeros(hcar.shape, F32)

    if conv_done:
        xc = xb_ref[...]
    else:
        xp[0:8, :] = tail[...]
        xp[8:8 + tl, :] = xb_ref[...]
        xc = cb_ref[...]
        for i in range(4):
            xc = xc + xp[5 + i:5 + i + tl, :] * cw_ref[i:i + 1, :]
        tail[...] = xp[tl:tl + 8, :]

    xcb = xc.astype(BF16)
    bd = xc.shape[1] // LRU_BLOCKS
    rs, is_ = [], []
    for b in range(LRU_BLOCKS):
        xblk = xcb[:, b * bd:(b + 1) * bd]
        rs.append(jnp.dot(xblk, wrb[b], preferred_element_type=F32))
        is_.append(jnp.dot(xblk, wib[b], preferred_element_type=F32))
    r = _sigmoid(jnp.concatenate(rs, axis=1) + br_ref[...])
    ig = _sigmoid(jnp.concatenate(is_, axis=1) + bi_ref[...])
    log_a = -LRU_C * r * jax.nn.softplus(-lam_ref[...])
    a = jnp.exp(log_a)
    a_sc[...] = a
    u_sc[...] = jnp.sqrt(-jnp.tanh(log_a) * (a * a + 1.0)) * (ig * xc)

    def step(i, h):
        h = a_sc[pl.ds(i, 1), :] * h + u_sc[pl.ds(i, 1), :]
        hs_sc[pl.ds(i, 1), :] = h
        return h

    h = lax.fori_loop(0, tl, step, hcar[...], unroll=8)
    hcar[...] = h
    last_ref[...] = h
    y_ref[...] = (hs_sc[...] * jax.nn.gelu(gate_ref[...])).astype(y_ref.dtype)


def lru_core(n, l, gate, xb, conv_buf8, h0, p, tl, out_dtype, conv_done):
    wd = gate.shape[1]
    nt = l // tl
    bd = wd // LRU_BLOCKS
    vec = lambda a: a.reshape(1, wd)
    cst2 = lambda b, t: (0, 0)
    in_specs = [pl.BlockSpec((tl, wd), lambda b, t: (b * nt + t, 0)),
                pl.BlockSpec((tl, wd), lambda b, t: (b * nt + t, 0)),
                pl.BlockSpec((None, 8, wd), lambda b, t: (b, 0, 0)),
                pl.BlockSpec((4, wd), cst2), pl.BlockSpec((1, wd), cst2),
                pl.BlockSpec((LRU_BLOCKS, bd, bd), lambda b, t: (0, 0, 0)),
                pl.BlockSpec((LRU_BLOCKS, bd, bd), lambda b, t: (0, 0, 0)),
                pl.BlockSpec((1, wd), cst2), pl.BlockSpec((1, wd), cst2), pl.BlockSpec((1, wd), cst2)]
    args = [gate, xb, conv_buf8, p["lru_conv_w"][0], vec(p["lru_conv_b"]), p["lru_w_r"][0], p["lru_w_i"][0],
            vec(p["lru_b_r"]), vec(p["lru_b_i"]), vec(p["lru_lam"])]
    if h0 is not None:
        in_specs.append(pl.BlockSpec((None, 1, wd), lambda b, t: (b, 0, 0)))
        args.append(h0.reshape(n, 1, wd))
    y, last = pl.pallas_call(
        functools.partial(_lru_kernel, tl=tl, has_state=h0 is not None, conv_done=conv_done),
        grid=(n, nt),
        in_specs=in_specs,
        out_specs=[pl.BlockSpec((tl, wd), lambda b, t: (b * nt + t, 0)),
                   pl.BlockSpec((None, 1, wd), lambda b, t: (b, 0, 0))],
        out_shape=[jax.ShapeDtypeStruct((n * l, wd), out_dtype), jax.ShapeDtypeStruct((n, 1, wd), F32)],
        scratch_shapes=[pltpu.VMEM((8 + tl, wd), F32), pltpu.VMEM((8, wd), F32),
                        pltpu.VMEM((tl, wd), F32), pltpu.VMEM((tl, wd), F32), pltpu.VMEM((tl, wd), F32),
                        pltpu.VMEM((1, wd), F32),
                        pltpu.VMEM((LRU_BLOCKS, bd, bd), BF16), pltpu.VMEM((LRU_BLOCKS, bd, bd), BF16)],
        compiler_params=_cparams("arbitrary", "arbitrary"),
        name="lru_core",
    )(*args)
    return y, last.reshape(n, wd)


def _pad_buf8(buf):
    return jnp.pad(buf, ((0, 0), (5, 0), (0, 0)))


def _trunk(seq, x, mod, pos, cache, p):
    n, l, m = seq.n, seq.l, seq.m
    prompt = seq.prompt
    d = x.shape[1]
    act = BF16 if prompt else F32
    new = {}
    reps = 1 if prompt else n

    h = norm_mod(seq, x, p["g_mix"], 0, mod, 0)
    rope64 = rope_tables(pos, SWA_HD, reps)
    nq = SWA_QH * SWA_HD
    q = matmul(seq, h, p["swa_w_qkv"], 0, col0=0, ncols=nq, tn=1024, out_dtype=act, epi="rope", rope=rope64,
               n_rope=nq // 1024, hd=SWA_HD, name="swa_q")
    tm_kv = 2 * seq.tm if prompt else seq.tm
    kv = matmul(seq, h, p["swa_w_qkv"], 0, col0=nq, tn=256, tm=tm_kv, out_dtype=F32, epi="rope", rope=rope64,
                n_rope=1, hd=SWA_HD, name="swa_kv")
    akw = dict(d=1, hd=SWA_HD, kvh=SWA_KVH, grp=SWA_QH // SWA_KVH, q_blk=0, k_blk=0, v_blk=1,
               sinks=p["swa_sinks"][0], want_lse=False)
    if prompt:
        o, _ = attn_prompt(n, l, q, kv, out_dtype=BF16, **akw)
        keep = min(SWA_WINDOW, l)
        new["swa_kv"] = kv.reshape(n, l, -1)[:, l - keep:].reshape(n, keep, 2, SWA_KVH, SWA_HD)
    else:
        c = cache["swa_kv"][0]
        o, _ = attn_sample(n, l, q, kv, c.reshape(n, c.shape[1], 2 * SWA_KVH, SWA_HD), **akw)
        new["swa_kv"] = kv.reshape(n, l, 2, SWA_KVH, SWA_HD)
    x = matmul(seq, o, p["swa_w_o"], 0, tn=1024, out_dtype=F32, epi="resid", res=x, mod=mod, mod_layer=0, mod_k=2,
               name="swa_o")
    x = ffn(seq, norm_mod(seq, x, p["g_ffn"], 0, mod, 3), p["w_ff1"], p["w_ff2"], 0, x, mod)

    h = norm_mod(seq, x, p["g_mix"], 1, mod, 0)
    nzx = SSM_DI + SSM_DI + 2 * SSM_G * SSM_S
    w_dt = p["ssd_w_in"][:, :, nzx:]
    dt = matmul(seq, h, w_dt, 0, tn=SSM_HEADS, out_dtype=F32, bias=p["ssd_dt_bias"], epi="softplus", name="ssd_dt")
    zx = matmul(seq, h, p["ssd_w_in"], 0, col0=0, ncols=nzx, tn=1024, out_dtype=F32, name="ssd_in")
    zx3 = zx.reshape(n, l, nzx)
    if prompt:
        buf8 = jnp.zeros((n, 8, nzx - SSM_DI), F32)
        h0 = None
        new["ssd_conv"] = zx3[:, l - 3:, SSM_DI:]
    else:
        buf8 = _pad_buf8(cache["ssd_conv"][0])
        h0 = cache["ssd"][0]
        new["ssd_conv"] = jnp.concatenate([cache["ssd_conv"][0], zx3[:, :, SSM_DI:]], axis=1)[:, -3:]
    y, new["ssd"] = ssd_core(n, l, zx, dt, buf8, h0, p, act, ng=SSM_G)
    x = matmul(seq, y, p["ssd_w_out"], 0, tn=512, out_dtype=F32, epi="resid", res=x, mod=mod,
               mod_layer=1, mod_k=2, name="ssd_out")
    x = ffn(seq, norm_mod(seq, x, p["g_ffn"], 1, mod, 3), p["w_ff1"], p["w_ff2"], 1, x, mod)

    h = norm_mod(seq, x, p["g_mix"], 2, mod, 0)
    rope128 = rope_tables(pos, DIL_HD, reps)
    ng = len(DIL_PATTERN)
    nq = ng * DIL_QH * DIL_HD
    q = matmul(seq, h, p["dil_w_qkv"], 0, col0=0, ncols=nq, tn=1024, out_dtype=F32, epi="rope", rope=rope128,
               n_rope=nq // 1024, hd=DIL_HD, name="dil_q")
    ck = DIL_KVH * DIL_HD
    grp = DIL_QH // DIL_KVH
    outs, lses = [], []
    for g, ((w, dil), key) in enumerate(zip(DIL_PATTERN, ("dil_kv_w128", "dil_kv_w512", "dil_kv_w2048"))):
        kv = matmul(seq, h, p["dil_w_qkv"], 0, col0=nq + g * ck, ncols=2 * ck, cstride=ng, tn=ck, tm=tm_kv,
                    out_dtype=F32, epi="rope", rope=rope128, n_rope=1, hd=DIL_HD, name="dil_kv")
        akw = dict(d=dil, hd=DIL_HD, kvh=DIL_KVH, grp=grp, q_blk=g, k_blk=0, v_blk=1)
        if not prompt:
            c = cache[key][0]
            o, lse = attn_sample(n, l, q, kv, c.reshape(n, c.shape[1], 2 * DIL_KVH, DIL_HD), **akw)
            new[key] = kv.reshape(n, l, 2, DIL_KVH, DIL_HD)
        else:
            if dil == 1:
                o, lse = attn_prompt(n, l, q, kv, **akw)
            else:
                o, lse = attn_dil_prompt(n, l, q, kv, d=dil, kvh=DIL_KVH, grp=grp, q_head0=g * DIL_QH,
                                         k_head0=0, v_head0=DIL_KVH)
            keep = min(w, l)
            new[key] = kv.reshape(n, l, -1)[:, l - keep:].reshape(n, keep, 2, DIL_KVH, DIL_HD)
        outs.append(o)
        lses.append(lse)
    o = dil_merge(outs, lses, seq.tm if not prompt else 512, act)
    x = matmul(seq, o, p["dil_w_o"], 0, tn=1024, out_dtype=F32, epi="resid", res=x, mod=mod, mod_layer=2, mod_k=2,
               name="dil_o")
    x = ffn(seq, norm_mod(seq, x, p["g_ffn"], 2, mod, 3), p["w_ff1"], p["w_ff2"], 2, x, mod)

    h = norm_mod(seq, x, p["g_mix"], 3, mod, 0)
    wd = p["lru_w_in"].shape[2] // 2
    if prompt:
        gate = matmul(seq, h, p["lru_w_in"], 0, col0=0, ncols=wd, tn=1024, out_dtype=F32, bias=p["lru_b_in"],
                      name="lru_in_gate")
        xb, tails = matmul_conv(seq, h, p["lru_w_in"], 0, col0=wd, ncols=wd, tn=1024, bias=p["lru_b_in"],
                                conv_w=p["lru_conv_w"][0], conv_b=p["lru_conv_b"].reshape(1, wd), conv_col0=0,
                                silu=False, name="lru_in_xb")
        buf8 = tails
        h0 = None
        new["lru_conv"] = tails[:, 5:, :]
    else:
        gx = matmul(seq, h, p["lru_w_in"], 0, tn=1024, out_dtype=F32, bias=p["lru_b_in"], name="lru_in")
        gate, xb = gx[:, :wd], gx[:, wd:]
        buf8 = _pad_buf8(cache["lru_conv"][0])
        h0 = cache["lru"][0]
        new["lru_conv"] = jnp.concatenate([cache["lru_conv"][0], xb.reshape(n, l, wd)], axis=1)[:, -3:]
    y, new["lru"] = lru_core(n, l, gate, xb, buf8, h0, p, 512 if prompt else l, act, conv_done=prompt)
    x = matmul(seq, y, p["lru_w_out"], 0, tn=1024, out_dtype=F32, epi="resid", res=x, mod=mod, mod_layer=3, mod_k=2,
               name="lru_out")
    x = ffn(seq, norm_mod(seq, x, p["g_ffn"], 3, mod, 3), p["w_ff1"], p["w_ff2"], 3, x, mod)

    y = final_norm(x, p["g_final"], seq.tm)
    return y.reshape(n, l, d), {k: v[None] for k, v in new.items()}


def kernel(x_prompt, x_sample, cache_swa_kv, state_ssd_conv, state_ssd, cache_dil_kv_w128, cache_dil_kv_w512,
           cache_dil_kv_w2048, state_lru_conv, state_lru, c_prompt, c_sample, w_ada, b_ada, g_mix, g_ffn,
           w_ff1, w_ff2, g_final, swa_w_qkv, swa_sinks, swa_w_o, ssd_w_in, ssd_conv_w, ssd_conv_b,
           ssd_dt_bias, ssd_a_log, ssd_d, ssd_norm, ssd_w_out, dil_w_qkv, dil_w_o, lru_w_in, lru_b_in,
           lru_conv_w, lru_conv_b, lru_w_r, lru_b_r, lru_w_i, lru_b_i, lru_lam, lru_w_out):
    p = dict(g_mix=g_mix, g_ffn=g_ffn, w_ff1=w_ff1.astype(BF16), w_ff2=w_ff2.astype(BF16), g_final=g_final,
             swa_w_qkv=swa_w_qkv, swa_sinks=swa_sinks, swa_w_o=swa_w_o,
             ssd_w_in=ssd_w_in, ssd_conv_w=ssd_conv_w, ssd_conv_b=ssd_conv_b, ssd_dt_bias=ssd_dt_bias,
             ssd_a_log=ssd_a_log, ssd_d=ssd_d, ssd_norm=ssd_norm, ssd_w_out=ssd_w_out,
             dil_w_qkv=dil_w_qkv, dil_w_o=dil_w_o,
             lru_w_in=lru_w_in, lru_b_in=lru_b_in, lru_conv_w=lru_conv_w, lru_conv_b=lru_conv_b,
             lru_w_r=lru_w_r, lru_b_r=lru_b_r, lru_w_i=lru_w_i, lru_b_i=lru_b_i, lru_lam=lru_lam,
             lru_w_out=lru_w_out)
    cache = dict(swa_kv=cache_swa_kv, ssd_conv=state_ssd_conv, ssd=state_ssd, dil_kv_w128=cache_dil_kv_w128,
                 dil_kv_w512=cache_dil_kv_w512, dil_kv_w2048=cache_dil_kv_w2048, lru_conv=state_lru_conv,
                 lru=state_lru)
    nb, l, d = x_prompt.shape
    ns, ls, _ = x_sample.shape
    depth = w_ada.shape[0]
    rows = -(-(nb + ns) // 16) * 16
    c_all = jnp.concatenate([c_prompt, c_sample, jnp.zeros((rows - nb - ns, d), F32)], axis=0)
    mod = ada_mod(c_all, w_ada, b_ada)
    mod_p = mod.reshape(depth, rows, 1, 6 * d)
    mod_s = jnp.repeat(mod[:, nb:nb + ns], ls, axis=1).reshape(depth, 1, ns * ls, 6 * d)

    seq_p = Seq(nb, l, 1024, True)
    seq_s = Seq(ns, ls, ns * ls, False)
    y_p, sp = _trunk(seq_p, x_prompt.reshape(nb * l, d), mod_p, jnp.arange(l, dtype=jnp.int32), None, p)
    y_s, ss = _trunk(seq_s, x_sample.reshape(ns * ls, d), mod_s, PAST_LEN + jnp.arange(ls, dtype=jnp.int32), cache, p)
    return (y_p, y_s,
            sp["swa_kv"], ss["swa_kv"],
            sp["ssd_conv"], ss["ssd_conv"],
            sp["ssd"], ss["ssd"],
            sp["dil_kv_w128"], ss["dil_kv_w128"],
            sp["dil_kv_w512"], ss["dil_kv_w512"],
            sp["dil_kv_w2048"], ss["dil_kv_w2048"],
            sp["lru_conv"], ss["lru_conv"],
            sp["lru"], ss["lru"])
```

```python
import functools

import jax
import jax.numpy as jnp
from jax import lax
from jax.experimental import pallas as pl
from jax.experimental.pallas import tpu as pltpu

F32 = jnp.float32
BF16 = jnp.bfloat16
HI = lax.Precision.HIGHEST

RMS_EPS = 1e-6
ROPE_THETA = 10000.0
NEG_INF = -1e30
LANES = 128
MXU_COLS = 256
VMEM_LIMIT = 56 * 1024 * 1024

PAST_LEN = 16384
ATTN_BLOCK = 128
SWA_WINDOW, SWA_HD, SWA_QH, SWA_KVH = 128, 64, 32, 4
DIL_PATTERN = ((128, 1), (512, 4), (2048, 16))
DIL_HD, DIL_QH, DIL_KVH = 128, 16, 4
SSM_HEADS, SSM_P, SSM_S, SSM_G, SSM_HPG, SSM_CHUNK = 64, 64, 128, 8, 8, 128
SSM_DI = SSM_HEADS * SSM_P
SSM_GW = SSM_HPG * SSM_P
LRU_BLOCKS, LRU_C = 8, 8.0


def _cparams(*sem):
    return pltpu.CompilerParams(dimension_semantics=sem, vmem_limit_bytes=VMEM_LIMIT)


def _nt_dot(a, b):
    return lax.dot_general(a, b, (((1,), (1,)), ((), ())), preferred_element_type=F32)


def _sigmoid(x):
    return 0.5 * jnp.tanh(0.5 * x) + 0.5


def _tn_dot(a, b):
    return lax.dot_general(a, b, (((0,), (0,)), ((), ())), preferred_element_type=F32)


class Seq:
    def __init__(self, n, l, tm, prompt):
        self.n, self.l, self.m, self.tm, self.prompt = n, l, n * l, tm, prompt

    def mod_spec(self, layer, k, d, row_axis):
        tm, l = self.tm, self.l
        if self.prompt:
            return pl.BlockSpec((None, None, 1, d), lambda *g: (layer, (g[row_axis] * tm) // l, 0, k))
        return pl.BlockSpec((None, None, tm, d), lambda *g: (layer, 0, g[row_axis], k))


def _ada_kernel(c_ref, w_ref, b_ref, o_ref):
    c = c_ref[...]
    cond = (c * jax.nn.sigmoid(c)).astype(BF16)
    o_ref[...] = jnp.dot(cond, w_ref[...].astype(BF16), preferred_element_type=F32) + b_ref[...]


def ada_mod(c_all, w_ada, b_ada, tn=1024):
    depth, d, n6 = w_ada.shape
    r = c_all.shape[0]
    return pl.pallas_call(
        _ada_kernel,
        grid=(depth, n6 // tn),
        in_specs=[pl.BlockSpec((r, d), lambda a, j: (0, 0)),
                  pl.BlockSpec((None, d, tn), lambda a, j: (a, 0, j)),
                  pl.BlockSpec((None, 1, tn), lambda a, j: (a, 0, j))],
        out_specs=pl.BlockSpec((None, r, tn), lambda a, j: (a, 0, j)),
        out_shape=jax.ShapeDtypeStruct((depth, r, n6), F32),
        compiler_params=_cparams("arbitrary", "arbitrary"),
        name="ada_mod",
    )(c_all, w_ada, b_ada.reshape(depth, 1, n6))


def _norm_mod_kernel(x_ref, g_ref, sh_ref, sc_ref, o_ref):
    x = x_ref[...]
    y = x * lax.rsqrt(jnp.mean(x * x, axis=-1, keepdims=True) + RMS_EPS) * g_ref[...]
    o_ref[...] = (y * (1.0 + sc_ref[...]) + sh_ref[...]).astype(o_ref.dtype)


def _norm_kernel(x_ref, g_ref, o_ref):
    x = x_ref[...]
    o_ref[...] = x * lax.rsqrt(jnp.mean(x * x, axis=-1, keepdims=True) + RMS_EPS) * g_ref[...]


def norm_mod(seq, x, g, layer, mod, k_shift):
    m, d = x.shape
    tm = seq.tm
    sub = Seq(seq.n, seq.l, tm, seq.prompt)
    return pl.pallas_call(
        _norm_mod_kernel,
        grid=(m // tm,),
        in_specs=[pl.BlockSpec((tm, d), lambda i: (i, 0)),
                  pl.BlockSpec((None, 1, d), lambda i: (layer, 0, 0)),
                  sub.mod_spec(layer, k_shift, d, 0),
                  sub.mod_spec(layer, k_shift + 1, d, 0)],
        out_specs=pl.BlockSpec((tm, d), lambda i: (i, 0)),
        out_shape=jax.ShapeDtypeStruct((m, d), BF16),
        compiler_params=_cparams("arbitrary"),
        name="norm_mod",
    )(x, g.reshape(g.shape[0], 1, d), mod, mod)


def final_norm(x, g, tm):
    m, d = x.shape
    return pl.pallas_call(
        _norm_kernel,
        grid=(m // tm,),
        in_specs=[pl.BlockSpec((tm, d), lambda i: (i, 0)), pl.BlockSpec((1, d), lambda i: (0, 0))],
        out_specs=pl.BlockSpec((tm, d), lambda i: (i, 0)),
        out_shape=jax.ShapeDtypeStruct((m, d), F32),
        compiler_params=_cparams("arbitrary"),
        name="final_norm",
    )(x, g.reshape(1, d))


def _rope_cols(y, cos, sin, hd):
    outs = []
    for c in range(y.shape[1] // LANES):
        yc = y[:, c * LANES:(c + 1) * LANES]
        if hd == LANES:
            partner = pltpu.roll(yc, LANES // 2, axis=1)
        else:
            lane = lax.broadcasted_iota(jnp.int32, yc.shape, 1)
            partner = jnp.where(lane % hd < hd // 2, pltpu.roll(yc, LANES - hd // 2, axis=1),
                                pltpu.roll(yc, hd // 2, axis=1))
        outs.append(yc * cos + partner * sin)
    return outs[0] if len(outs) == 1 else jnp.concatenate(outs, axis=1)


def _mm_kernel(*refs, epi, n_rope, all_rope, hd, has_bias):
    x_ref, w_ref = refs[0], refs[1]
    o_ref, wb_ref = refs[-2], refs[-1]
    extra = refs[2:-2]

    @pl.when(pl.program_id(1) == 0)
    def _():
        wb_ref[...] = w_ref[...].astype(BF16)

    if has_bias:
        bias_ref, extra = extra[0], extra[1:]
    xb = x_ref[...].astype(BF16)
    tn = o_ref.shape[1]
    sub = min(tn, MXU_COLS)
    for c in range(tn // sub):
        cs = slice(c * sub, (c + 1) * sub)
        y = jnp.dot(xb, wb_ref[:, cs], preferred_element_type=F32)
        if has_bias:
            y = y + bias_ref[:, cs]
        if epi == "none":
            out = y
        elif epi == "softplus":
            out = jax.nn.softplus(y)
        elif epi == "resid":
            res_ref, gate_ref = extra
            out = res_ref[:, cs] + gate_ref[:, cs] * y
        elif epi == "rope":
            cos_ref, sin_ref = extra
            out = _rope_cols(y, cos_ref[...], sin_ref[...], hd)
            if not all_rope:
                out = jnp.where(pl.program_id(0) < n_rope, out, y)
        else:
            raise ValueError(epi)
        o_ref[:, cs] = out.astype(o_ref.dtype)


def matmul(seq, x, w, layer, *, col0=0, ncols=None, tn, out_dtype, tm=None, bias=None, epi="none",
           rope=None, n_rope=0, hd=LANES, res=None, mod=None, mod_layer=0, mod_k=0, cstride=1, name="mm"):
    m, k = x.shape
    ntot = w.shape[2]
    ncols = ntot - col0 if ncols is None else ncols
    tm = seq.tm if tm is None else tm
    cb = col0 // tn
    assert col0 % tn == 0 and ncols % tn == 0 and m % tm == 0
    in_specs = [pl.BlockSpec((tm, k), lambda j, i: (i, 0)),
                pl.BlockSpec((None, k, tn), lambda j, i: (layer, 0, cb + j * cstride))]
    args = [x, w]
    if bias is not None:
        in_specs.append(pl.BlockSpec((None, 1, tn), lambda j, i: (0, 0, cb + j * cstride)))
        args.append(bias.reshape(1, 1, -1))
    if epi == "rope":
        cos, sin = rope
        nrt = cos.shape[0] // tm
        for t in (cos, sin):
            in_specs.append(pl.BlockSpec((tm, LANES), lambda j, i: (i % nrt, 0)))
            args.append(t)
    if epi == "resid":
        sub = Seq(seq.n, seq.l, tm, seq.prompt)
        base = sub.mod_spec(mod_layer, mod_k, tn, 1)
        nk = w.shape[2] // tn
        gate_spec = pl.BlockSpec(base.block_shape,
                                 lambda j, i, f=base.index_map: f(j, i)[:3] + (f(j, i)[3] * nk + j,))
        in_specs += [pl.BlockSpec((tm, tn), lambda j, i: (i, j)), gate_spec]
        args += [res, mod]
    return pl.pallas_call(
        functools.partial(_mm_kernel, epi=epi, n_rope=n_rope, all_rope=n_rope >= ncols // tn, hd=hd,
                          has_bias=bias is not None),
        grid=(ncols // tn, m // tm),
        in_specs=in_specs,
        out_specs=pl.BlockSpec((tm, tn), lambda j, i: (i, j)),
        out_shape=jax.ShapeDtypeStruct((m, ncols), out_dtype),
        scratch_shapes=[pltpu.VMEM((k, tn), BF16)],
        compiler_params=_cparams("arbitrary", "arbitrary"),
        name=name,
    )(*args)


def _mm_stream_kernel(x_hbm, w_ref, o_hbm, wb_ref, *, tm, nbuf):
    j = pl.program_id(0)
    k, tn = wb_ref.shape
    wb_ref[...] = w_ref[...].astype(BF16)
    sub = min(tn, MXU_COLS)

    def body(x_ref, o_ref):
        xb = x_ref[...].astype(BF16)
        for c in range(tn // sub):
            cs = slice(c * sub, (c + 1) * sub)
            o_ref[:, cs] = jnp.dot(xb, wb_ref[:, cs], preferred_element_type=F32).astype(o_ref.dtype)

    pltpu.emit_pipeline(
        body,
        grid=(x_hbm.shape[0] // tm,),
        in_specs=[pl.BlockSpec((tm, k), lambda i: (i, 0), pipeline_mode=pl.Buffered(nbuf))],
        out_specs=[pl.BlockSpec((tm, tn), lambda i: (i, 0))],
    )(x_hbm, o_hbm.at[:, pl.ds(pl.multiple_of(j * tn, tn), tn)])


def matmul_stream(x, w, layer, *, ncols, tn, tm, out_dtype, nbuf=3, name="mm_stream"):
    m, k = x.shape
    assert ncols % tn == 0 and m % tm == 0
    return pl.pallas_call(
        functools.partial(_mm_stream_kernel, tm=tm, nbuf=nbuf),
        grid=(ncols // tn,),
        in_specs=[pl.BlockSpec(memory_space=pl.ANY),
                  pl.BlockSpec((None, k, tn), lambda j: (layer, 0, j))],
        out_specs=pl.BlockSpec(memory_space=pl.ANY),
        out_shape=jax.ShapeDtypeStruct((m, ncols), out_dtype),
        scratch_shapes=[pltpu.VMEM((k, tn), BF16)],
        compiler_params=_cparams("arbitrary"),
        name=name,
    )(x, w)


def _mm_conv_kernel(*refs, tiles_per_seq, silu, has_bias):
    x_ref, w_ref = refs[0], refs[1]
    k = 3 if has_bias else 2
    cw_ref, cb_ref = refs[k], refs[k + 1]
    o_ref, tails_ref, wb_ref, xp_ref, tail_ref = refs[k + 2:]
    i = pl.program_id(1)

    @pl.when(i == 0)
    def _():
        wb_ref[...] = w_ref[...].astype(BF16)

    @pl.when(i % tiles_per_seq == 0)
    def _():
        tail_ref[...] = jnp.zeros(tail_ref.shape, F32)

    xb = x_ref[...].astype(BF16)
    tm, tn = o_ref.shape
    sub = min(tn, MXU_COLS)
    for c in range(tn // sub):
        cs = slice(c * sub, (c + 1) * sub)
        y = jnp.dot(xb, wb_ref[:, cs], preferred_element_type=F32)
        if has_bias:
            y = y + refs[2][:, cs]
        xp_ref[c, 0:8, :] = tail_ref[:, cs]
        xp_ref[c, 8:8 + tm, :] = y
        acc = cb_ref[:, cs]
        for t in range(4):
            acc = acc + xp_ref[c, 5 + t:5 + t + tm, :] * cw_ref[t:t + 1, cs]
        last = y[tm - 8:tm, :]
        tail_ref[:, cs] = last
        tails_ref[:, cs] = last
        o_ref[:, cs] = (acc * _sigmoid(acc) if silu else acc).astype(o_ref.dtype)


def matmul_conv(seq, x, w, layer, *, col0, ncols, tn, conv_w, conv_b, conv_col0, silu, bias=None, name="mm_conv"):
    m, k = x.shape
    tm = seq.tm
    assert seq.prompt and seq.l % tm == 0 and col0 % tn == 0 and ncols % tn == 0 and conv_col0 % tn == 0
    cb, ccb, tps = col0 // tn, conv_col0 // tn, seq.l // tm
    sub = min(tn, MXU_COLS)
    in_specs = [pl.BlockSpec((tm, k), lambda j, i: (i, 0)),
                pl.BlockSpec((None, k, tn), lambda j, i: (layer, 0, cb + j))]
    args = [x, w]
    if bias is not None:
        in_specs.append(pl.BlockSpec((None, 1, tn), lambda j, i: (0, 0, cb + j)))
        args.append(bias.reshape(1, 1, -1))
    in_specs += [pl.BlockSpec((4, tn), lambda j, i: (0, ccb + j)), pl.BlockSpec((1, tn), lambda j, i: (0, ccb + j))]
    args += [conv_w, conv_b]
    return pl.pallas_call(
        functools.partial(_mm_conv_kernel, tiles_per_seq=tps, silu=silu, has_bias=bias is not None),
        grid=(ncols // tn, m // tm),
        in_specs=in_specs,
        out_specs=[pl.BlockSpec((tm, tn), lambda j, i: (i, j)),
                   pl.BlockSpec((None, 8, tn), lambda j, i: (i // tps, 0, j))],
        out_shape=[jax.ShapeDtypeStruct((m, ncols), F32), jax.ShapeDtypeStruct((seq.n, 8, ncols), F32)],
        scratch_shapes=[pltpu.VMEM((k, tn), BF16), pltpu.VMEM((tn // sub, 8 + tm, sub), F32),
                        pltpu.VMEM((8, tn), F32)],
        compiler_params=_cparams("arbitrary", "arbitrary"),
        name=name,
    )(*args)


def _ffn_kernel(x_ref, w1_ref, w2_ref, res_ref, gate_ref, o_ref, h_ref, *, na, tf):
    s = pl.program_id(1)

    @pl.when(s < na)
    def _():
        h = jnp.dot(x_ref[...], w1_ref[...], preferred_element_type=F32)
        col = pl.multiple_of(s * tf, tf)
        h_ref[:, pl.ds(col, tf)] = jnp.square(jnp.maximum(h, 0.0)).astype(BF16)

    @pl.when(s >= na)
    def _():
        y = jnp.dot(h_ref[...], w2_ref[...], preferred_element_type=F32)
        o_ref[...] = res_ref[...] + gate_ref[...] * y


def _ffn_tiles(tm, d, f):
    for tf, tn in ((2048, 512), (1024, 512), (1024, 256), (512, 256)):
        windows = 2 * 2 * (tm * d + d * tf + f * tn) + 2 * 2 * 4 * tm * tn
        if windows + 2 * tm * f + 6 * tm * tf <= 0.9 * VMEM_LIMIT:
            return tf, tn
    raise ValueError("FFN row tile too large for VMEM")


def ffn(seq, h, w1, w2, layer, res, mod):
    m, d = h.shape
    f = w1.shape[2]
    tm = seq.tm
    tf, tn = _ffn_tiles(tm, d, f)
    na, nb = f // tf, d // tn
    sub = Seq(seq.n, seq.l, tm, seq.prompt)
    base = sub.mod_spec(layer, 5, tn, 0)
    ocol = lambda s: jnp.maximum(s - na, 0)
    gate_spec = pl.BlockSpec(base.block_shape,
                             lambda i, s, fm=base.index_map: fm(i, s)[:3] + (5 * nb + ocol(s),))
    return pl.pallas_call(
        functools.partial(_ffn_kernel, na=na, tf=tf),
        grid=(m // tm, na + nb),
        in_specs=[pl.BlockSpec((tm, d), lambda i, s: (i, 0)),
                  pl.BlockSpec((None, d, tf), lambda i, s: (layer, 0, jnp.minimum(s, na - 1))),
                  pl.BlockSpec((None, f, tn), lambda i, s: (layer, 0, ocol(s))),
                  pl.BlockSpec((tm, tn), lambda i, s: (i, ocol(s))),
                  gate_spec],
        out_specs=pl.BlockSpec((tm, tn), lambda i, s: (i, ocol(s))),
        out_shape=jax.ShapeDtypeStruct((m, d), F32),
        scratch_shapes=[pltpu.VMEM((tm, f), BF16)],
        compiler_params=_cparams("arbitrary", "arbitrary"),
        name="ffn",
    )(h, w1, w2, res, mod)


def rope_tables(pos, hd, reps):
    half = hd // 2
    inv = ROPE_THETA ** (-jnp.arange(half, dtype=F32) / half)
    ang = pos.astype(F32)[:, None] * inv[None, :]
    cos = jnp.concatenate([jnp.cos(ang), jnp.cos(ang)], axis=-1)
    sin = jnp.concatenate([-jnp.sin(ang), jnp.sin(ang)], axis=-1)
    lane_reps = LANES // hd
    return jnp.tile(cos, (reps, lane_reps)), jnp.tile(sin, (reps, lane_reps))


def _attn_prompt_kernel(*refs, hd, kvh, grp, scale, has_sinks, want_lse):
    if has_sinks:
        sink_ref, refs = refs[0], refs[1:]
    q_ref, kp_ref, kc_ref, vp_ref, vc_ref = refs[:5]
    o_ref = refs[5]
    lse_ref = refs[6] if want_lse else None
    ub = pl.program_id(2)
    bq = q_ref.shape[0]
    rows = grp * bq
    iq = lax.broadcasted_iota(jnp.int32, (rows, 2 * bq), 0) % bq
    jk = lax.broadcasted_iota(jnp.int32, (rows, 2 * bq), 1)
    mask = (jk >= iq) & (jk <= iq + bq) & ((jk >= bq) | (ub > 0))
    rcol = lax.broadcasted_iota(jnp.int32, (rows, 1), 0)
    lane = lax.broadcasted_iota(jnp.int32, (bq, LANES), 1)
    lse_tile = jnp.zeros((bq, LANES), F32)
    for kh in range(kvh):
        ksl = slice(kh * hd, (kh + 1) * hd)
        kk = jnp.concatenate([kp_ref[:, ksl], kc_ref[:, ksl]], axis=0).astype(BF16)
        vv = jnp.concatenate([vp_ref[:, ksl], vc_ref[:, ksl]], axis=0).astype(BF16)
        qs = jnp.concatenate([q_ref[:, (kh * grp + g) * hd:(kh * grp + g + 1) * hd] for g in range(grp)], axis=0)
        s = jnp.where(mask, _nt_dot(qs.astype(BF16), kk) * scale, NEG_INF)
        mx = jnp.max(s, axis=-1, keepdims=True)
        if has_sinks:
            sk = jnp.zeros((rows, 1), F32)
            for g in range(grp):
                sk = jnp.where(rcol // bq == g, sink_ref[kh * grp + g], sk)
            mx = jnp.maximum(mx, sk)
        p = jnp.exp(s - mx)
        den = jnp.sum(p, axis=-1, keepdims=True)
        if has_sinks:
            den = den + jnp.exp(sk - mx)
        o = jnp.dot(p.astype(BF16), vv, preferred_element_type=F32) / den
        lse = mx + jnp.log(den)
        for g in range(grp):
            h = kh * grp + g
            o_ref[:, h * hd:(h + 1) * hd] = o[g * bq:(g + 1) * bq, :].astype(o_ref.dtype)
            if want_lse:
                lse_tile = jnp.where(lane == h, lse[g * bq:(g + 1) * bq, :], lse_tile)
    if want_lse:
        lse_ref[...] = lse_tile


def attn_prompt(n, l, q, kv, *, d, hd, kvh, grp, q_blk, k_blk, v_blk, sinks=None, want_lse=True, out_dtype=F32):
    bq = ATTN_BLOCK
    cq, ck = grp * kvh * hd, kvh * hd
    nq, nk = q.shape[1] // cq, kv.shape[1] // ck
    lu = l // d
    qv = q.reshape(n, lu, d * q.shape[1])
    kvv = kv.reshape(n, lu, d * kv.shape[1])
    prev = lambda u: jnp.maximum(u - 1, 0)
    in_specs = [pl.BlockSpec((None, bq, cq), lambda b, r, u: (b, u, r * nq + q_blk)),
                pl.BlockSpec((None, bq, ck), lambda b, r, u: (b, prev(u), r * nk + k_blk)),
                pl.BlockSpec((None, bq, ck), lambda b, r, u: (b, u, r * nk + k_blk)),
                pl.BlockSpec((None, bq, ck), lambda b, r, u: (b, prev(u), r * nk + v_blk)),
                pl.BlockSpec((None, bq, ck), lambda b, r, u: (b, u, r * nk + v_blk))]
    args = [qv, kvv, kvv, kvv, kvv]
    if sinks is not None:
        in_specs.insert(0, pl.BlockSpec(memory_space=pltpu.SMEM))
        args.insert(0, sinks)
    out_shape = [jax.ShapeDtypeStruct((n, lu, d * cq), out_dtype)]
    out_specs = [pl.BlockSpec((None, bq, cq), lambda b, r, u: (b, u, r))]
    if want_lse:
        out_shape.append(jax.ShapeDtypeStruct((n, lu, d * LANES), F32))
        out_specs.append(pl.BlockSpec((None, bq, LANES), lambda b, r, u: (b, u, r)))
    outs = pl.pallas_call(
        functools.partial(_attn_prompt_kernel, hd=hd, kvh=kvh, grp=grp, scale=hd ** -0.5,
                          has_sinks=sinks is not None, want_lse=want_lse),
        grid=(n, d, lu // bq),
        in_specs=in_specs,
        out_specs=out_specs,
        out_shape=out_shape,
        compiler_params=_cparams("arbitrary", "arbitrary", "arbitrary"),
        name=f"attn_prompt_d{d}",
    )(*args)
    o = outs[0].reshape(n * l, cq)
    return (o, outs[1].reshape(n * l, LANES)) if want_lse else (o, None)


def _attn_dil_kernel(*refs, d, grp, scale, has_prev):
    q_refs = refs[:grp]
    kp_ref, kc_ref, vp_ref, vc_ref = refs[grp:grp + 4]
    o_refs = refs[grp + 4:2 * grp + 4]
    lse_ref = refs[2 * grp + 4]
    ub, kh = pl.program_id(1), pl.program_id(2)
    bq = ATTN_BLOCK
    nk = 2 * bq if has_prev else bq
    iq = lax.broadcasted_iota(jnp.int32, (grp * bq, nk), 0) % bq
    jk = lax.broadcasted_iota(jnp.int32, (grp * bq, nk), 1)
    if has_prev:
        mask = (jk >= iq) & (jk <= iq + bq) & ((jk >= bq) | (ub > 0))
    else:
        mask = jk <= iq
    lane = lax.broadcasted_iota(jnp.int32, (bq, LANES), 1)

    @pl.when(kh == 0)
    def _():
        lse_ref[...] = jnp.zeros(lse_ref.shape, F32)

    for r in range(d):
        rows = pl.ds(r, bq, stride=d)
        qs = jnp.concatenate([qr[rows, :] for qr in q_refs], axis=0).astype(BF16)
        if has_prev:
            kk = jnp.concatenate([kp_ref[rows, :], kc_ref[rows, :]], axis=0).astype(BF16)
            vv = jnp.concatenate([vp_ref[rows, :], vc_ref[rows, :]], axis=0).astype(BF16)
        else:
            kk = kc_ref[rows, :].astype(BF16)
            vv = vc_ref[rows, :].astype(BF16)
        s = jnp.where(mask, _nt_dot(qs, kk) * scale, NEG_INF)
        mx = jnp.max(s, axis=-1, keepdims=True)
        p = jnp.exp(s - mx)
        den = jnp.sum(p, axis=-1, keepdims=True)
        o = jnp.dot(p.astype(BF16), vv, preferred_element_type=F32) / den
        lse = mx + jnp.log(den)
        tile = lse_ref[r * bq:(r + 1) * bq, :]
        for g in range(grp):
            o_refs[g][rows, :] = o[g * bq:(g + 1) * bq, :]
            tile = jnp.where(lane == kh * grp + g, lse[g * bq:(g + 1) * bq, :], tile)
        lse_ref[r * bq:(r + 1) * bq, :] = tile


def attn_dil_prompt(n, l, q, kv, *, d, kvh, grp, q_head0, k_head0, v_head0):
    hd = LANES
    rt = d * ATTN_BLOCK
    nub = l // rt
    has_prev = nub > 1
    prev = lambda u: jnp.maximum(u - 1, 0)
    cur = lambda u: u
    blk = lambda col, rowf: pl.BlockSpec((rt, hd), lambda b, u, kh: (b * nub + rowf(u), col(kh)))
    q_specs = [blk(lambda kh, g=g: q_head0 + kh * grp + g, cur) for g in range(grp)]
    outs = pl.pallas_call(
        functools.partial(_attn_dil_kernel, d=d, grp=grp, scale=hd ** -0.5, has_prev=has_prev),
        grid=(n, nub, kvh),
        in_specs=q_specs + [blk(lambda kh: k_head0 + kh, prev), blk(lambda kh: k_head0 + kh, cur),
                            blk(lambda kh: v_head0 + kh, prev), blk(lambda kh: v_head0 + kh, cur)],
        out_specs=[blk(lambda kh: kh, cur)] * grp + [blk(lambda kh: 0, cur)],
        out_shape=[jax.ShapeDtypeStruct((n * l, kvh * hd), F32)] * grp + [jax.ShapeDtypeStruct((n * l, LANES), F32)],
        compiler_params=_cparams("arbitrary", "arbitrary", "arbitrary"),
        name=f"attn_dil_d{d}",
    )(*([q] * grp), kv, kv, kv, kv)
    lse = outs[grp].reshape(n * nub, d, ATTN_BLOCK, LANES).transpose(0, 2, 1, 3).reshape(n * l, LANES)
    return list(outs[:grp]), lse


def _attn_sample_kernel(*refs, hd, kvh, grp, scale, d, has_sinks, want_lse):
    if has_sinks:
        sink_ref, refs = refs[0], refs[1:]
    q_ref, kn_ref, vn_ref, c_ref = refs[:4]
    o_ref = refs[4]
    lse_ref = refs[5] if want_lse else None
    nb = c_ref.shape[0]
    lq = q_ref.shape[0] // nb
    flat = len(c_ref.shape) == 3
    w = c_ref.shape[1] // (2 * kvh) if flat else c_ref.shape[1]

    def cache_head(bi, idx):
        return c_ref[bi, pl.ds(idx, w, stride=2 * kvh), :] if flat else c_ref[bi, :, idx, :]

    rows = grp * lq
    row = lax.broadcasted_iota(jnp.int32, (rows, w + lq), 0)
    jk = lax.broadcasted_iota(jnp.int32, (rows, w + lq), 1)
    dist = row % lq + w - jk
    mask = (dist >= 0) & (dist <= w) & (dist % d == 0)
    rcol = lax.broadcasted_iota(jnp.int32, (rows, 1), 0)
    lane = lax.broadcasted_iota(jnp.int32, (lq, LANES), 1)
    for bi in range(nb):
        rsl = slice(bi * lq, (bi + 1) * lq)
        lse_tile = jnp.zeros((lq, LANES), F32)
        for kh in range(kvh):
            ksl = slice(kh * hd, (kh + 1) * hd)
            kk = jnp.concatenate([cache_head(bi, kh), kn_ref[rsl, ksl]], axis=0).astype(BF16)
            vv = jnp.concatenate([cache_head(bi, kvh + kh), vn_ref[rsl, ksl]], axis=0).astype(BF16)
            qs = jnp.concatenate([q_ref[rsl, (kh * grp + g) * hd:(kh * grp + g + 1) * hd] for g in range(grp)],
                                 axis=0)
            s = jnp.where(mask, _nt_dot(qs.astype(BF16), kk) * scale, NEG_INF)
            mx = jnp.max(s, axis=-1, keepdims=True)
            if has_sinks:
                sk = jnp.zeros((rows, 1), F32)
                for g in range(grp):
                    sk = jnp.where(rcol // lq == g, sink_ref[kh * grp + g], sk)
                mx = jnp.maximum(mx, sk)
            p = jnp.exp(s - mx)
            den = jnp.sum(p, axis=-1, keepdims=True)
            if has_sinks:
                den = den + jnp.exp(sk - mx)
            o = jnp.dot(p.astype(BF16), vv, preferred_element_type=F32) / den
            lse = mx + jnp.log(den)
            for g in range(grp):
                h = kh * grp + g
                o_ref[rsl, h * hd:(h + 1) * hd] = o[g * lq:(g + 1) * lq, :]
                if want_lse:
                    lse_tile = jnp.where(lane == h, lse[g * lq:(g + 1) * lq, :], lse_tile)
        if want_lse:
            lse_ref[rsl, :] = lse_tile


def attn_sample(n, l, q, kv, cache, *, d, hd, kvh, grp, q_blk, k_blk, v_blk, sinks=None, want_lse=True):
    cq, ck = grp * kvh * hd, kvh * hd
    w = cache.shape[1]
    nb = max(b for b in (4, 2, 1) if n % b == 0 and b * w <= 4096)
    if hd == LANES:
        cache = cache.reshape(n, w * 2 * kvh, hd)
        cache_spec = pl.BlockSpec((nb, w * 2 * kvh, hd), lambda b: (b, 0, 0))
    else:
        cache_spec = pl.BlockSpec((nb, w, 2 * kvh, hd), lambda b: (b, 0, 0, 0))
    l, n = nb * l, n // nb
    in_specs = [pl.BlockSpec((l, cq), lambda b: (b, q_blk)),
                pl.BlockSpec((l, ck), lambda b: (b, k_blk)),
                pl.BlockSpec((l, ck), lambda b: (b, v_blk)),
                cache_spec]
    args = [q, kv, kv, cache]
    if sinks is not None:
        in_specs.insert(0, pl.BlockSpec(memory_space=pltpu.SMEM))
        args.insert(0, sinks)
    out_shape = [jax.ShapeDtypeStruct((n * l, cq), F32)]
    out_specs = [pl.BlockSpec((l, cq), lambda b: (b, 0))]
    if want_lse:
        out_shape.append(jax.ShapeDtypeStruct((n * l, LANES), F32))
        out_specs.append(pl.BlockSpec((l, LANES), lambda b: (b, 0)))
    outs = pl.pallas_call(
        functools.partial(_attn_sample_kernel, hd=hd, kvh=kvh, grp=grp, scale=hd ** -0.5, d=d,
                          has_sinks=sinks is not None, want_lse=want_lse),
        grid=(n,),
        in_specs=in_specs,
        out_specs=out_specs,
        out_shape=out_shape,
        compiler_params=_cparams("arbitrary"),
        name=f"attn_sample_d{d}",
    )(*args)
    return (outs[0], outs[1]) if want_lse else (outs[0], None)


def _dil_merge_kernel(*refs, counts, qh, hd):
    no = sum(counts)
    o_refs, l_refs, out_ref = refs[:no], refs[no:no + len(counts)], refs[-1]
    ls = [r[...] for r in l_refs]
    mx = functools.reduce(jnp.maximum, ls)
    es = [jnp.exp(v - mx) for v in ls]
    tot = functools.reduce(lambda a, b: a + b, es)
    wts = [e / tot for e in es]
    for h in range(qh):
        acc, first = None, 0
        for cnt, wt in zip(counts, wts):
            if cnt == 1:
                piece = o_refs[first][:, h * hd:(h + 1) * hd]
            else:
                piece = o_refs[first + h % cnt][:, (h // cnt) * hd:(h // cnt + 1) * hd]
            first += cnt
            term = wt[:, h:h + 1] * piece
            acc = term if acc is None else acc + term
        out_ref[:, h * hd:(h + 1) * hd] = acc.astype(out_ref.dtype)


def dil_merge(outs, lses, tm, out_dtype):
    groups = [o if isinstance(o, (list, tuple)) else [o] for o in outs]
    flat = [a for grp_arrays in groups for a in grp_arrays]
    m = flat[0].shape[0]
    c = DIL_QH * DIL_HD
    return pl.pallas_call(
        functools.partial(_dil_merge_kernel, counts=tuple(len(g) for g in groups), qh=DIL_QH, hd=DIL_HD),
        grid=(m // tm,),
        in_specs=[pl.BlockSpec((tm, a.shape[1]), lambda i: (i, 0)) for a in flat]
        + [pl.BlockSpec((tm, LANES), lambda i: (i, 0))] * len(lses),
        out_specs=pl.BlockSpec((tm, c), lambda i: (i, 0)),
        out_shape=jax.ShapeDtypeStruct((m, c), out_dtype),
        compiler_params=_cparams("arbitrary"),
        name="dil_merge",
    )(*flat, *lses)


def _conv_silu(xp_ref, tail_ref, x_ref, w_ref, b_ref, q):
    xp_ref[0:8, :] = tail_ref[...]
    xp_ref[8:8 + q, :] = x_ref[...]
    y = b_ref[...]
    for i in range(4):
        y = y + xp_ref[5 + i:5 + i + q, :] * w_ref[i:i + 1, :]
    tail_ref[...] = xp_ref[q:q + 8, :]
    return y * _sigmoid(y)


def _ssd_kernel(*refs, q, ng, has_state):
    (z_ref, x_ref, b_ref, c_ref, bufx_ref, bufb_ref, bufc_ref, dtc_ref, dtr_ref, alr_ref, alc_ref, dpar_ref,
     cwx_ref, cwb_ref, cwc_ref, cbx_ref, cbb_ref, cbc_ref, nw_ref) = refs[:19]
    refs = refs[19:]
    if has_state:
        h0_ref, refs = refs[0], refs[1:]
    y_ref, st_ref, xpx, xpb, xpc, tlx, tlb, tlc, state = refs
    c = pl.program_id(2)
    gw, s = SSM_GW, SSM_S

    @pl.when(c == 0)
    def _():
        tlx[...] = bufx_ref[...]
        tlb[...] = bufb_ref[...]
        tlc[...] = bufc_ref[...]
        state[...] = h0_ref[...] if has_state else jnp.zeros(state.shape, F32)

    xs_all = _conv_silu(xpx, tlx, x_ref, cwx_ref, cbx_ref, q)
    bm_all = _conv_silu(xpb, tlb, b_ref, cwb_ref, cbb_ref, q)
    cm_all = _conv_silu(xpc, tlc, c_ref, cwc_ref, cbc_ref, q)

    li = lax.broadcasted_iota(jnp.int32, (q, q), 0)
    mi = lax.broadcasted_iota(jnp.int32, (q, q), 1)
    causal = li >= mi
    tri_c = causal.astype(F32)
    tri_r = (li <= mi).astype(F32)
    low = lax.broadcasted_iota(jnp.int32, (q, LANES), 1) < SSM_P
    on_mxu = q % LANES == 0

    def onehot(width, block):
        r = lax.broadcasted_iota(jnp.int32, (SSM_HPG, width), 0)
        c = lax.broadcasted_iota(jnp.int32, (SSM_HPG, width), 1)
        return (c // block == r).astype(BF16)

    e_chan = onehot(gw, SSM_P)
    e_time = onehot(SSM_HPG * q, q) if on_mxu else None

    def expand(v, e):
        hi = v.astype(BF16)
        r1 = v - hi.astype(F32)
        mid = r1.astype(BF16)
        lo = (r1 - mid.astype(F32)).astype(BF16)
        return (jnp.dot(hi, e, preferred_element_type=F32) + jnp.dot(mid, e, preferred_element_type=F32)
                + jnp.dot(lo, e, preferred_element_type=F32))

    def widen(v):
        if on_mxu:
            return expand(v, e_chan)
        return jnp.concatenate([jnp.where(low, v[:, 2 * j:2 * j + 1], v[:, 2 * j + 1:2 * j + 2])
                                for j in range(gw // LANES)], axis=1)

    for k in range(ng):
        xs = xs_all[:, k * gw:(k + 1) * gw]
        bmb = bm_all[:, k * s:(k + 1) * s].astype(BF16)
        cmb = cm_all[:, k * s:(k + 1) * s].astype(BF16)
        dt_c = dtc_ref[k]
        dt_r = dtr_ref[k]
        a_r = -jnp.exp(alr_ref[k])
        a_c = -jnp.exp(alc_ref[k])
        cs_c = jnp.dot(tri_c, dt_c * a_r, precision=HI, preferred_element_type=F32)
        cs_r = jnp.dot(dt_r * a_c, tri_r, precision=HI, preferred_element_type=F32)
        cs_last = cs_c[q - 1:q, :]

        tend, ecs = jnp.exp(cs_last - cs_c), jnp.exp(cs_c)
        if on_mxu:
            wide = widen(jnp.concatenate([dt_c, tend, ecs], axis=0))
            dtw, tendw, ecsw = wide[:q], wide[q:2 * q], wide[2 * q:]
            segc = expand(cs_c, e_time)
        else:
            dtw, tendw, ecsw = widen(dt_c), widen(tend), widen(ecs)
        xdt = xs * dtw
        xe = (xdt * tendw).astype(BF16)
        xdtb = xdt.astype(BF16)
        cb = _nt_dot(cmb, bmb)
        st = state[k]
        y_off = _nt_dot(cmb, st.astype(BF16)) * ecsw
        y_diag = []
        for j in range(gw // LANES):
            pair = []
            for h in (2 * j, 2 * j + 1):
                col = segc[:, h * q:(h + 1) * q] if on_mxu else cs_c[:, h:h + 1]
                seg = col - cs_r[h:h + 1, :]
                gm = (cb * jnp.exp(jnp.where(causal, seg, NEG_INF))).astype(BF16)
                pair.append(jnp.dot(gm, xdtb[:, j * LANES:(j + 1) * LANES], preferred_element_type=F32))
            y_diag.append(jnp.where(low, pair[0], pair[1]))
        y = jnp.concatenate(y_diag, axis=1) + y_off + xs * dpar_ref[:, k * gw:(k + 1) * gw]
        new_st = _tn_dot(xe, bmb)
        dec_last = jnp.exp(cs_r[:, q - 1:q])
        st = jnp.concatenate([st[h * SSM_P:(h + 1) * SSM_P, :] * dec_last[h:h + 1, :] for h in range(SSM_HPG)],
                             axis=0) + new_st
        state[k] = st
        st_ref[k] = st

        z = z_ref[:, k * gw:(k + 1) * gw]
        y = y * (z * _sigmoid(z))
        y = y * lax.rsqrt(jnp.mean(y * y, axis=-1, keepdims=True) + RMS_EPS) * nw_ref[:, k * gw:(k + 1) * gw]
        y_ref[:, k * gw:(k + 1) * gw] = y.astype(y_ref.dtype)


def ssd_core(n, l, zx, dt, conv_buf8, h0, p, out_dtype, ng):
    q = SSM_CHUNK if l % SSM_CHUNK == 0 else l
    nc = l // q
    g = SSM_G
    gw, s = ng * SSM_GW, ng * SSM_S
    xb0 = SSM_DI // gw
    bb0 = 2 * SSM_DI // s
    cb0 = bb0 + g // ng
    kb0 = SSM_DI // s
    kc0 = kb0 + g // ng
    dt4 = dt.reshape(n, l, g, SSM_HPG)
    dt_c = jnp.transpose(dt4, (0, 2, 1, 3))
    dt_r = jnp.transpose(dt4, (0, 2, 3, 1))
    alog = p["ssd_a_log"].reshape(g, SSM_HPG)
    row = lambda b, gi, c: (b * nc + c)
    in_specs = [
        pl.BlockSpec((q, gw), lambda b, gi, c: (row(b, gi, c), gi)),
        pl.BlockSpec((q, gw), lambda b, gi, c: (row(b, gi, c), xb0 + gi)),
        pl.BlockSpec((q, s), lambda b, gi, c: (row(b, gi, c), bb0 + gi)),
        pl.BlockSpec((q, s), lambda b, gi, c: (row(b, gi, c), cb0 + gi)),
        pl.BlockSpec((None, 8, gw), lambda b, gi, c: (b, 0, gi)),
        pl.BlockSpec((None, 8, s), lambda b, gi, c: (b, 0, kb0 + gi)),
        pl.BlockSpec((None, 8, s), lambda b, gi, c: (b, 0, kc0 + gi)),
        pl.BlockSpec((None, ng, q, SSM_HPG), lambda b, gi, c: (b, gi, c, 0)),
        pl.BlockSpec((None, ng, SSM_HPG, q), lambda b, gi, c: (b, gi, 0, c)),
        pl.BlockSpec((ng, 1, SSM_HPG), lambda b, gi, c: (gi, 0, 0)),
        pl.BlockSpec((ng, SSM_HPG, 1), lambda b, gi, c: (gi, 0, 0)),
        pl.BlockSpec((1, gw), lambda b, gi, c: (0, gi)),
        pl.BlockSpec((4, gw), lambda b, gi, c: (0, gi)),
        pl.BlockSpec((4, s), lambda b, gi, c: (0, kb0 + gi)),
        pl.BlockSpec((4, s), lambda b, gi, c: (0, kc0 + gi)),
        pl.BlockSpec((1, gw), lambda b, gi, c: (0, gi)),
        pl.BlockSpec((1, s), lambda b, gi, c: (0, kb0 + gi)),
        pl.BlockSpec((1, s), lambda b, gi, c: (0, kc0 + gi)),
        pl.BlockSpec((1, gw), lambda b, gi, c: (0, gi)),
    ]
    cw = p["ssd_conv_w"][0]
    cbias = p["ssd_conv_b"]
    args = [zx, zx, zx, zx, conv_buf8, conv_buf8, conv_buf8, dt_c, dt_r,
            alog.reshape(g, 1, SSM_HPG), alog.reshape(g, SSM_HPG, 1),
            jnp.repeat(p["ssd_d"].reshape(-1), SSM_P).reshape(1, SSM_DI),
            cw, cw, cw, cbias, cbias, cbias, p["ssd_norm"]]
    st_spec = pl.BlockSpec((None, ng, SSM_GW, SSM_S), lambda b, gi, c: (b, gi, 0, 0))
    if h0 is not None:
        in_specs.append(st_spec)
        args.append(h0.reshape(n, g, SSM_GW, SSM_S))
    y, st = pl.pallas_call(
        functools.partial(_ssd_kernel, q=q, ng=ng, has_state=h0 is not None),
        grid=(n, g // ng, nc),
        in_specs=in_specs,
        out_specs=[pl.BlockSpec((q, gw), lambda b, gi, c: (row(b, gi, c), gi)), st_spec],
        out_shape=[jax.ShapeDtypeStruct((n * l, SSM_DI), out_dtype),
                   jax.ShapeDtypeStruct((n, g, SSM_GW, SSM_S), F32)],
        scratch_shapes=[pltpu.VMEM((8 + q, gw), F32), pltpu.VMEM((8 + q, s), F32), pltpu.VMEM((8 + q, s), F32),
                        pltpu.VMEM((8, gw), F32), pltpu.VMEM((8, s), F32), pltpu.VMEM((8, s), F32),
                        pltpu.VMEM((ng, SSM_GW, SSM_S), F32)],
        compiler_params=_cparams("arbitrary", "arbitrary", "arbitrary"),
        name="ssd_core",
    )(*args)
    return y, st.reshape(n, SSM_HEADS, SSM_P, SSM_S)


def _lru_kernel(*refs, tl, has_state, conv_done):
    (gate_ref, xb_ref, buf_ref, cw_ref, cb_ref, wr_ref, wi_ref, br_ref, bi_ref, lam_ref) = refs[:10]
    refs = refs[10:]
    if has_state:
        h0_ref, refs = refs[0], refs[1:]
    y_ref, last_ref, xp, tail, a_sc, u_sc, hs_sc, hcar, wrb, wib = refs
    t = pl.program_id(1)

    @pl.when((pl.program_id(0) == 0) & (t == 0))
    def _():
        wrb[...] = wr_ref[...].astype(BF16)
        wib[...] = wi_ref[...].astype(BF16)

    @pl.when(t == 0)
    def _():
        if not conv_done:
            tail[...] = buf_ref[...]
        hcar[...] = h0_ref[...] if has_state else jnp.zeros(hcar.shape, F32)

    if conv_done:
        xc = xb_ref[...]
    else:
        xp[0:8, :] = tail[...]
        xp[8:8 + tl, :] = xb_ref[...]
        xc = cb_ref[...]
        for i in range(4):
            xc = xc + xp[5 + i:5 + i + tl, :] * cw_ref[i:i + 1, :]
        tail[...] = xp[tl:tl + 8, :]

    xcb = xc.astype(BF16)
    bd = xc.shape[1] // LRU_BLOCKS
    rs, is_ = [], []
    for b in range(LRU_BLOCKS):
        xblk = xcb[:, b * bd:(b + 1) * bd]
        rs.append(jnp.dot(xblk, wrb[b], preferred_element_type=F32))
        is_.append(jnp.dot(xblk, wib[b], preferred_element_type=F32))
    r = _sigmoid(jnp.concatenate(rs, axis=1) + br_ref[...])
    ig = _sigmoid(jnp.concatenate(is_, axis=1) + bi_ref[...])
    log_a = -LRU_C * r * jax.nn.softplus(-lam_ref[...])
    a = jnp.exp(log_a)
    a_sc[...] = a
    u_sc[...] = jnp.sqrt(-jnp.tanh(log_a) * (a * a + 1.0)) * (ig * xc)

    def step(i, h):
        h = a_sc[pl.ds(i, 1), :] * h + u_sc[pl.ds(i, 1), :]
        hs_sc[pl.ds(i, 1), :] = h
        return h

    h = lax.fori_loop(0, tl, step, hcar[...], unroll=8)
    hcar[...] = h
    last_ref[...] = h
    y_ref[...] = (hs_sc[...] * jax.nn.gelu(gate_ref[...])).astype(y_ref.dtype)


def lru_core(n, l, gate, xb, conv_buf8, h0, p, tl, out_dtype, conv_done):
    wd = gate.shape[1]
    nt = l // tl
    bd = wd // LRU_BLOCKS
    vec = lambda a: a.reshape(1, wd)
    cst2 = lambda b, t: (0, 0)
    in_specs = [pl.BlockSpec((tl, wd), lambda b, t: (b * nt + t, 0)),
                pl.BlockSpec((tl, wd), lambda b, t: (b * nt + t, 0)),
                pl.BlockSpec((None, 8, wd), lambda b, t: (b, 0, 0)),
                pl.BlockSpec((4, wd), cst2), pl.BlockSpec((1, wd), cst2),
                pl.BlockSpec((LRU_BLOCKS, bd, bd), lambda b, t: (0, 0, 0)),
                pl.BlockSpec((LRU_BLOCKS, bd, bd), lambda b, t: (0, 0, 0)),
                pl.BlockSpec((1, wd), cst2), pl.BlockSpec((1, wd), cst2), pl.BlockSpec((1, wd), cst2)]
    args = [gate, xb, conv_buf8, p["lru_conv_w"][0], vec(p["lru_conv_b"]), p["lru_w_r"][0], p["lru_w_i"][0],
            vec(p["lru_b_r"]), vec(p["lru_b_i"]), vec(p["lru_lam"])]
    if h0 is not None:
        in_specs.append(pl.BlockSpec((None, 1, wd), lambda b, t: (b, 0, 0)))
        args.append(h0.reshape(n, 1, wd))
    y, last = pl.pallas_call(
        functools.partial(_lru_kernel, tl=tl, has_state=h0 is not None, conv_done=conv_done),
        grid=(n, nt),
        in_specs=in_specs,
        out_specs=[pl.BlockSpec((tl, wd), lambda b, t: (b * nt + t, 0)),
                   pl.BlockSpec((None, 1, wd), lambda b, t: (b, 0, 0))],
        out_shape=[jax.ShapeDtypeStruct((n * l, wd), out_dtype), jax.ShapeDtypeStruct((n, 1, wd), F32)],
        scratch_shapes=[pltpu.VMEM((8 + tl, wd), F32), pltpu.VMEM((8, wd), F32),
                        pltpu.VMEM((tl, wd), F32), pltpu.VMEM((tl, wd), F32), pltpu.VMEM((tl, wd), F32),
                        pltpu.VMEM((1, wd), F32),
                        pltpu.VMEM((LRU_BLOCKS, bd, bd), BF16), pltpu.VMEM((LRU_BLOCKS, bd, bd), BF16)],
        compiler_params=_cparams("arbitrary", "arbitrary"),
        name="lru_core",
    )(*args)
    return y, last.reshape(n, wd)


def _pad_buf8(buf):
    return jnp.pad(buf, ((0, 0), (5, 0), (0, 0)))


def _trunk(seq, x, mod, pos, cache, p):
    n, l, m = seq.n, seq.l, seq.m
    prompt = seq.prompt
    d = x.shape[1]
    act = BF16 if prompt else F32
    new = {}
    reps = 1 if prompt else n

    h = norm_mod(seq, x, p["g_mix"], 0, mod, 0)
    rope64 = rope_tables(pos, SWA_HD, reps)
    nq = SWA_QH * SWA_HD
    q = matmul(seq, h, p["swa_w_qkv"], 0, col0=0, ncols=nq, tn=1024, out_dtype=act, epi="rope", rope=rope64,
               n_rope=nq // 1024, hd=SWA_HD, name="swa_q")
    tm_kv = 2 * seq.tm if prompt else seq.tm
    kv = matmul(seq, h, p["swa_w_qkv"], 0, col0=nq, tn=256, tm=tm_kv, out_dtype=F32, epi="rope", rope=rope64,
                n_rope=1, hd=SWA_HD, name="swa_kv")
    akw = dict(d=1, hd=SWA_HD, kvh=SWA_KVH, grp=SWA_QH // SWA_KVH, q_blk=0, k_blk=0, v_blk=1,
               sinks=p["swa_sinks"][0], want_lse=False)
    if prompt:
        o, _ = attn_prompt(n, l, q, kv, out_dtype=BF16, **akw)
        keep = min(SWA_WINDOW, l)
        new["swa_kv"] = kv.reshape(n, l, -1)[:, l - keep:].reshape(n, keep, 2, SWA_KVH, SWA_HD)
    else:
        c = cache["swa_kv"][0]
        o, _ = attn_sample(n, l, q, kv, c.reshape(n, c.shape[1], 2 * SWA_KVH, SWA_HD), **akw)
        new["swa_kv"] = kv.reshape(n, l, 2, SWA_KVH, SWA_HD)
    x = matmul(seq, o, p["swa_w_o"], 0, tn=1024, out_dtype=F32, epi="resid", res=x, mod=mod, mod_layer=0, mod_k=2,
               name="swa_o")
    x = ffn(seq, norm_mod(seq, x, p["g_ffn"], 0, mod, 3), p["w_ff1"], p["w_ff2"], 0, x, mod)

    h = norm_mod(seq, x, p["g_mix"], 1, mod, 0)
    nzx = SSM_DI + SSM_DI + 2 * SSM_G * SSM_S
    w_dt = p["ssd_w_in"][:, :, nzx:]
    dt = matmul(seq, h, w_dt, 0, tn=SSM_HEADS, out_dtype=F32, bias=p["ssd_dt_bias"], epi="softplus", name="ssd_dt")
    if prompt:
        zx = matmul_stream(h, p["ssd_w_in"], 0, ncols=nzx, tn=1024, tm=seq.tm, out_dtype=F32, name="ssd_in")
    else:
        zx = matmul(seq, h, p["ssd_w_in"], 0, col0=0, ncols=nzx, tn=1024, out_dtype=F32, name="ssd_in")
    zx3 = zx.reshape(n, l, nzx)
    if prompt:
        buf8 = jnp.zeros((n, 8, nzx - SSM_DI), F32)
        h0 = None
        new["ssd_conv"] = zx3[:, l - 3:, SSM_DI:]
    else:
        buf8 = _pad_buf8(cache["ssd_conv"][0])
        h0 = cache["ssd"][0]
        new["ssd_conv"] = jnp.concatenate([cache["ssd_conv"][0], zx3[:, :, SSM_DI:]], axis=1)[:, -3:]
    y, new["ssd"] = ssd_core(n, l, zx, dt, buf8, h0, p, act, ng=SSM_G)
    x = matmul(seq, y, p["ssd_w_out"], 0, tn=512, out_dtype=F32, epi="resid", res=x, mod=mod,
               mod_layer=1, mod_k=2, name="ssd_out")
    x = ffn(seq, norm_mod(seq, x, p["g_ffn"], 1, mod, 3), p["w_ff1"], p["w_ff2"], 1, x, mod)

    h = norm_mod(seq, x, p["g_mix"], 2, mod, 0)
    rope128 = rope_tables(pos, DIL_HD, reps)
    ng = len(DIL_PATTERN)
    nq = ng * DIL_QH * DIL_HD
    q = matmul(seq, h, p["dil_w_qkv"], 0, col0=0, ncols=nq, tn=1024, out_dtype=F32, epi="rope", rope=rope128,
               n_rope=nq // 1024, hd=DIL_HD, name="dil_q")
    ck = DIL_KVH * DIL_HD
    grp = DIL_QH // DIL_KVH
    outs, lses = [], []
    for g, ((w, dil), key) in enumerate(zip(DIL_PATTERN, ("dil_kv_w128", "dil_kv_w512", "dil_kv_w2048"))):
        kv = matmul(seq, h, p["dil_w_qkv"], 0, col0=nq + g * ck, ncols=2 * ck, cstride=ng, tn=ck, tm=tm_kv,
                    out_dtype=F32, epi="rope", rope=rope128, n_rope=1, hd=DIL_HD, name="dil_kv")
        akw = dict(d=dil, hd=DIL_HD, kvh=DIL_KVH, grp=grp, q_blk=g, k_blk=0, v_blk=1)
        if not prompt:
            c = cache[key][0]
            o, lse = attn_sample(n, l, q, kv, c.reshape(n, c.shape[1], 2 * DIL_KVH, DIL_HD), **akw)
            new[key] = kv.reshape(n, l, 2, DIL_KVH, DIL_HD)
        else:
            if dil == 1:
                o, lse = attn_prompt(n, l, q, kv, **akw)
            else:
                o, lse = attn_dil_prompt(n, l, q, kv, d=dil, kvh=DIL_KVH, grp=grp, q_head0=g * DIL_QH,
                                         k_head0=0, v_head0=DIL_KVH)
            keep = min(w, l)
            new[key] = kv.reshape(n, l, -1)[:, l - keep:].reshape(n, keep, 2, DIL_KVH, DIL_HD)
        outs.append(o)
        lses.append(lse)
    o = dil_merge(outs, lses, seq.tm if not prompt else 512, act)
    x = matmul(seq, o, p["dil_w_o"], 0, tn=1024, out_dtype=F32, epi="resid", res=x, mod=mod, mod_layer=2, mod_k=2,
               name="dil_o")
    x = ffn(seq, norm_mod(seq, x, p["g_ffn"], 2, mod, 3), p["w_ff1"], p["w_ff2"], 2, x, mod)

    h = norm_mod(seq, x, p["g_mix"], 3, mod, 0)
    wd = p["lru_w_in"].shape[2] // 2
    if prompt:
        gate = matmul(seq, h, p["lru_w_in"], 0, col0=0, ncols=wd, tn=1024, out_dtype=F32, bias=p["lru_b_in"],
                      name="lru_in_gate")
        xb, tails = matmul_conv(seq, h, p["lru_w_in"], 0, col0=wd, ncols=wd, tn=1024, bias=p["lru_b_in"],
                                conv_w=p["lru_conv_w"][0], conv_b=p["lru_conv_b"].reshape(1, wd), conv_col0=0,
                                silu=False, name="lru_in_xb")
        buf8 = tails
        h0 = None
        new["lru_conv"] = tails[:, 5:, :]
    else:
        gx = matmul(seq, h, p["lru_w_in"], 0, tn=1024, out_dtype=F32, bias=p["lru_b_in"], name="lru_in")
        gate, xb = gx[:, :wd], gx[:, wd:]
        buf8 = _pad_buf8(cache["lru_conv"][0])
        h0 = cache["lru"][0]
        new["lru_conv"] = jnp.concatenate([cache["lru_conv"][0], xb.reshape(n, l, wd)], axis=1)[:, -3:]
    y, new["lru"] = lru_core(n, l, gate, xb, buf8, h0, p, 512 if prompt else l, act, conv_done=prompt)
    x = matmul(seq, y, p["lru_w_out"], 0, tn=1024, out_dtype=F32, epi="resid", res=x, mod=mod, mod_layer=3, mod_k=2,
               name="lru_out")
    x = ffn(seq, norm_mod(seq, x, p["g_ffn"], 3, mod, 3), p["w_ff1"], p["w_ff2"], 3, x, mod)

    y = final_norm(x, p["g_final"], seq.tm)
    return y.reshape(n, l, d), {k: v[None] for k, v in new.items()}


def kernel(x_prompt, x_sample, cache_swa_kv, state_ssd_conv, state_ssd, cache_dil_kv_w128, cache_dil_kv_w512,
           cache_dil_kv_w2048, state_lru_conv, state_lru, c_prompt, c_sample, w_ada, b_ada, g_mix, g_ffn,
           w_ff1, w_ff2, g_final, swa_w_qkv, swa_sinks, swa_w_o, ssd_w_in, ssd_conv_w, ssd_conv_b,
           ssd_dt_bias, ssd_a_log, ssd_d, ssd_norm, ssd_w_out, dil_w_qkv, dil_w_o, lru_w_in, lru_b_in,
           lru_conv_w, lru_conv_b, lru_w_r, lru_b_r, lru_w_i, lru_b_i, lru_lam, lru_w_out):
    p = dict(g_mix=g_mix, g_ffn=g_ffn, w_ff1=w_ff1.astype(BF16), w_ff2=w_ff2.astype(BF16), g_final=g_final,
             swa_w_qkv=swa_w_qkv, swa_sinks=swa_sinks, swa_w_o=swa_w_o,
             ssd_w_in=ssd_w_in, ssd_conv_w=ssd_conv_w, ssd_conv_b=ssd_conv_b, ssd_dt_bias=ssd_dt_bias,
             ssd_a_log=ssd_a_log, ssd_d=ssd_d, ssd_norm=ssd_norm, ssd_w_out=ssd_w_out,
             dil_w_qkv=dil_w_qkv, dil_w_o=dil_w_o,
             lru_w_in=lru_w_in, lru_b_in=lru_b_in, lru_conv_w=lru_conv_w, lru_conv_b=lru_conv_b,
             lru_w_r=lru_w_r, lru_b_r=lru_b_r, lru_w_i=lru_w_i, lru_b_i=lru_b_i, lru_lam=lru_lam,
             lru_w_out=lru_w_out)
    cache = dict(swa_kv=cache_swa_kv, ssd_conv=state_ssd_conv, ssd=state_ssd, dil_kv_w128=cache_dil_kv_w128,
                 dil_kv_w512=cache_dil_kv_w512, dil_kv_w2048=cache_dil_kv_w2048, lru_conv=state_lru_conv,
                 lru=state_lru)
    nb, l, d = x_prompt.shape
    ns, ls, _ = x_sample.shape
    depth = w_ada.shape[0]
    rows = -(-(nb + ns) // 16) * 16
    c_all = jnp.concatenate([c_prompt, c_sample, jnp.zeros((rows - nb - ns, d), F32)], axis=0)
    mod = ada_mod(c_all, w_ada, b_ada)
    mod_p = mod.reshape(depth, rows, 1, 6 * d)
    mod_s = jnp.repeat(mod[:, nb:nb + ns], ls, axis=1).reshape(depth, 1, ns * ls, 6 * d)

    seq_p = Seq(nb, l, 1024, True)
    seq_s = Seq(ns, ls, ns * ls, False)
    y_p, sp = _trunk(seq_p, x_prompt.reshape(nb * l, d), mod_p, jnp.arange(l, dtype=jnp.int32), None, p)
    y_s, ss = _trunk(seq_s, x_sample.reshape(ns * ls, d), mod_s, PAST_LEN + jnp.arange(ls, dtype=jnp.int32), cache, p)
    return (y_p, y_s,
            sp["swa_kv"], ss["swa_kv"],
            sp["ssd_conv"], ss["ssd_conv"],
            sp["ssd"], ss["ssd"],
            sp["dil_kv_w128"], ss["dil_kv_w128"],
            sp["dil_kv_w512"], ss["dil_kv_w512"],
            sp["dil_kv_w2048"], ss["dil_kv_w2048"],
            sp["lru_conv"], ss["lru_conv"],
            sp["lru"], ss["lru"])
```
